```python
import math
import jax, jax.numpy as jnp
from jax import lax
import numpy as np

D_MODEL = 2048
BATCH = 1
SEQ = 8192
DEPTH = 4

HEAD_DIM = 128
ROPE_THETA = 10000.0
LN_EPS = 1e-5
BLOCK = 128
ALPHA = (2 * DEPTH) ** 0.25
BETA = (8 * DEPTH) ** -0.25

POOL_WINDOWS = (2, 4, 8, 16)
POOL_GROUP_DIM = D_MODEL // 8
POOL_DIM = len(POOL_WINDOWS) * POOL_GROUP_DIM

DIL_PATTERNS = ((128, 1), (512, 4), (2048, 16))
DIL_HEADS_PER_GROUP = D_MODEL // 256
DIL_HEADS = len(DIL_PATTERNS) * DIL_HEADS_PER_GROUP
DIL_QKV = DIL_HEADS * HEAD_DIM
DIL_OUT = DIL_HEADS_PER_GROUP * HEAD_DIM
EVEN_IN = POOL_DIM + 3 * DIL_QKV
EVEN_MIX = POOL_DIM + DIL_OUT

DSA_HEADS = D_MODEL // HEAD_DIM
IDX_HEADS = 16
IDX_DIM = 64
DSA_TOPK = 256
DSA_Q = DSA_HEADS * HEAD_DIM
ODD_IN = DSA_Q + 2 * HEAD_DIM + IDX_HEADS * IDX_DIM + IDX_DIM + IDX_HEADS
ODD_MIX = DSA_Q

D_FF = 5632
N_EXPERTS = 8
TOP_K = 2

N_EVEN = (DEPTH + 1) // 2
N_ODD = DEPTH // 2

kernel_name = "hybrid_pool_dilated_dsa_moe_deepnorm"


def layer_norm(x, g, b):
    xf = x.astype(jnp.float32)
    mu = jnp.mean(xf, axis=-1, keepdims=True)
    var = jnp.mean(jnp.square(xf - mu), axis=-1, keepdims=True)
    y = (xf - mu) * lax.rsqrt(var + LN_EPS) * g.astype(jnp.float32) + b.astype(jnp.float32)
    return y.astype(x.dtype)


def rope(x):
    T, hd = x.shape[1], x.shape[-1]
    inv_freq = ROPE_THETA ** (-jnp.arange(0, hd, 2, dtype=jnp.float32) / hd)
    ang = jnp.arange(T, dtype=jnp.float32)[:, None] * inv_freq[None, :]
    cos = jnp.cos(ang)[None, :, None, :]
    sin = jnp.sin(ang)[None, :, None, :]
    xf = x.astype(jnp.float32)
    x1, x2 = xf[..., : hd // 2], xf[..., hd // 2:]
    out = jnp.concatenate([x1 * cos - x2 * sin, x2 * cos + x1 * sin], axis=-1)
    return out.astype(x.dtype)


def pool_mixer(u, pool_w, pool_scale):
    B, T, _ = u.shape
    uf = u.astype(jnp.float32).reshape(B, T, len(POOL_WINDOWS), POOL_GROUP_DIM)
    cs = jnp.cumsum(uf, axis=1)
    t = jnp.arange(T)
    pooled = []
    for g, w in enumerate(POOL_WINDOWS):
        c = cs[:, :, g]
        prev = jnp.pad(c, ((0, 0), (w, 0), (0, 0)))[:, :T]
        cnt = jnp.minimum(t + 1, w).astype(jnp.float32)[None, :, None]
        pooled.append((c - prev) / cnt - uf[:, :, g])
    p = jnp.stack(pooled, axis=2).astype(u.dtype)
    y = jnp.einsum('btgc,gcd->btgd', p, pool_w)
    return y.reshape(B, T, POOL_DIM) * pool_scale


def dilated_window_attention(q, k, v, window, dil):
    B, T, H, hd = q.shape
    count = window // dil
    assert count <= BLOCK
    sub = -(-T // dil)
    nblk = -(-sub // BLOCK)
    Tp = nblk * BLOCK * dil

    def to_blocks(a):
        a = jnp.pad(a, ((0, 0), (0, Tp - T), (0, 0), (0, 0)))
        a = a.reshape(B, nblk * BLOCK, dil, H, hd).transpose(0, 2, 1, 3, 4)
        return a.reshape(B * dil, nblk, BLOCK, H, hd)

    def with_prev(a):
        prev = jnp.pad(a[:, :-1], ((0, 0), (1, 0), (0, 0), (0, 0), (0, 0)))
        return jnp.concatenate([prev, a], axis=2)

    qb = to_blocks(q)
    kc = with_prev(to_blocks(k))
    vc = with_prev(to_blocks(v))
    s = jnp.einsum('znqhd,znkhd->znhqk', qb, kc).astype(jnp.float32) * (hd ** -0.5)
    n = jnp.arange(nblk)[:, None, None]
    qi = jnp.arange(BLOCK)[None, :, None]
    kj = jnp.arange(2 * BLOCK)[None, None, :]
    rel = BLOCK + qi - kj
    mask = (rel >= 0) & (rel <= count) & (n * BLOCK + kj >= BLOCK)
    s = jnp.where(mask[None, :, None], s, -jnp.inf)
    m = jnp.max(s, axis=-1)
    p = jnp.exp(s - m[..., None])
    l = jnp.sum(p, axis=-1)
    o = jnp.einsum('znhqk,znkhd->znqhd', p.astype(v.dtype), vc).astype(jnp.float32)
    o = o / jnp.swapaxes(l, 2, 3)[..., None]

    def from_blocks(a):
        rest = a.shape[3:]
        a = a.reshape(B, dil, nblk * BLOCK, *rest)
        a = jnp.moveaxis(a, 1, 2).reshape(B, Tp, *rest)
        return a[:, :T]

    return (from_blocks(o), from_blocks(jnp.swapaxes(m, 2, 3)), from_blocks(jnp.swapaxes(l, 2, 3)))


def mixture_of_dilations(q, k, v):
    B, T = q.shape[:2]
    os_, ms, ls = [], [], []
    for g, (w, d) in enumerate(DIL_PATTERNS):
        sl = slice(g * DIL_HEADS_PER_GROUP, (g + 1) * DIL_HEADS_PER_GROUP)
        o, m, l = dilated_window_attention(q[:, :, sl], k[:, :, sl], v[:, :, sl], w, d)
        os_.append(o); ms.append(m); ls.append(l)
    o = jnp.stack(os_, 0)
    m = jnp.stack(ms, 0)
    l = jnp.stack(ls, 0)
    wgt = l * jnp.exp(m - jnp.max(m, axis=0, keepdims=True))
    out = jnp.sum(wgt[..., None] * o, axis=0) / jnp.sum(wgt, axis=0)[..., None]
    return out.reshape(B, T, DIL_OUT).astype(q.dtype)


def dsa_attention(q, k, v, iq, ik, iw):
    B, T, H, hd = q.shape
    ksel = min(DSA_TOPK, T // 4)
    nblk = T // BLOCK
    keypos = jnp.arange(T)
    bidx = jnp.arange(B)[:, None, None]

    def blocks(a):
        a = a.reshape(B, nblk, BLOCK, *a.shape[2:])
        return jnp.moveaxis(a, 1, 0)

    def one_block(args):
        qb, iqb, iwb, n = args
        qpos = n * BLOCK + jnp.arange(BLOCK)
        logits = jnp.einsum('bqhc,bsc->bqhs', iqb, ik).astype(jnp.float32) * (IDX_DIM ** -0.5)
        score = jnp.einsum('bqhs,bqh->bqs', jax.nn.relu(logits), iwb.astype(jnp.float32))
        causal = keypos[None, :] <= qpos[:, None]
        score = jnp.where(causal[None], score, -jnp.inf)
        _, idx = lax.top_k(score, ksel)
        valid = idx <= qpos[None, :, None]
        kg = k[bidx, idx]
        vg = v[bidx, idx]
        s = jnp.einsum('bqhd,bqkd->bhqk', qb, kg).astype(jnp.float32) * (hd ** -0.5)
        s = jnp.where(valid[:, None], s, -jnp.inf)
        p = jax.nn.softmax(s, axis=-1).astype(v.dtype)
        return jnp.einsum('bhqk,bqkd->bqhd', p, vg)

    o = lax.map(one_block, (blocks(q), blocks(iq), blocks(iw), jnp.arange(nblk)))
    return jnp.moveaxis(o, 0, 1).reshape(B, T, H * hd)


def swiglu(h, wg, wu, wd):
    return (jax.nn.silu(h @ wg) * (h @ wu)) @ wd


def moe_swiglu(h, router, wg, wu, wd):
    logits = (h @ router).astype(jnp.float32)
    top_vals, top_idx = lax.top_k(logits, TOP_K)
    gates = jax.nn.softmax(top_vals, axis=-1)
    combine = jnp.sum(jax.nn.one_hot(top_idx, N_EXPERTS, dtype=jnp.float32) * gates[..., None], axis=-2)
    y = jnp.zeros_like(h)
    for e in range(N_EXPERTS):
        y = y + combine[..., e:e + 1].astype(h.dtype) * swiglu(h, wg[e], wu[e], wd[e])
    return y


def even_layer(x, w_in, pool_w, pool_scale, w_out, ln1_g, ln1_b, wg, wu, wd, ln2_g, ln2_b):
    B, T, _ = x.shape
    proj = x @ w_in
    u = proj[..., :POOL_DIM]
    q = proj[..., POOL_DIM:POOL_DIM + DIL_QKV].reshape(B, T, DIL_HEADS, HEAD_DIM)
    k = proj[..., POOL_DIM + DIL_QKV:POOL_DIM + 2 * DIL_QKV].reshape(B, T, DIL_HEADS, HEAD_DIM)
    v = proj[..., POOL_DIM + 2 * DIL_QKV:].reshape(B, T, DIL_HEADS, HEAD_DIM)
    attn = mixture_of_dilations(rope(q), rope(k), v)
    mix = jnp.concatenate([pool_mixer(u, pool_w, pool_scale), attn], axis=-1) @ w_out
    x = layer_norm(ALPHA * x + mix, ln1_g, ln1_b)
    return layer_norm(ALPHA * x + swiglu(x, wg, wu, wd), ln2_g, ln2_b)


def odd_layer(x, w_in, w_out, ln1_g, ln1_b, router, wg, wu, wd, ln2_g, ln2_b):
    B, T, _ = x.shape
    proj = x @ w_in
    o = 0
    q = proj[..., o:o + DSA_Q].reshape(B, T, DSA_HEADS, HEAD_DIM); o += DSA_Q
    k = proj[..., o:o + HEAD_DIM]; o += HEAD_DIM
    v = proj[..., o:o + HEAD_DIM]; o += HEAD_DIM
    iq = proj[..., o:o + IDX_HEADS * IDX_DIM].reshape(B, T, IDX_HEADS, IDX_DIM); o += IDX_HEADS * IDX_DIM
    ik = proj[..., o:o + IDX_DIM]; o += IDX_DIM
    iw = proj[..., o:o + IDX_HEADS] * (IDX_HEADS ** -0.5)
    attn = dsa_attention(rope(q), rope(k[:, :, None])[:, :, 0], v, iq, ik, iw)
    x = layer_norm(ALPHA * x + attn @ w_out, ln1_g, ln1_b)
    return layer_norm(ALPHA * x + moe_swiglu(x, router, wg, wu, wd), ln2_g, ln2_b)


def setup_inputs(seed: int = 0) -> dict:
    key = jax.random.key(seed)
    ks = jax.random.split(key, 24)
    f32 = jnp.float32
    D = D_MODEL

    def nrm(k, shape, scale):
        return jax.random.normal(k, shape, f32) * scale

    ev_col = jnp.concatenate([jnp.full((POOL_DIM,), BETA, f32), jnp.ones((2 * DIL_QKV,), f32),
                              jnp.full((DIL_QKV,), BETA, f32)])
    od_col = jnp.concatenate([jnp.ones((DSA_Q + HEAD_DIM,), f32), jnp.full((HEAD_DIM,), BETA, f32),
                              jnp.ones((ODD_IN - DSA_Q - 2 * HEAD_DIM,), f32)])
    return {
        "x": jax.random.normal(ks[0], (BATCH, SEQ, D), f32),
        "ev_w_in": nrm(ks[1], (N_EVEN, D, EVEN_IN), D ** -0.5) * ev_col,
        "ev_pool_w": nrm(ks[2], (N_EVEN, len(POOL_WINDOWS), POOL_GROUP_DIM, POOL_GROUP_DIM), POOL_GROUP_DIM ** -0.5),
        "ev_pool_scale": 1.0 + nrm(ks[3], (N_EVEN, POOL_DIM), 0.02),
        "ev_w_out": nrm(ks[4], (N_EVEN, EVEN_MIX, D), EVEN_MIX ** -0.5 * BETA),
        "ev_ln1_g": 1.0 + nrm(ks[5], (N_EVEN, D), 0.02),
        "ev_ln1_b": nrm(ks[6], (N_EVEN, D), 0.02),
        "ev_ffn_w_gate": nrm(ks[7], (N_EVEN, D, D_FF), D ** -0.5),
        "ev_ffn_w_up": nrm(ks[8], (N_EVEN, D, D_FF), D ** -0.5),
        "ev_ffn_w_down": nrm(ks[9], (N_EVEN, D_FF, D), D_FF ** -0.5 * BETA),
        "ev_ln2_g": 1.0 + nrm(ks[10], (N_EVEN, D), 0.02),
        "ev_ln2_b": nrm(ks[11], (N_EVEN, D), 0.02),
        "od_w_in": nrm(ks[12], (N_ODD, D, ODD_IN), D ** -0.5) * od_col,
        "od_w_out": nrm(ks[13], (N_ODD, ODD_MIX, D), ODD_MIX ** -0.5 * BETA),
        "od_ln1_g": 1.0 + nrm(ks[14], (N_ODD, D), 0.02),
        "od_ln1_b": nrm(ks[15], (N_ODD, D), 0.02),
        "od_router": nrm(ks[16], (N_ODD, D, N_EXPERTS), D ** -0.5),
        "od_exp_w_gate": nrm(ks[17], (N_ODD, N_EXPERTS, D, D_FF), D ** -0.5),
        "od_exp_w_up": nrm(ks[18], (N_ODD, N_EXPERTS, D, D_FF), D ** -0.5),
        "od_exp_w_down": nrm(ks[19], (N_ODD, N_EXPERTS, D_FF, D), D_FF ** -0.5 * BETA),
        "od_ln2_g": 1.0 + nrm(ks[20], (N_ODD, D), 0.02),
        "od_ln2_b": nrm(ks[21], (N_ODD, D), 0.02),
    }


def reference(x, ev_w_in, ev_pool_w, ev_pool_scale, ev_w_out, ev_ln1_g, ev_ln1_b,
              ev_ffn_w_gate, ev_ffn_w_up, ev_ffn_w_down, ev_ln2_g, ev_ln2_b,
              od_w_in, od_w_out, od_ln1_g, od_ln1_b, od_router,
              od_exp_w_gate, od_exp_w_up, od_exp_w_down, od_ln2_g, od_ln2_b):
    for layer in range(DEPTH):
        i = layer // 2
        if layer % 2 == 0:
            x = even_layer(x, ev_w_in[i], ev_pool_w[i], ev_pool_scale[i], ev_w_out[i],
                           ev_ln1_g[i], ev_ln1_b[i], ev_ffn_w_gate[i], ev_ffn_w_up[i],
                           ev_ffn_w_down[i], ev_ln2_g[i], ev_ln2_b[i])
        else:
            x = odd_layer(x, od_w_in[i], od_w_out[i], od_ln1_g[i], od_ln1_b[i], od_router[i],
                          od_exp_w_gate[i], od_exp_w_up[i], od_exp_w_down[i],
                          od_ln2_g[i], od_ln2_b[i])
    return x
```

```python
import functools

import jax
import jax.numpy as jnp
from jax import lax
from jax.experimental import pallas as pl
from jax.experimental.pallas import tpu as pltpu

F32 = jnp.float32
BF16 = jnp.bfloat16
I32 = jnp.int32

D_MODEL = 2048
DEPTH = 4
HEAD_DIM = 128
ROPE_THETA = 10000.0
LN_EPS = 1e-5
BLOCK = 128
ALPHA = (2 * DEPTH) ** 0.25
POOL_WINDOWS = (2, 4, 8, 16)
POOL_GROUP_DIM = D_MODEL // 8
POOL_DIM = len(POOL_WINDOWS) * POOL_GROUP_DIM
DIL_PATTERNS = ((128, 1), (512, 4), (2048, 16))
DIL_HEADS_PER_GROUP = D_MODEL // 256
DIL_HEADS = len(DIL_PATTERNS) * DIL_HEADS_PER_GROUP
DIL_QKV = DIL_HEADS * HEAD_DIM
DIL_OUT = DIL_HEADS_PER_GROUP * HEAD_DIM
DSA_HEADS = D_MODEL // HEAD_DIM
IDX_HEADS = 16
IDX_DIM = 64
DSA_TOPK = 256
DSA_Q = DSA_HEADS * HEAD_DIM
D_FF = 5632
N_EXPERTS = 8
TOP_K = 2

LANES = 128
VMEM_LIMIT = 56 * 1024 * 1024

NEG = -1e30
INT_MIN = -(2 ** 31)


def _cparams(sem):
    return pltpu.CompilerParams(dimension_semantics=sem, vmem_limit_bytes=VMEM_LIMIT)


def _layer_norm(z, g, b):
    mu = jnp.mean(z, axis=-1, keepdims=True)
    zc = z - mu
    var = jnp.mean(zc * zc, axis=-1, keepdims=True)
    return zc * lax.rsqrt(var + LN_EPS) * g + b


def _proj_kernel(x_ref, w_ref, cos_ref, sin_ref, o_ref, *, rope_cols, n_tiles, head_major):
    acc = jnp.dot(x_ref[...], w_ref[...].astype(BF16), preferred_element_type=F32)
    tn = acc.shape[1]
    nh = tn // LANES

    def write(rope_heads):
        for c in range(nh):
            a = acc[:, c * LANES:(c + 1) * LANES]
            if c < rope_heads:
                a = a * cos_ref[...] + pltpu.roll(a, HEAD_DIM // 2, 1) * sin_ref[...]
            if head_major:
                o_ref[c] = a.astype(o_ref.dtype)
            else:
                o_ref[:, c * LANES:(c + 1) * LANES] = a.astype(o_ref.dtype)

    if n_tiles == 1:
        write(rope_cols // LANES)
    else:
        rope_tiles = rope_cols // tn
        j = pl.program_id(0)
        pl.when(j < rope_tiles)(lambda: write(nh))
        pl.when(j >= rope_tiles)(lambda: write(0))


def _proj(xb, w, cos2, sin2, *, col0, ncols, tn, rope_cols, out_dtype, head_major=False, tm=1024):
    T, K = xb.shape
    tm = min(tm, T)
    assert T % tm == 0 and ncols % tn == 0 and col0 % tn == 0 and tn % LANES == 0
    assert rope_cols % (LANES if ncols == tn else tn) == 0
    n_tiles = ncols // tn
    c0 = col0 // tn
    if head_major:
        out_shape = jax.ShapeDtypeStruct((ncols // LANES, T, LANES), out_dtype)
        out_spec = pl.BlockSpec((tn // LANES, tm, LANES), lambda j, i: (j, i, 0))
    else:
        out_shape = jax.ShapeDtypeStruct((T, ncols), out_dtype)
        out_spec = pl.BlockSpec((tm, tn), lambda j, i: (i, j))
    return pl.pallas_call(
        functools.partial(_proj_kernel, rope_cols=rope_cols, n_tiles=n_tiles, head_major=head_major),
        out_shape=out_shape,
        grid=(n_tiles, T // tm),
        in_specs=[
            pl.BlockSpec((tm, K), lambda j, i: (i, 0)),
            pl.BlockSpec((K, tn), lambda j, i: (0, j + c0)),
            pl.BlockSpec((tm, LANES), lambda j, i: (i, 0)),
            pl.BlockSpec((tm, LANES), lambda j, i: (i, 0)),
        ],
        out_specs=out_spec,
        compiler_params=_cparams(("arbitrary", "arbitrary")),
        name="proj",
    )(xb, w, cos2, sin2)


def _rope_tables(T):
    inv_freq = ROPE_THETA ** (-jnp.arange(0, HEAD_DIM, 2, dtype=F32) / HEAD_DIM)
    ang = jnp.arange(T, dtype=F32)[:, None] * inv_freq[None, :]
    cos, sin = jnp.cos(ang), jnp.sin(ang)
    return jnp.concatenate([cos, cos], axis=-1), jnp.concatenate([-sin, sin], axis=-1)


POOL_HALO = 16


def _pool_kernel(u_ref, h_ref, w_ref, sc_ref, o_ref):
    i = pl.program_id(0)
    tm = u_ref.shape[0]
    cur = u_ref[...]
    halo = jnp.where(i > 0, h_ref[...], 0.0)
    cat = jnp.concatenate([halo, cur], axis=0)
    t = i * tm + lax.broadcasted_iota(I32, (tm, 1), 0)
    G = POOL_GROUP_DIM
    for g, w in enumerate(POOL_WINDOWS):
        s = cat[:, g * G:(g + 1) * G]
        k = 1
        while k < w:
            s = s + pltpu.roll(s, k, 0)
            k *= 2
        win = s[POOL_HALO:]
        cnt = jnp.minimum(t + 1, w).astype(F32)
        p = win / cnt - cur[:, g * G:(g + 1) * G]
        y = jnp.dot(p.astype(BF16), w_ref[g].astype(BF16), preferred_element_type=F32)
        o_ref[:, g * G:(g + 1) * G] = (y * sc_ref[:, g * G:(g + 1) * G]).astype(o_ref.dtype)


def _pool(u, pool_w, pool_scale, tm=512):
    T = u.shape[0]
    tm = min(tm, T)
    assert T % tm == 0 and tm % POOL_HALO == 0 and max(POOL_WINDOWS) <= POOL_HALO
    hb = tm // POOL_HALO
    return pl.pallas_call(
        _pool_kernel,
        out_shape=jax.ShapeDtypeStruct((T, POOL_DIM), BF16),
        grid=(T // tm,),
        in_specs=[
            pl.BlockSpec((tm, POOL_DIM), lambda i: (i, 0)),
            pl.BlockSpec((POOL_HALO, POOL_DIM), lambda i: (jnp.maximum(i * hb - 1, 0), 0)),
            pl.BlockSpec(pool_w.shape, lambda i: (0, 0, 0)),
            pl.BlockSpec((1, POOL_DIM), lambda i: (0, 0)),
        ],
        out_specs=pl.BlockSpec((tm, POOL_DIM), lambda i: (i, 0)),
        compiler_params=_cparams(("arbitrary",)),
        name="pool",
    )(u, u, pool_w, pool_scale.reshape(1, POOL_DIM))


DIL_SPAN = BLOCK * max(d for _, d in DIL_PATTERNS)


def _dil_kernel(*refs):
    ng = len(DIL_PATTERNS)
    in_refs = refs[:5 * ng]
    o_ref = refs[5 * ng]
    kf_ref, vf_ref, os_ref, ms_ref, ls_ref = refs[5 * ng + 1:]
    b = pl.program_id(1)
    scale = HEAD_DIM ** -0.5

    qi = lax.broadcasted_iota(I32, (BLOCK, 2 * BLOCK), 0)
    kj = lax.broadcasted_iota(I32, (BLOCK, 2 * BLOCK), 1)
    rel = BLOCK + qi - kj
    for g, (window, d) in enumerate(DIL_PATTERNS):
        q_ref, kc_ref, kp_ref, vc_ref, vp_ref = in_refs[5 * g:5 * g + 5]
        count = window // d
        band = (rel >= 0) & (rel <= count)
        bias_std = jnp.where(band, 0.0, NEG)
        bias_first = jnp.where(band & (kj >= BLOCK), 0.0, NEG)
        hist = BLOCK * d
        nsb = DIL_SPAN // hist
        kf_ref[0:hist, :] = kp_ref[...]
        kf_ref[hist:hist + DIL_SPAN, :] = kc_ref[...]
        vf_ref[0:hist, :] = vp_ref[...]
        vf_ref[hist:hist + DIL_SPAN, :] = vc_ref[...]

        def body(idx, _, g=g, d=d, hist=hist, nsb=nsb, q_ref=q_ref, bias_std=bias_std, bias_first=bias_first):
            r = idx // nsb
            sb = idx % nsb
            start = r + hist * sb
            q = q_ref[pl.ds(start, BLOCK, stride=d), :].astype(BF16)
            k = kf_ref[pl.ds(start, 2 * BLOCK, stride=d), :].astype(BF16)
            v = vf_ref[pl.ds(start, 2 * BLOCK, stride=d), :].astype(BF16)
            s = lax.dot_general(q, k, (((1,), (1,)), ((), ())), preferred_element_type=F32) * scale
            first = jnp.logical_and(b == 0, sb == 0)
            s = s + jnp.where(first, bias_first, bias_std)
            m = jnp.max(s, axis=-1, keepdims=True)
            p = jnp.exp(s - m)
            l = jnp.sum(p, axis=-1, keepdims=True)
            o = jnp.dot(p.astype(BF16), v, preferred_element_type=F32)
            os_ref[g, pl.ds(start, BLOCK, stride=d), :] = o
            ms_ref[g, pl.ds(start, BLOCK, stride=d), :] = jnp.broadcast_to(m, (BLOCK, LANES))
            ls_ref[g, pl.ds(start, BLOCK, stride=d), :] = jnp.broadcast_to(l, (BLOCK, LANES))
            return 0

        lax.fori_loop(0, d * nsb, body, 0)

    mx = ms_ref[0]
    for g in range(1, ng):
        mx = jnp.maximum(mx, ms_ref[g])
    num = jnp.zeros((DIL_SPAN, LANES), F32)
    den = jnp.zeros((DIL_SPAN, LANES), F32)
    for g in range(ng):
        e = jnp.exp(ms_ref[g] - mx)
        num = num + e * os_ref[g]
        den = den + e * ls_ref[g]
    o_ref[...] = (num / den).astype(o_ref.dtype)


def _dilated(qkv):
    T = qkv.shape[1]
    assert T % DIL_SPAN == 0 and HEAD_DIM == LANES
    H = DIL_HEADS_PER_GROUP
    in_specs, args = [], []
    max_hist = 0
    for g, (window, d) in enumerate(DIL_PATTERNS):
        assert window // d <= BLOCK and DIL_SPAN % (BLOCK * d) == 0
        hist = BLOCK * d
        max_hist = max(max_hist, hist)
        ratio = DIL_SPAN // hist

        def cur(j, b, base):
            return (base + j, b, 0)

        def prev(j, b, base, ratio=ratio):
            return (base + j, jnp.maximum(b * ratio - 1, 0), 0)

        qh, kh, vh = g * H, DIL_HEADS + g * H, 2 * DIL_HEADS + g * H
        in_specs += [
            pl.BlockSpec((None, DIL_SPAN, LANES), functools.partial(cur, base=qh)),
            pl.BlockSpec((None, DIL_SPAN, LANES), functools.partial(cur, base=kh)),
            pl.BlockSpec((None, hist, LANES), functools.partial(prev, base=kh)),
            pl.BlockSpec((None, DIL_SPAN, LANES), functools.partial(cur, base=vh)),
            pl.BlockSpec((None, hist, LANES), functools.partial(prev, base=vh)),
        ]
        args += [qkv] * 5
    ng = len(DIL_PATTERNS)
    return pl.pallas_call(
        _dil_kernel,
        out_shape=jax.ShapeDtypeStruct((T, DIL_OUT), BF16),
        grid=(H, T // DIL_SPAN),
        in_specs=in_specs,
        out_specs=pl.BlockSpec((DIL_SPAN, LANES), lambda j, b: (b, j)),
        scratch_shapes=[
            pltpu.VMEM((max_hist + DIL_SPAN, LANES), F32),
            pltpu.VMEM((max_hist + DIL_SPAN, LANES), F32),
            pltpu.VMEM((ng, DIL_SPAN, LANES), F32),
            pltpu.VMEM((ng, DIL_SPAN, LANES), F32),
            pltpu.VMEM((ng, DIL_SPAN, LANES), F32),
        ],
        compiler_params=_cparams(("arbitrary", "arbitrary")),
        name="dilated",
    )(*args)


def _out_ln_kernel(*refs, n_in):
    a_refs = refs[:n_in]
    w_refs = refs[n_in:2 * n_in]
    x_ref, g_ref, b_ref, o_ref, ob_ref = refs[2 * n_in:]
    acc = ALPHA * x_ref[...]
    for a_ref, w_ref in zip(a_refs, w_refs):
        acc = acc + jnp.dot(a_ref[...], w_ref[...], preferred_element_type=F32)
    y = _layer_norm(acc, g_ref[...], b_ref[...])
    o_ref[...] = y
    ob_ref[...] = y.astype(BF16)


def _out_ln(a_list, wb, x, g, b, tm=512):
    T, D = x.shape
    tm = min(tm, T)
    n_in = len(a_list)
    kk = a_list[0].shape[1]
    assert all(a.shape[1] == kk for a in a_list) and wb.shape[0] == n_in * kk and T % tm == 0
    w3 = wb.reshape(n_in, kk, D)
    in_specs = [pl.BlockSpec((tm, kk), lambda i: (i, 0)) for _ in range(n_in)]
    in_specs += [pl.BlockSpec((None, kk, D), functools.partial(lambda i, c: (c, 0, 0), c=c)) for c in range(n_in)]
    in_specs += [pl.BlockSpec((tm, D), lambda i: (i, 0)),
                 pl.BlockSpec((1, D), lambda i: (0, 0)),
                 pl.BlockSpec((1, D), lambda i: (0, 0))]
    return pl.pallas_call(
        functools.partial(_out_ln_kernel, n_in=n_in),
        out_shape=(jax.ShapeDtypeStruct((T, D), F32), jax.ShapeDtypeStruct((T, D), BF16)),
        grid=(T // tm,),
        in_specs=in_specs,
        out_specs=(pl.BlockSpec((tm, D), lambda i: (i, 0)), pl.BlockSpec((tm, D), lambda i: (i, 0))),
        compiler_params=_cparams(("arbitrary",)),
        name="out_ln",
    )(*a_list, *([w3] * n_in), x, g.reshape(1, D), b.reshape(1, D))


def _res_ln_kernel(x_ref, y_ref, g_ref, b_ref, o_ref, ob_ref):
    y = _layer_norm(ALPHA * x_ref[...] + y_ref[...], g_ref[...], b_ref[...])
    o_ref[...] = y
    ob_ref[...] = y.astype(BF16)


def _res_ln(x, y, g, b, tm=512):
    T, D = x.shape
    tm = min(tm, T)
    row = pl.BlockSpec((tm, D), lambda i: (i, 0))
    vec = pl.BlockSpec((1, D), lambda i: (0, 0))
    return pl.pallas_call(
        _res_ln_kernel,
        out_shape=(jax.ShapeDtypeStruct((T, D), F32), jax.ShapeDtypeStruct((T, D), BF16)),
        grid=(T // tm,),
        in_specs=[row, row, vec, vec],
        out_specs=(row, row),
        compiler_params=_cparams(("arbitrary",)),
        name="res_ln",
    )(x, y, g.reshape(1, D), b.reshape(1, D))


FFN_TM = 1024
FFN_TF = 256
FFN_CHUNK = 256


def _ffn_kernel(vt_ref, ve_ref, vlo_ref, vhi_ref, vok_ref,
                x_ref, wg_ref, wu_ref, wd_ref, gate_ref, o_ref, wgb, wub, wdb, *, use_gate):
    v = pl.program_id(0)
    f = pl.program_id(1)
    nf = pl.num_programs(1)
    lo = vlo_ref[v]
    hi = vhi_ref[v]

    @pl.when(jnp.logical_and(vok_ref[v] == 2, f == 0))
    def _():
        o_ref[...] = jnp.zeros(o_ref.shape, o_ref.dtype)

    @pl.when(hi > lo)
    def _():
        wgb[...] = wg_ref[...].astype(BF16)
        wub[...] = wu_ref[...].astype(BF16)
        wdb[...] = wd_ref[...].astype(BF16)

        def chunk(c, _):
            r0 = pl.multiple_of(c * FFN_CHUNK, FFN_CHUNK)
            x = x_ref[pl.ds(r0, FFN_CHUNK), :]
            g = jnp.dot(x, wgb[...], preferred_element_type=F32)
            u = jnp.dot(x, wub[...], preferred_element_type=F32)
            h = (g * jax.nn.sigmoid(g) * u).astype(BF16)
            y = jnp.dot(h, wdb[...], preferred_element_type=F32)

            @pl.when(f == 0)
            def _():
                o_ref[pl.ds(r0, FFN_CHUNK), :] = y

            @pl.when(f > 0)
            def _():
                o_ref[pl.ds(r0, FFN_CHUNK), :] += y

            if use_gate:
                @pl.when(f == nf - 1)
                def _():
                    o_ref[pl.ds(r0, FFN_CHUNK), :] = o_ref[pl.ds(r0, FFN_CHUNK), :] * gate_ref[pl.ds(r0, FFN_CHUNK), :]
            return 0

        lax.fori_loop(lo, hi, chunk, 0)


def _ffn(visits, xs, wg, wu, wd, gate, *, use_gate):
    vt, ve, vlo, vhi, vok = visits
    R, D = xs.shape
    F = wg.shape[2]
    assert R % FFN_TM == 0 and F % FFN_TF == 0 and FFN_TM % FFN_CHUNK == 0
    nf = F // FFN_TF
    V = vt.shape[0]

    def fidx(v, f, vok):
        return jnp.where(vok[v] == 1, f, nf - 1)

    return pl.pallas_call(
        functools.partial(_ffn_kernel, use_gate=use_gate),
        out_shape=jax.ShapeDtypeStruct((R, D), F32),
        grid_spec=pltpu.PrefetchScalarGridSpec(
            num_scalar_prefetch=5,
            grid=(V, nf),
            in_specs=[
                pl.BlockSpec((FFN_TM, D), lambda v, f, vt, ve, vlo, vhi, vok: (vt[v], 0)),
                pl.BlockSpec((None, D, FFN_TF), lambda v, f, vt, ve, vlo, vhi, vok: (ve[v], 0, fidx(v, f, vok))),
                pl.BlockSpec((None, D, FFN_TF), lambda v, f, vt, ve, vlo, vhi, vok: (ve[v], 0, fidx(v, f, vok))),
                pl.BlockSpec((None, FFN_TF, D), lambda v, f, vt, ve, vlo, vhi, vok: (ve[v], fidx(v, f, vok), 0)),
                pl.BlockSpec((FFN_TM, 1), lambda v, f, vt, ve, vlo, vhi, vok: (vt[v], 0)),
            ],
            out_specs=pl.BlockSpec((FFN_TM, D), lambda v, f, vt, ve, vlo, vhi, vok: (vt[v], 0)),
            scratch_shapes=[
                pltpu.VMEM((D, FFN_TF), BF16),
                pltpu.VMEM((D, FFN_TF), BF16),
                pltpu.VMEM((FFN_TF, D), BF16),
            ],
        ),
        compiler_params=_cparams(("arbitrary", "arbitrary")),
        name="ffn",
    )(vt, ve, vlo, vhi, vok, xs, wg, wu, wd, gate)


def _dense_visits(T):
    nt = T // FFN_TM
    cpt = FFN_TM // FFN_CHUNK
    z = jnp.zeros((nt,), I32)
    return (jnp.arange(nt, dtype=I32), z, z, jnp.full((nt,), cpt, I32), jnp.ones((nt,), I32))


def _router_kernel(x_ref, r_ref, o_ref):
    logits = jnp.dot(x_ref[...], r_ref[...], preferred_element_type=F32, precision=lax.Precision.HIGHEST)
    lane = lax.broadcasted_iota(I32, logits.shape, 1)
    logits = jnp.where(lane < N_EXPERTS, logits, -jnp.inf)
    v1 = jnp.max(logits, axis=-1, keepdims=True)
    i1 = jnp.min(jnp.where(logits == v1, lane, LANES), axis=-1, keepdims=True)
    rest = jnp.where(lane == i1, -jnp.inf, logits)
    v2 = jnp.max(rest, axis=-1, keepdims=True)
    i2 = jnp.min(jnp.where(rest == v2, lane, LANES), axis=-1, keepdims=True)
    e2 = jnp.exp(v2 - v1)
    g1 = 1.0 / (1.0 + e2)
    g2 = e2 / (1.0 + e2)
    out = jnp.where(lane == 0, i1.astype(F32), 0.0)
    out = jnp.where(lane == 1, i2.astype(F32), out)
    out = jnp.where(lane == 2, g1, out)
    out = jnp.where(lane == 3, g2, out)
    o_ref[...] = out


def _router(x, router, tm=512):
    T, D = x.shape
    tm = min(tm, T)
    rp = jnp.pad(router, ((0, 0), (0, LANES - N_EXPERTS)))
    return pl.pallas_call(
        _router_kernel,
        out_shape=jax.ShapeDtypeStruct((T, LANES), F32),
        grid=(T // tm,),
        in_specs=[pl.BlockSpec((tm, D), lambda i: (i, 0)), pl.BlockSpec((D, LANES), lambda i: (0, 0))],
        out_specs=pl.BlockSpec((tm, LANES), lambda i: (i, 0)),
        compiler_params=_cparams(("arbitrary",)),
        name="router",
    )(x, rp)


GATHER_ROWS = 256


def _gather_kernel(idx_ref, x_hbm, o_ref, sem):
    i = pl.program_id(0)

    def issue(r, _):
        tok = idx_ref[i * GATHER_ROWS + r]
        pltpu.make_async_copy(x_hbm.at[tok], o_ref.at[r], sem).start()
        return 0

    lax.fori_loop(0, GATHER_ROWS, issue, 0)
    pltpu.make_async_copy(x_hbm.at[pl.ds(0, GATHER_ROWS)], o_ref, sem).wait()


def _gather_rows(x3, idx):
    R = idx.shape[0]
    S = x3.shape[1]
    assert R % GATHER_ROWS == 0
    return pl.pallas_call(
        _gather_kernel,
        out_shape=jax.ShapeDtypeStruct((R, S, LANES), x3.dtype),
        grid_spec=pltpu.PrefetchScalarGridSpec(
            num_scalar_prefetch=1,
            grid=(R // GATHER_ROWS,),
            in_specs=[pl.BlockSpec(memory_space=pl.ANY)],
            out_specs=pl.BlockSpec((GATHER_ROWS, S, LANES), lambda i, idx: (i, 0, 0)),
            scratch_shapes=[pltpu.SemaphoreType.DMA(())],
        ),
        compiler_params=_cparams(("arbitrary",)),
        name="gather_rows",
    )(idx, x3)


def _combine_kernel(p0_ref, p1_ref, y_hbm, x_ref, g_ref, b_ref, o_ref, ob_ref, buf, sem):
    i = pl.program_id(0)

    def issue(r, _):
        t = i * GATHER_ROWS + r
        pltpu.make_async_copy(y_hbm.at[p0_ref[t]], buf.at[0, r], sem).start()
        pltpu.make_async_copy(y_hbm.at[p1_ref[t]], buf.at[1, r], sem).start()
        return 0

    lax.fori_loop(0, GATHER_ROWS, issue, 0)
    pltpu.make_async_copy(y_hbm.at[pl.ds(0, GATHER_ROWS)], buf.at[0], sem).wait()
    pltpu.make_async_copy(y_hbm.at[pl.ds(0, GATHER_ROWS)], buf.at[1], sem).wait()
    z = ALPHA * x_ref[...] + (buf[0] + buf[1])
    n = z.shape[1] * z.shape[2]
    mu = jnp.sum(z, axis=(1, 2), keepdims=True) / n
    zc = z - mu
    var = jnp.sum(zc * zc, axis=(1, 2), keepdims=True) / n
    y = zc * lax.rsqrt(var + LN_EPS) * g_ref[...] + b_ref[...]
    o_ref[...] = y
    ob_ref[...] = y.astype(BF16)


def _combine_ln(y3, p0, p1, x3, g, b):
    T, S, _ = x3.shape
    assert T % GATHER_ROWS == 0
    row = pl.BlockSpec((GATHER_ROWS, S, LANES), lambda i, p0, p1: (i, 0, 0))
    vec = pl.BlockSpec((1, S, LANES), lambda i, p0, p1: (0, 0, 0))
    return pl.pallas_call(
        _combine_kernel,
        out_shape=(jax.ShapeDtypeStruct((T, S, LANES), F32), jax.ShapeDtypeStruct((T, S, LANES), BF16)),
        grid_spec=pltpu.PrefetchScalarGridSpec(
            num_scalar_prefetch=2,
            grid=(T // GATHER_ROWS,),
            in_specs=[pl.BlockSpec(memory_space=pl.ANY), row, vec, vec],
            out_specs=(row, row),
            scratch_shapes=[pltpu.VMEM((2, GATHER_ROWS, S, LANES), F32), pltpu.SemaphoreType.DMA(())],
        ),
        compiler_params=_cparams(("arbitrary",)),
        name="combine_ln",
    )(p0, p1, y3, x3, g.reshape(1, S, LANES), b.reshape(1, S, LANES))


def _moe_plan(route, T):
    E = N_EXPERTS
    e = route[:, :TOP_K].astype(I32)
    gates = route[:, TOP_K:2 * TOP_K]
    ef = e.reshape(-1)
    onehot = (ef[:, None] == jnp.arange(E, dtype=I32)[None, :]).astype(I32)
    rank = jnp.cumsum(onehot, axis=0) - onehot
    counts = jnp.sum(onehot, axis=0)
    chunks = (counts + FFN_CHUNK - 1) // FFN_CHUNK
    seg_end = jnp.cumsum(chunks)
    seg_start = seg_end - chunks
    pos = seg_start[ef] * FFN_CHUNK + jnp.sum(rank * onehot, axis=1)
    R = -(-(TOP_K * T + E * FFN_CHUNK) // FFN_TM) * FFN_TM
    tok = jnp.zeros((R,), I32).at[pos].set(jnp.arange(TOP_K * T, dtype=I32) // TOP_K)
    gate_rows = jnp.zeros((R,), F32).at[pos].set(gates.reshape(-1))
    cpt = FFN_TM // FFN_CHUNK
    nt = R // FFN_TM
    used = seg_end[-1]
    used_tiles_end = (used + cpt - 1) // cpt * cpt
    pts = jnp.sort(jnp.concatenate([jnp.arange(nt, dtype=I32) * cpt, seg_start.astype(I32)]))
    nxt = jnp.concatenate([pts[1:], jnp.array([nt * cpt], I32)])
    ok = (pts < used) & (nxt > pts)
    nxt = jnp.where(nxt >= used, jnp.maximum(used_tiles_end, pts), nxt)
    tile = pts // cpt
    nxt = jnp.minimum(nxt, (tile + 1) * cpt)
    exp = jnp.minimum(jnp.searchsorted(seg_end, pts, side="right"), E - 1).astype(I32)
    last = jnp.max(jnp.where(ok, jnp.arange(pts.shape[0]), 0))
    vt = jnp.where(ok, tile, tile[last]).astype(I32)
    ve = jnp.where(ok, exp, exp[last]).astype(I32)
    vlo = jnp.where(ok, pts - tile * cpt, 0).astype(I32)
    vhi = jnp.where(ok, nxt - tile * cpt, 0).astype(I32)
    order = jnp.argsort(jnp.logical_not(ok), stable=True)
    vt, ve, vlo, vhi, vok = (a[order] for a in (vt, ve, vlo, vhi, ok.astype(I32)))
    n_slots = vt.shape[0]
    used_tiles = used_tiles_end // cpt
    fill = jnp.arange(n_slots, dtype=I32) - (n_slots - (nt - used_tiles))
    vt = jnp.where(fill >= 0, used_tiles + fill, vt).astype(I32)
    vok = jnp.where(fill >= 0, 2, vok).astype(I32)
    visits = (vt, ve, vlo, vhi, vok)
    p = pos.reshape(T, TOP_K).astype(I32)
    return tok, gate_rows, visits, p[:, 0], p[:, 1]


DSA_TK = 512


def _dsa_kernel(q_ref, iq_ref, iw_ref, ike_ref, iko_ref, kt_ref, v_ref, o_ref, key_s, bias_s, *, ksel, idx_bits):
    n = pl.program_id(0)
    TK = DSA_TK
    nkt = (n * BLOCK + BLOCK + TK - 1) // TK
    qpos = n * BLOCK + lax.broadcasted_iota(I32, (BLOCK, 1), 0)
    lane_k = lax.broadcasted_iota(I32, (1, TK), 1)

    iqb = iq_ref[...]
    npair = IDX_HEADS // 2
    lhs = jnp.concatenate([iqb[:, p * LANES:(p + 1) * LANES] for p in range(npair)], axis=0)
    wts = iw_ref[...] * ((IDX_DIM ** -0.5) * (IDX_HEADS ** -0.5))

    def score_tile(j, _):
        le = jnp.dot(lhs, ike_ref[j], preferred_element_type=F32)
        lo = jnp.dot(lhs, iko_ref[j], preferred_element_type=F32)
        sc = jnp.zeros((BLOCK, TK), F32)
        for p in range(npair):
            sc = sc + jnp.maximum(le[p * BLOCK:(p + 1) * BLOCK], 0.0) * wts[:, 2 * p:2 * p + 1]
            sc = sc + jnp.maximum(lo[p * BLOCK:(p + 1) * BLOCK], 0.0) * wts[:, 2 * p + 1:2 * p + 2]
        bits = pltpu.bitcast(sc, I32)
        key = bits ^ ((bits >> 31) & 0x7FFFFFFF)
        key_s[j] = jnp.where(j * TK + lane_k <= qpos, key, INT_MIN)
        return 0

    lax.fori_loop(0, nkt, score_tile, 0)

    def count(pred):
        def body(j, acc):
            hit = jnp.where(pred(key_s[j], j * TK + lane_k), 1.0, 0.0)
            for c in range(TK // LANES):
                acc = acc + hit[:, c * LANES:(c + 1) * LANES]
            return acc
        acc = lax.fori_loop(0, nkt, body, jnp.zeros((BLOCK, LANES), F32))
        return jnp.sum(acc, axis=1, keepdims=True)

    searching = (n + 1) * BLOCK > ksel

    def write_bias(thr, pidx):
        def body(j, _):
            key = key_s[j]
            sel = (key > thr) | ((key == thr) & (j * TK + lane_k <= pidx))
            bias_s[j] = jnp.where(sel, 0.0, NEG)
            return 0
        lax.fori_loop(0, nkt, body, 0)

    @pl.when(jnp.logical_not(searching))
    def _():
        write_bias(jnp.full((BLOCK, 1), INT_MIN, I32), jnp.full((BLOCK, 1), -1, I32))

    @pl.when(searching)
    def _():
        def bit_step(i, carry):
            thr, cnt_thr = carry
            cand = thr + jnp.left_shift(jnp.int32(1), 31 - i)
            cnt = count(lambda key, kidx: key >= cand)
            ok = cnt >= ksel
            return jnp.where(ok, cand, thr), jnp.where(ok, cnt, cnt_thr)

        thr0 = jnp.full((BLOCK, 1), INT_MIN, I32)
        thr, cnt_thr = lax.fori_loop(0, 32, bit_step, (thr0, jnp.full((BLOCK, 1), -1.0, F32)))
        exact = jnp.max(jnp.abs(cnt_thr - ksel)) == 0

        @pl.when(exact)
        def _():
            write_bias(thr, jnp.full((BLOCK, 1), 2 ** 30, I32))

        @pl.when(jnp.logical_not(exact))
        def _():
            need = ksel - count(lambda key, kidx: key > thr)

            def idx_step(i, pidx):
                cand = pidx + jnp.left_shift(jnp.int32(1), idx_bits - 1 - i)
                c = count(lambda key, kidx: (key == thr) & (kidx < cand))
                return jnp.where(c < need, cand, pidx)

            pidx = lax.fori_loop(0, idx_bits, idx_step, jnp.zeros((BLOCK, 1), I32))
            write_bias(thr, pidx)

    qb = q_ref[...]
    qs = jnp.concatenate([qb[:, h * LANES:(h + 1) * LANES] for h in range(DSA_HEADS)], axis=0)
    rows = DSA_HEADS * BLOCK
    scale = HEAD_DIM ** -0.5

    def attend(j, carry):
        m, l, acc = carry
        s = jnp.dot(qs, kt_ref[j], preferred_element_type=F32) * scale
        s = (s.reshape(DSA_HEADS, BLOCK, TK) + bias_s[j][None]).reshape(rows, TK)
        m_new = jnp.maximum(m, jnp.max(s, axis=-1, keepdims=True))
        alpha = jnp.exp(m - m_new)
        p = jnp.exp(s - m_new)
        l = alpha * l + jnp.sum(p, axis=-1, keepdims=True)
        acc = alpha * acc + jnp.dot(p.astype(BF16), v_ref[j], preferred_element_type=F32)
        return m_new, l, acc

    m0 = jnp.full((rows, 1), NEG, F32)
    l0 = jnp.zeros((rows, 1), F32)
    a0 = jnp.zeros((rows, HEAD_DIM), F32)
    _, l, acc = lax.fori_loop(0, nkt, attend, (m0, l0, a0))
    o = acc / l
    for h in range(DSA_HEADS):
        o_ref[:, h * LANES:(h + 1) * LANES] = o[h * BLOCK:(h + 1) * BLOCK].astype(o_ref.dtype)


def _dsa(q, iq, iw, ik, k, v):
    T = q.shape[0]
    TK = DSA_TK
    assert T % TK == 0 and TK % BLOCK == 0 and HEAD_DIM == LANES and 2 * IDX_DIM == LANES
    ksel = min(DSA_TOPK, T // 4)
    assert ksel % BLOCK == 0
    nt = T // TK
    z = jnp.zeros_like(ik)

    def tiles_t(a):
        return a.reshape(nt, TK, a.shape[1]).transpose(0, 2, 1)

    ike = tiles_t(jnp.concatenate([ik, z], axis=1))
    iko = tiles_t(jnp.concatenate([z, ik], axis=1))
    kt = tiles_t(k)
    v3 = v.reshape(nt, TK, HEAD_DIM)
    full3 = lambda a: pl.BlockSpec(a.shape, lambda n: (0, 0, 0))
    return pl.pallas_call(
        functools.partial(_dsa_kernel, ksel=ksel, idx_bits=max(1, (T - 1).bit_length())),
        out_shape=jax.ShapeDtypeStruct((T, DSA_Q), BF16),
        grid=(T // BLOCK,),
        in_specs=[
            pl.BlockSpec((BLOCK, DSA_Q), lambda n: (n, 0)),
            pl.BlockSpec((BLOCK, IDX_HEADS * IDX_DIM), lambda n: (n, 0)),
            pl.BlockSpec((BLOCK, LANES), lambda n: (n, 0)),
            full3(ike), full3(iko), full3(kt), full3(v3),
        ],
        out_specs=pl.BlockSpec((BLOCK, DSA_Q), lambda n: (n, 0)),
        scratch_shapes=[pltpu.VMEM((nt, BLOCK, TK), I32), pltpu.VMEM((nt, BLOCK, TK), F32)],
        compiler_params=_cparams(("arbitrary",)),
        name="dsa",
    )(q, iq, iw, ike, iko, kt, v3)


def _even_layer(x, xb, rope, w_in, pool_w, pool_scale, w_out, ln1_g, ln1_b, wg, wu, wd, ln2_g, ln2_b):
    T = x.shape[0]
    cos2, sin2 = rope
    u = _proj(xb, w_in, cos2, sin2, col0=0, ncols=POOL_DIM, tn=1024, rope_cols=0, out_dtype=F32)
    qkv = _proj(xb, w_in, cos2, sin2, col0=POOL_DIM, ncols=3 * DIL_QKV, tn=1024, rope_cols=2 * DIL_QKV,
                out_dtype=F32, head_major=True)
    pooled = _pool(u, pool_w, pool_scale)
    attn = _dilated(qkv)
    x1, x1b = _out_ln([pooled, attn], w_out.astype(BF16), x, ln1_g, ln1_b)
    y = _ffn(_dense_visits(T), x1b, wg[None], wu[None], wd[None], jnp.ones((T, 1), F32), use_gate=False)
    return _res_ln(x1, y, ln2_g, ln2_b)


def _odd_layer(x, xb, rope, w_in, w_out, ln1_g, ln1_b, router, wg, wu, wd, ln2_g, ln2_b):
    T, D = x.shape
    cos2, sin2 = rope
    q = _proj(xb, w_in, cos2, sin2, col0=0, ncols=DSA_Q, tn=1024, rope_cols=DSA_Q, out_dtype=BF16)
    n_rest = w_in.shape[1] - DSA_Q
    n_pad = -(-n_rest // LANES) * LANES
    w_rest = jnp.pad(w_in[:, DSA_Q:], ((0, 0), (0, n_pad - n_rest)))
    rest = _proj(xb, w_rest, cos2, sin2, col0=0, ncols=n_pad, tn=n_pad, rope_cols=HEAD_DIM, out_dtype=F32)
    o = 0
    k = rest[:, o:o + HEAD_DIM].astype(BF16); o += HEAD_DIM
    v = rest[:, o:o + HEAD_DIM].astype(BF16); o += HEAD_DIM
    iq = rest[:, o:o + IDX_HEADS * IDX_DIM].astype(BF16); o += IDX_HEADS * IDX_DIM
    ik = rest[:, o:o + IDX_DIM].astype(BF16); o += IDX_DIM
    iw = jnp.pad(rest[:, o:o + IDX_HEADS], ((0, 0), (0, LANES - IDX_HEADS)))
    attn = _dsa(q, iq, iw, ik, k, v)
    x1, x1b = _out_ln([attn], w_out.astype(BF16), x, ln1_g, ln1_b)
    route = _router(x1, router)
    tok, gate_rows, visits, p0, p1 = _moe_plan(route, T)
    S = D // LANES
    xs = _gather_rows(x1b.reshape(T, S, LANES), tok)
    R = xs.shape[0]
    y = _ffn(visits, xs.reshape(R, D), wg, wu, wd, gate_rows.reshape(R, 1), use_gate=True)
    x2, x2b = _combine_ln(y.reshape(R, S, LANES), p0, p1, x1.reshape(T, S, LANES), ln2_g, ln2_b)
    return x2.reshape(T, D), x2b.reshape(T, D)


def kernel(x, ev_w_in, ev_pool_w, ev_pool_scale, ev_w_out, ev_ln1_g, ev_ln1_b, ev_ffn_w_gate, ev_ffn_w_up,
           ev_ffn_w_down, ev_ln2_g, ev_ln2_b, od_w_in, od_w_out, od_ln1_g, od_ln1_b, od_router,
           od_exp_w_gate, od_exp_w_up, od_exp_w_down, od_ln2_g, od_ln2_b):
    B, T, D = x.shape
    rope = _rope_tables(T)
    outs = []
    for bi in range(B):
        h = x[bi]
        hb = h.astype(BF16)
        for layer in range(DEPTH):
            i = layer // 2
            if layer % 2 == 0:
                h, hb = _even_layer(h, hb, rope, ev_w_in[i], ev_pool_w[i], ev_pool_scale[i], ev_w_out[i],
                                    ev_ln1_g[i], ev_ln1_b[i], ev_ffn_w_gate[i], ev_ffn_w_up[i],
                                    ev_ffn_w_down[i], ev_ln2_g[i], ev_ln2_b[i])
            else:
                h, hb = _odd_layer(h, hb, rope, od_w_in[i], od_w_out[i], od_ln1_g[i], od_ln1_b[i],
                                   od_router[i], od_exp_w_gate[i], od_exp_w_up[i], od_exp_w_down[i],
                                   od_ln2_g[i], od_ln2_b[i])
        outs.append(h)
    return jnp.stack(outs, axis=0)
```

```python
import functools

import jax
import jax.numpy as jnp
from jax import lax
from jax.experimental import pallas as pl
from jax.experimental.pallas import tpu as pltpu

F32 = jnp.float32
BF16 = jnp.bfloat16
I32 = jnp.int32

D_MODEL = 2048
DEPTH = 4
HEAD_DIM = 128
ROPE_THETA = 10000.0
LN_EPS = 1e-5
BLOCK = 128
ALPHA = (2 * DEPTH) ** 0.25
POOL_WINDOWS = (2, 4, 8, 16)
POOL_GROUP_DIM = D_MODEL // 8
POOL_DIM = len(POOL_WINDOWS) * POOL_GROUP_DIM
DIL_PATTERNS = ((128, 1), (512, 4), (2048, 16))
DIL_HEADS_PER_GROUP = D_MODEL // 256
DIL_HEADS = len(DIL_PATTERNS) * DIL_HEADS_PER_GROUP
DIL_QKV = DIL_HEADS * HEAD_DIM
DIL_OUT = DIL_HEADS_PER_GROUP * HEAD_DIM
DSA_HEADS = D_MODEL // HEAD_DIM
IDX_HEADS = 16
IDX_DIM = 64
DSA_TOPK = 256
DSA_Q = DSA_HEADS * HEAD_DIM
D_FF = 5632
N_EXPERTS = 8
TOP_K = 2

LANES = 128
VMEM_LIMIT = 56 * 1024 * 1024

NEG = -1e30
LOG2_E = 1.4426950408889634
INT_MIN = -(2 ** 31)


def _cparams(sem):
    return pltpu.CompilerParams(dimension_semantics=sem, vmem_limit_bytes=VMEM_LIMIT)


def _layer_norm(z, g, b):
    mu = jnp.mean(z, axis=-1, keepdims=True)
    zc = z - mu
    var = jnp.mean(zc * zc, axis=-1, keepdims=True)
    return zc * lax.rsqrt(var + LN_EPS) * g + b


def _proj_kernel(x_ref, w_ref, cos_ref, sin_ref, o_ref, *, rope_cols, n_tiles, head_major):
    acc = jnp.dot(x_ref[...], w_ref[...].astype(BF16), preferred_element_type=F32)
    tn = acc.shape[1]
    nh = tn // LANES

    def write(rope_heads):
        for c in range(nh):
            a = acc[:, c * LANES:(c + 1) * LANES]
            if c < rope_heads:
                a = a * cos_ref[...] + pltpu.roll(a, HEAD_DIM // 2, 1) * sin_ref[...]
            if head_major:
                o_ref[c] = a.astype(o_ref.dtype)
            else:
                o_ref[:, c * LANES:(c + 1) * LANES] = a.astype(o_ref.dtype)

    if n_tiles == 1:
        write(rope_cols // LANES)
    else:
        rope_tiles = rope_cols // tn
        j = pl.program_id(0)
        pl.when(j < rope_tiles)(lambda: write(nh))
        pl.when(j >= rope_tiles)(lambda: write(0))


def _proj(xb, w, layer, cos2, sin2, *, col0, ncols, tn, rope_cols, out_dtype, head_major=False, tm=1024):
    T, K = xb.shape
    tm = min(tm, T)
    assert T % tm == 0 and ncols % tn == 0 and col0 % tn == 0 and tn % LANES == 0
    assert rope_cols % (LANES if ncols == tn else tn) == 0
    n_tiles = ncols // tn
    c0 = col0 // tn
    if head_major:
        out_shape = jax.ShapeDtypeStruct((ncols // LANES, T, LANES), out_dtype)
        out_spec = pl.BlockSpec((tn // LANES, tm, LANES), lambda j, i: (j, i, 0))
    else:
        out_shape = jax.ShapeDtypeStruct((T, ncols), out_dtype)
        out_spec = pl.BlockSpec((tm, tn), lambda j, i: (i, j))
    return pl.pallas_call(
        functools.partial(_proj_kernel, rope_cols=rope_cols, n_tiles=n_tiles, head_major=head_major),
        out_shape=out_shape,
        grid=(n_tiles, T // tm),
        in_specs=[
            pl.BlockSpec((tm, K), lambda j, i: (i, 0)),
            pl.BlockSpec((None, K, tn), lambda j, i: (layer, 0, j + c0)),
            pl.BlockSpec((tm, LANES), lambda j, i: (i, 0)),
            pl.BlockSpec((tm, LANES), lambda j, i: (i, 0)),
        ],
        out_specs=out_spec,
        compiler_params=_cparams(("arbitrary", "arbitrary")),
        name="proj",
    )(xb, w, cos2, sin2)


def _rope_tables(T):
    inv_freq = ROPE_THETA ** (-jnp.arange(0, HEAD_DIM, 2, dtype=F32) / HEAD_DIM)
    ang = jnp.arange(T, dtype=F32)[:, None] * inv_freq[None, :]
    cos, sin = jnp.cos(ang), jnp.sin(ang)
    return jnp.concatenate([cos, cos], axis=-1), jnp.concatenate([-sin, sin], axis=-1)


POOL_HALO = 16


def _pool_kernel(u_ref, h_ref, w_ref, sc_ref, o_ref):
    i = pl.program_id(0)
    tm = u_ref.shape[0]
    cur = u_ref[...]
    halo = jnp.where(i > 0, h_ref[...], 0.0)
    cat = jnp.concatenate([halo, cur], axis=0)
    t = i * tm + lax.broadcasted_iota(I32, (tm, 1), 0)
    G = POOL_GROUP_DIM
    for g, w in enumerate(POOL_WINDOWS):
        s = cat[:, g * G:(g + 1) * G]
        k = 1
        while k < w:
            s = s + pltpu.roll(s, k, 0)
            k *= 2
        win = s[POOL_HALO:]
        cnt = jnp.minimum(t + 1, w).astype(F32)
        p = win / cnt - cur[:, g * G:(g + 1) * G]
        y = jnp.dot(p.astype(BF16), w_ref[g].astype(BF16), preferred_element_type=F32)
        o_ref[:, g * G:(g + 1) * G] = (y * sc_ref[:, g * G:(g + 1) * G]).astype(o_ref.dtype)


def _pool(u, pool_w, layer, pool_scale, tm=512):
    T = u.shape[0]
    tm = min(tm, T)
    assert T % tm == 0 and tm % POOL_HALO == 0 and max(POOL_WINDOWS) <= POOL_HALO
    hb = tm // POOL_HALO
    return pl.pallas_call(
        _pool_kernel,
        out_shape=jax.ShapeDtypeStruct((T, POOL_DIM), BF16),
        grid=(T // tm,),
        in_specs=[
            pl.BlockSpec((tm, POOL_DIM), lambda i: (i, 0)),
            pl.BlockSpec((POOL_HALO, POOL_DIM), lambda i: (jnp.maximum(i * hb - 1, 0), 0)),
            pl.BlockSpec((None,) + pool_w.shape[1:], lambda i: (layer, 0, 0, 0)),
            pl.BlockSpec((1, POOL_DIM), lambda i: (0, 0)),
        ],
        out_specs=pl.BlockSpec((tm, POOL_DIM), lambda i: (i, 0)),
        compiler_params=_cparams(("arbitrary",)),
        name="pool",
    )(u, u, pool_w, pool_scale.reshape(1, POOL_DIM))


DIL_SPAN = BLOCK * max(d for _, d in DIL_PATTERNS)
DIL_UNROLL = 4


def _dil_kernel(*refs):
    ng = len(DIL_PATTERNS)
    in_refs = refs[:5 * ng]
    o_ref = refs[5 * ng]
    kf_ref, vf_ref, os_ref, ms_ref, ls_ref = refs[5 * ng + 1:]
    b = pl.program_id(1)
    scale = HEAD_DIM ** -0.5

    qi = lax.broadcasted_iota(I32, (BLOCK, 2 * BLOCK), 0)
    kj = lax.broadcasted_iota(I32, (BLOCK, 2 * BLOCK), 1)
    rel = BLOCK + qi - kj
    for g, (window, d) in enumerate(DIL_PATTERNS):
        q_ref, kc_ref, kp_ref, vc_ref, vp_ref = in_refs[5 * g:5 * g + 5]
        count = window // d
        band = (rel >= 0) & (rel <= count)
        bias_std = jnp.where(band, 0.0, NEG)
        bias_first = jnp.where(band & (kj >= BLOCK), 0.0, NEG)
        hist = BLOCK * d
        nsb = DIL_SPAN // hist
        kf_ref[0:hist, :] = kp_ref[...]
        kf_ref[hist:hist + DIL_SPAN, :] = kc_ref[...]
        vf_ref[0:hist, :] = vp_ref[...]
        vf_ref[hist:hist + DIL_SPAN, :] = vc_ref[...]

        def block(idx, g=g, d=d, hist=hist, nsb=nsb, q_ref=q_ref, bias_std=bias_std, bias_first=bias_first):
            r = idx // nsb
            sb = idx % nsb
            start = r + hist * sb
            q = q_ref[pl.ds(start, BLOCK, stride=d), :].astype(BF16)
            k = kf_ref[pl.ds(start, 2 * BLOCK, stride=d), :].astype(BF16)
            v = vf_ref[pl.ds(start, 2 * BLOCK, stride=d), :].astype(BF16)
            s = lax.dot_general(q, k, (((1,), (1,)), ((), ())), preferred_element_type=F32) * scale
            first = jnp.logical_and(b == 0, sb == 0)
            s = s + jnp.where(first, bias_first, bias_std)
            m = jnp.max(s, axis=-1, keepdims=True)
            p = jnp.exp(s - m)
            l = jnp.sum(p, axis=-1, keepdims=True)
            o = jnp.dot(p.astype(BF16), v, preferred_element_type=F32)
            os_ref[g, pl.ds(start, BLOCK, stride=d), :] = o
            ms_ref[g, pl.ds(start, BLOCK, stride=d), :] = jnp.broadcast_to(m, (BLOCK, LANES))
            ls_ref[g, pl.ds(start, BLOCK, stride=d), :] = jnp.broadcast_to(l, (BLOCK, LANES))

        def body(i, _, block=block):
            for u in range(DIL_UNROLL):
                block(i * DIL_UNROLL + u)
            return 0

        assert (d * nsb) % DIL_UNROLL == 0
        lax.fori_loop(0, d * nsb // DIL_UNROLL, body, 0)

    mx = ms_ref[0]
    for g in range(1, ng):
        mx = jnp.maximum(mx, ms_ref[g])
    num = jnp.zeros((DIL_SPAN, LANES), F32)
    den = jnp.zeros((DIL_SPAN, LANES), F32)
    for g in range(ng):
        e = jnp.exp(ms_ref[g] - mx)
        num = num + e * os_ref[g]
        den = den + e * ls_ref[g]
    o_ref[...] = (num / den).astype(o_ref.dtype)


def _dilated(qkv):
    T = qkv.shape[1]
    assert T % DIL_SPAN == 0 and HEAD_DIM == LANES
    H = DIL_HEADS_PER_GROUP
    in_specs, args = [], []
    max_hist = 0
    for g, (window, d) in enumerate(DIL_PATTERNS):
        assert window // d <= BLOCK and DIL_SPAN % (BLOCK * d) == 0
        hist = BLOCK * d
        max_hist = max(max_hist, hist)
        ratio = DIL_SPAN // hist

        def cur(j, b, base):
            return (base + j, b, 0)

        def prev(j, b, base, ratio=ratio):
            return (base + j, jnp.maximum(b * ratio - 1, 0), 0)

        qh, kh, vh = g * H, DIL_HEADS + g * H, 2 * DIL_HEADS + g * H
        in_specs += [
            pl.BlockSpec((None, DIL_SPAN, LANES), functools.partial(cur, base=qh)),
            pl.BlockSpec((None, DIL_SPAN, LANES), functools.partial(cur, base=kh)),
            pl.BlockSpec((None, hist, LANES), functools.partial(prev, base=kh)),
            pl.BlockSpec((None, DIL_SPAN, LANES), functools.partial(cur, base=vh)),
            pl.BlockSpec((None, hist, LANES), functools.partial(prev, base=vh)),
        ]
        args += [qkv] * 5
    ng = len(DIL_PATTERNS)
    return pl.pallas_call(
        _dil_kernel,
        out_shape=jax.ShapeDtypeStruct((T, DIL_OUT), BF16),
        grid=(H, T // DIL_SPAN),
        in_specs=in_specs,
        out_specs=pl.BlockSpec((DIL_SPAN, LANES), lambda j, b: (b, j)),
        scratch_shapes=[
            pltpu.VMEM((max_hist + DIL_SPAN, LANES), F32),
            pltpu.VMEM((max_hist + DIL_SPAN, LANES), F32),
            pltpu.VMEM((ng, DIL_SPAN, LANES), F32),
            pltpu.VMEM((ng, DIL_SPAN, LANES), F32),
            pltpu.VMEM((ng, DIL_SPAN, LANES), F32),
        ],
        compiler_params=_cparams(("arbitrary", "arbitrary")),
        name="dilated",
    )(*args)


def _out_ln_kernel(*refs, n_in):
    a_refs = refs[:n_in]
    w_refs = refs[n_in:2 * n_in]
    x_ref, g_ref, b_ref, o_ref, ob_ref = refs[2 * n_in:]
    acc = ALPHA * x_ref[...]
    for a_ref, w_ref in zip(a_refs, w_refs):
        acc = acc + jnp.dot(a_ref[...], w_ref[...], preferred_element_type=F32)
    y = _layer_norm(acc, g_ref[...], b_ref[...])
    o_ref[...] = y
    ob_ref[...] = y.astype(BF16)


def _out_ln(a_list, wb, x, g, b, tm=512):
    T, D = x.shape
    tm = min(tm, T)
    n_in = len(a_list)
    kk = a_list[0].shape[1]
    assert all(a.shape[1] == kk for a in a_list) and wb.shape[0] == n_in * kk and T % tm == 0
    w3 = wb.reshape(n_in, kk, D)
    in_specs = [pl.BlockSpec((tm, kk), lambda i: (i, 0)) for _ in range(n_in)]
    in_specs += [pl.BlockSpec((None, kk, D), functools.partial(lambda i, c: (c, 0, 0), c=c)) for c in range(n_in)]
    in_specs += [pl.BlockSpec((tm, D), lambda i: (i, 0)),
                 pl.BlockSpec((1, D), lambda i: (0, 0)),
                 pl.BlockSpec((1, D), lambda i: (0, 0))]
    return pl.pallas_call(
        functools.partial(_out_ln_kernel, n_in=n_in),
        out_shape=(jax.ShapeDtypeStruct((T, D), F32), jax.ShapeDtypeStruct((T, D), BF16)),
        grid=(T // tm,),
        in_specs=in_specs,
        out_specs=(pl.BlockSpec((tm, D), lambda i: (i, 0)), pl.BlockSpec((tm, D), lambda i: (i, 0))),
        compiler_params=_cparams(("arbitrary",)),
        name="out_ln",
    )(*a_list, *([w3] * n_in), x, g.reshape(1, D), b.reshape(1, D))


def _res_ln_kernel(x_ref, y_ref, g_ref, b_ref, o_ref, ob_ref):
    y = _layer_norm(ALPHA * x_ref[...] + y_ref[...], g_ref[...], b_ref[...])
    o_ref[...] = y
    ob_ref[...] = y.astype(BF16)


def _res_ln(x, y, g, b, tm=512):
    T, D = x.shape
    tm = min(tm, T)
    row = pl.BlockSpec((tm, D), lambda i: (i, 0))
    vec = pl.BlockSpec((1, D), lambda i: (0, 0))
    return pl.pallas_call(
        _res_ln_kernel,
        out_shape=(jax.ShapeDtypeStruct((T, D), F32), jax.ShapeDtypeStruct((T, D), BF16)),
        grid=(T // tm,),
        in_specs=[row, row, vec, vec],
        out_specs=(row, row),
        compiler_params=_cparams(("arbitrary",)),
        name="res_ln",
    )(x, y, g.reshape(1, D), b.reshape(1, D))


FFN_TM = 1024
FFN_TF = 256
FFN_CHUNK = 256


def _ffn_kernel(vt_ref, ve_ref, vlo_ref, vhi_ref, vok_ref, x_ref, wg_ref, wu_ref, wd_ref, o_ref):
    v = pl.program_id(0)
    f = pl.program_id(1)
    lo = vlo_ref[v]
    hi = vhi_ref[v]

    @pl.when(jnp.logical_and(vok_ref[v] == 2, f == 0))
    def _():
        o_ref[...] = jnp.zeros(o_ref.shape, o_ref.dtype)

    def rows(c):
        return pl.ds(pl.multiple_of(c * FFN_CHUNK, FFN_CHUNK), FFN_CHUNK)

    @pl.when(f == 0)
    def _():
        def zero(c, _):
            o_ref[rows(c), :] = jnp.zeros((FFN_CHUNK, o_ref.shape[1]), o_ref.dtype)
            return 0
        lax.fori_loop(lo, hi, zero, 0)

    def chunk(c):
        x = x_ref[rows(c), :]
        g = jnp.dot(x, wg_ref[...].astype(BF16), preferred_element_type=F32)
        u = jnp.dot(x, wu_ref[...].astype(BF16), preferred_element_type=F32)
        h = (g * jax.nn.sigmoid(g) * u).astype(BF16)
        o_ref[rows(c), :] += jnp.dot(h, wd_ref[...].astype(BF16), preferred_element_type=F32)

    def pair(i, _):
        chunk(lo + 2 * i)
        chunk(lo + 2 * i + 1)
        return 0

    n = hi - lo
    lax.fori_loop(0, n // 2, pair, 0)

    @pl.when(n % 2 == 1)
    def _():
        chunk(hi - 1)


def _ffn(visits, xs, wg, wu, wd):
    vt, ve, vlo, vhi, vok = visits
    R, D = xs.shape
    F = wg.shape[2]
    assert R % FFN_TM == 0 and F % FFN_TF == 0 and FFN_TM % FFN_CHUNK == 0
    nf = F // FFN_TF
    V = vt.shape[0]

    def fidx(v, f, vok):
        return jnp.where(vok[v] == 1, f, nf - 1)

    return pl.pallas_call(
        _ffn_kernel,
        out_shape=jax.ShapeDtypeStruct((R, D), F32),
        grid_spec=pltpu.PrefetchScalarGridSpec(
            num_scalar_prefetch=5,
            grid=(V, nf),
            in_specs=[
                pl.BlockSpec((FFN_TM, D), lambda v, f, vt, ve, vlo, vhi, vok: (vt[v], 0)),
                pl.BlockSpec((None, D, FFN_TF), lambda v, f, vt, ve, vlo, vhi, vok: (ve[v], 0, fidx(v, f, vok))),
                pl.BlockSpec((None, D, FFN_TF), lambda v, f, vt, ve, vlo, vhi, vok: (ve[v], 0, fidx(v, f, vok))),
                pl.BlockSpec((None, FFN_TF, D), lambda v, f, vt, ve, vlo, vhi, vok: (ve[v], fidx(v, f, vok), 0)),
            ],
            out_specs=pl.BlockSpec((FFN_TM, D), lambda v, f, vt, ve, vlo, vhi, vok: (vt[v], 0)),
        ),
        compiler_params=_cparams(("arbitrary", "arbitrary")),
        name="ffn",
    )(vt, ve, vlo, vhi, vok, xs, wg, wu, wd)


def _dense_visits(T, layer):
    nt = T // FFN_TM
    cpt = FFN_TM // FFN_CHUNK
    z = jnp.zeros((nt,), I32)
    return (jnp.arange(nt, dtype=I32), z + layer, z, z + cpt, z + 1)


def _router_kernel(x_ref, r_ref, o_ref):
    logits = jnp.dot(x_ref[...], r_ref[...], preferred_element_type=F32, precision=lax.Precision.HIGHEST)
    lane = lax.broadcasted_iota(I32, logits.shape, 1)
    logits = jnp.where(lane < N_EXPERTS, logits, -jnp.inf)
    v1 = jnp.max(logits, axis=-1, keepdims=True)
    i1 = jnp.min(jnp.where(logits == v1, lane, LANES), axis=-1, keepdims=True)
    rest = jnp.where(lane == i1, -jnp.inf, logits)
    v2 = jnp.max(rest, axis=-1, keepdims=True)
    i2 = jnp.min(jnp.where(rest == v2, lane, LANES), axis=-1, keepdims=True)
    e2 = jnp.exp(v2 - v1)
    g1 = 1.0 / (1.0 + e2)
    g2 = e2 / (1.0 + e2)
    out = jnp.where(lane == 0, i1.astype(F32), 0.0)
    out = jnp.where(lane == 1, i2.astype(F32), out)
    out = jnp.where(lane == 2, g1, out)
    out = jnp.where(lane == 3, g2, out)
    o_ref[...] = out


def _router(x, router, tm=512):
    T, D = x.shape
    tm = min(tm, T)
    rp = jnp.pad(router, ((0, 0), (0, LANES - N_EXPERTS)))
    return pl.pallas_call(
        _router_kernel,
        out_shape=jax.ShapeDtypeStruct((T, LANES), F32),
        grid=(T // tm,),
        in_specs=[pl.BlockSpec((tm, D), lambda i: (i, 0)), pl.BlockSpec((D, LANES), lambda i: (0, 0))],
        out_specs=pl.BlockSpec((tm, LANES), lambda i: (i, 0)),
        compiler_params=_cparams(("arbitrary",)),
        name="router",
    )(x, rp)


GATHER_ROWS = 256


def _dispatch_kernel(p0_ref, p1_ref, x_ref, init_hbm, o_hbm, sem):
    del init_hbm
    i = pl.program_id(0)

    def issue(r, _):
        t = i * GATHER_ROWS + r
        pltpu.make_async_copy(x_ref.at[r], o_hbm.at[p0_ref[t]], sem).start()
        pltpu.make_async_copy(x_ref.at[r], o_hbm.at[p1_ref[t]], sem).start()
        return 0

    lax.fori_loop(0, GATHER_ROWS, issue, 0)
    for _ in range(TOP_K):
        pltpu.make_async_copy(x_ref, o_hbm.at[pl.ds(0, GATHER_ROWS)], sem).wait()


def _dispatch(x3, p0, p1, R):
    T, S, _ = x3.shape
    assert T % GATHER_ROWS == 0 and R >= GATHER_ROWS
    init = jnp.zeros((R, S, LANES), x3.dtype)
    return pl.pallas_call(
        _dispatch_kernel,
        out_shape=jax.ShapeDtypeStruct((R, S, LANES), x3.dtype),
        grid_spec=pltpu.PrefetchScalarGridSpec(
            num_scalar_prefetch=2,
            grid=(T // GATHER_ROWS,),
            in_specs=[pl.BlockSpec((GATHER_ROWS, S, LANES), lambda i, p0, p1: (i, 0, 0)),
                      pl.BlockSpec(memory_space=pl.ANY)],
            out_specs=pl.BlockSpec(memory_space=pl.ANY),
            scratch_shapes=[pltpu.SemaphoreType.DMA(())],
        ),
        input_output_aliases={3: 0},
        compiler_params=_cparams(("arbitrary",)),
        name="dispatch",
    )(p0, p1, x3, init)


def _combine_kernel(p0_ref, p1_ref, y_hbm, r_ref, x_ref, g_ref, b_ref, o_ref, ob_ref, buf, sem):
    i = pl.program_id(0)

    def issue(r, _):
        t = i * GATHER_ROWS + r
        pltpu.make_async_copy(y_hbm.at[p0_ref[t]], buf.at[0, r], sem).start()
        pltpu.make_async_copy(y_hbm.at[p1_ref[t]], buf.at[1, r], sem).start()
        return 0

    lax.fori_loop(0, GATHER_ROWS, issue, 0)
    pltpu.make_async_copy(y_hbm.at[pl.ds(0, GATHER_ROWS)], buf.at[0], sem).wait()
    pltpu.make_async_copy(y_hbm.at[pl.ds(0, GATHER_ROWS)], buf.at[1], sem).wait()
    route = r_ref[...]
    g0 = route[:, :, TOP_K:TOP_K + 1]
    g1 = route[:, :, TOP_K + 1:TOP_K + 2]
    z = ALPHA * x_ref[...] + (g0 * buf[0] + g1 * buf[1])
    n = z.shape[1] * z.shape[2]
    mu = jnp.sum(z, axis=(1, 2), keepdims=True) / n
    zc = z - mu
    var = jnp.sum(zc * zc, axis=(1, 2), keepdims=True) / n
    y = zc * lax.rsqrt(var + LN_EPS) * g_ref[...] + b_ref[...]
    o_ref[...] = y
    ob_ref[...] = y.astype(BF16)


def _combine_ln(y3, p0, p1, route, x3, g, b):
    T, S, _ = x3.shape
    assert T % GATHER_ROWS == 0
    row = pl.BlockSpec((GATHER_ROWS, S, LANES), lambda i, p0, p1: (i, 0, 0))
    vec = pl.BlockSpec((1, S, LANES), lambda i, p0, p1: (0, 0, 0))
    rspec = pl.BlockSpec((GATHER_ROWS, 1, LANES), lambda i, p0, p1: (i, 0, 0))
    return pl.pallas_call(
        _combine_kernel,
        out_shape=(jax.ShapeDtypeStruct((T, S, LANES), F32), jax.ShapeDtypeStruct((T, S, LANES), BF16)),
        grid_spec=pltpu.PrefetchScalarGridSpec(
            num_scalar_prefetch=2,
            grid=(T // GATHER_ROWS,),
            in_specs=[pl.BlockSpec(memory_space=pl.ANY), rspec, row, vec, vec],
            out_specs=(row, row),
            scratch_shapes=[pltpu.VMEM((2, GATHER_ROWS, S, LANES), F32), pltpu.SemaphoreType.DMA(())],
        ),
        compiler_params=_cparams(("arbitrary",)),
        name="combine_ln",
    )(p0, p1, y3, route.reshape(T, 1, LANES), x3, g.reshape(1, S, LANES), b.reshape(1, S, LANES))


def _moe_plan(route, T, layer):
    E = N_EXPERTS
    e = route[:, :TOP_K].astype(I32)
    ef = e.reshape(-1)
    onehot = (ef[:, None] == jnp.arange(E, dtype=I32)[None, :]).astype(I32)
    rank = jnp.cumsum(onehot, axis=0) - onehot
    counts = jnp.sum(onehot, axis=0)
    chunks = (counts + FFN_CHUNK - 1) // FFN_CHUNK
    seg_end = jnp.cumsum(chunks)
    seg_start = seg_end - chunks
    pos = seg_start[ef] * FFN_CHUNK + jnp.sum(rank * onehot, axis=1)
    R = -(-(TOP_K * T + E * FFN_CHUNK) // FFN_TM) * FFN_TM
    cpt = FFN_TM // FFN_CHUNK
    nt = R // FFN_TM
    n_slots = nt + E
    slot = jnp.arange(n_slots, dtype=I32)

    def place(vals, dest):
        return jnp.sum(jnp.where(dest[None, :] == slot[:, None], vals[None, :], 0), axis=1).astype(I32)

    tile_pts = jnp.arange(nt, dtype=I32) * cpt
    seg_pts = seg_start.astype(I32)
    tile_rank = jnp.arange(nt, dtype=I32) + jnp.sum(seg_pts[None, :] < tile_pts[:, None], axis=1)
    seg_rank = jnp.arange(E, dtype=I32) + jnp.sum(tile_pts[None, :] <= seg_pts[:, None], axis=1)
    pts = place(jnp.concatenate([tile_pts, seg_pts]), jnp.concatenate([tile_rank, seg_rank]))
    used = seg_end[-1]
    used_tiles_end = (used + cpt - 1) // cpt * cpt
    nxt = jnp.concatenate([pts[1:], jnp.array([nt * cpt], I32)])
    ok = (pts < used) & (nxt > pts)
    nxt = jnp.where(nxt >= used, jnp.maximum(used_tiles_end, pts), nxt)
    tile = pts // cpt
    nxt = jnp.minimum(nxt, (tile + 1) * cpt)
    exp = jnp.minimum(jnp.sum(seg_end[None, :] <= pts[:, None], axis=1), E - 1).astype(I32)
    last = jnp.max(jnp.where(ok, slot, 0))
    vt = jnp.where(ok, tile, jnp.sum(jnp.where(slot == last, tile, 0)))
    ve = jnp.where(ok, exp, jnp.sum(jnp.where(slot == last, exp, 0)))
    vlo = jnp.where(ok, pts - tile * cpt, 0)
    vhi = jnp.where(ok, nxt - tile * cpt, 0)
    oki = ok.astype(I32)
    dest = jnp.where(ok, jnp.cumsum(oki) - oki, jnp.sum(oki) + jnp.cumsum(1 - oki) - (1 - oki))
    vt, ve, vlo, vhi, vok = (place(a, dest) for a in (vt, ve, vlo, vhi, oki))
    used_tiles = used_tiles_end // cpt
    fill = slot - (n_slots - (nt - used_tiles))
    vt = jnp.where(fill >= 0, used_tiles + fill, vt).astype(I32)
    vok = jnp.where(fill >= 0, 2, vok).astype(I32)
    visits = (vt, (ve + layer * E).astype(I32), vlo, vhi, vok)
    p = pos.reshape(T, TOP_K).astype(I32)
    return R, visits, p[:, 0], p[:, 1]


DSA_TK = 512


def _dsa_kernel(q_ref, iq_ref, iw_ref, ike_ref, iko_ref, kt_ref, v_ref, o_ref,
                key_s, bias_s, s_s, p_s, m_s, l_s, al_s, acc_s, *, ksel, idx_bits):
    n = pl.program_id(0)
    TK = DSA_TK
    nkt = (n * BLOCK + BLOCK + TK - 1) // TK
    qpos = n * BLOCK + lax.broadcasted_iota(I32, (BLOCK, 1), 0)
    lane_k = lax.broadcasted_iota(I32, (1, TK), 1)

    iqb = iq_ref[...]
    npair = IDX_HEADS // 2
    lhs = jnp.concatenate([iqb[:, p * LANES:(p + 1) * LANES] for p in range(npair)], axis=0)
    wts = iw_ref[...] * ((IDX_DIM ** -0.5) * (IDX_HEADS ** -0.5))

    def score_tile(j, _):
        le = jnp.dot(lhs, ike_ref[j], preferred_element_type=F32)
        lo = jnp.dot(lhs, iko_ref[j], preferred_element_type=F32)
        sc = jnp.zeros((BLOCK, TK), F32)
        for p in range(npair):
            sc = sc + jnp.maximum(le[p * BLOCK:(p + 1) * BLOCK], 0.0) * wts[:, 2 * p:2 * p + 1]
            sc = sc + jnp.maximum(lo[p * BLOCK:(p + 1) * BLOCK], 0.0) * wts[:, 2 * p + 1:2 * p + 2]
        bits = pltpu.bitcast(sc, I32)
        key = bits ^ ((bits >> 31) & 0x7FFFFFFF)
        key_s[j] = jnp.where(j * TK + lane_k <= qpos, key, INT_MIN)
        return 0

    lax.fori_loop(0, nkt, score_tile, 0)

    def count(pred):
        def body(j, acc):
            hit = jnp.where(pred(key_s[j], j * TK + lane_k), 1.0, 0.0)
            for c in range(TK // LANES):
                acc = acc + hit[:, c * LANES:(c + 1) * LANES]
            return acc
        acc = lax.fori_loop(0, nkt, body, jnp.zeros((BLOCK, LANES), F32))
        return jnp.sum(acc, axis=1, keepdims=True)

    searching = (n + 1) * BLOCK > ksel

    def write_bias(thr, pidx):
        def body(j, _):
            key = key_s[j]
            sel = (key > thr) | ((key == thr) & (j * TK + lane_k <= pidx))
            bias_s[j] = jnp.where(sel, 0.0, NEG)
            return 0
        lax.fori_loop(0, nkt, body, 0)

    @pl.when(jnp.logical_not(searching))
    def _():
        write_bias(jnp.full((BLOCK, 1), INT_MIN, I32), jnp.full((BLOCK, 1), -1, I32))

    @pl.when(searching)
    def _():
        def bit_step(carry):
            i, thr, cnt_thr, _ = carry
            cand = thr + jnp.left_shift(jnp.int32(1), 31 - i)
            cnt = count(lambda key, kidx: key >= cand)
            ok = cnt >= ksel
            thr = jnp.where(ok, cand, thr)
            cnt_thr = jnp.where(ok, cnt, cnt_thr)
            return i + 1, thr, cnt_thr, jnp.max(jnp.abs(cnt_thr - ksel))

        thr0 = jnp.full((BLOCK, 1), INT_MIN, I32)
        _, thr, _, miss = lax.while_loop(
            lambda c: jnp.logical_and(c[0] < 32, c[3] > 0.0), bit_step,
            (jnp.int32(0), thr0, jnp.full((BLOCK, 1), -1.0, F32), jnp.float32(1.0)))
        exact = miss == 0.0

        @pl.when(exact)
        def _():
            write_bias(thr, jnp.full((BLOCK, 1), 2 ** 30, I32))

        @pl.when(jnp.logical_not(exact))
        def _():
            need = ksel - count(lambda key, kidx: key > thr)

            def idx_step(i, pidx):
                cand = pidx + jnp.left_shift(jnp.int32(1), idx_bits - 1 - i)
                c = count(lambda key, kidx: (key == thr) & (kidx < cand))
                return jnp.where(c < need, cand, pidx)

            pidx = lax.fori_loop(0, idx_bits, idx_step, jnp.zeros((BLOCK, 1), I32))
            write_bias(thr, pidx)

    qb = q_ref[...]
    qs = jnp.concatenate([qb[:, h * LANES:(h + 1) * LANES] for h in range(DSA_HEADS)], axis=0)
    c2 = (HEAD_DIM ** -0.5) * LOG2_E
    ncol = TK // LANES
    m_s[...] = jnp.full(m_s.shape, NEG, F32)
    l_s[...] = jnp.zeros(l_s.shape, F32)
    acc_s[...] = jnp.zeros(acc_s.shape, F32)

    def attend(j, _):
        s_s[...] = jnp.dot(qs, kt_ref[j], preferred_element_type=F32)
        bias = bias_s[j]
        for h in range(DSA_HEADS):
            r = slice(h * BLOCK, (h + 1) * BLOCK)
            t = [s_s[r, c * LANES:(c + 1) * LANES] * c2 + bias[:, c * LANES:(c + 1) * LANES] for c in range(ncol)]
            mx = t[0]
            for c in range(1, ncol):
                mx = jnp.maximum(mx, t[c])
            m_old = m_s[r, :]
            m_new = jnp.maximum(m_old, jnp.max(mx, axis=-1, keepdims=True))
            alpha = jnp.exp2(m_old - m_new)
            psum = jnp.zeros((BLOCK, LANES), F32)
            for c in range(ncol):
                p = jnp.exp2(t[c] - m_new)
                psum = psum + p
                p_s[r, c * LANES:(c + 1) * LANES] = p.astype(BF16)
            m_s[r, :] = m_new
            l_s[r, :] = alpha * l_s[r, :] + jnp.sum(psum, axis=-1, keepdims=True)
            al_s[r, :] = alpha
        acc_s[...] = al_s[...] * acc_s[...] + jnp.dot(p_s[...], v_ref[j], preferred_element_type=F32)
        return 0

    lax.fori_loop(0, nkt, attend, 0)
    for h in range(DSA_HEADS):
        r = slice(h * BLOCK, (h + 1) * BLOCK)
        o_ref[:, h * LANES:(h + 1) * LANES] = (acc_s[r, :] / l_s[r, :]).astype(o_ref.dtype)


def _dsa(q, iq, iw, ik, k, v):
    T = q.shape[0]
    TK = DSA_TK
    assert T % TK == 0 and TK % BLOCK == 0 and HEAD_DIM == LANES and 2 * IDX_DIM == LANES
    ksel = min(DSA_TOPK, T // 4)
    assert ksel % BLOCK == 0
    nt = T // TK
    rows = DSA_HEADS * BLOCK
    z = jnp.zeros_like(ik)

    def tiles_t(a):
        return a.reshape(nt, TK, a.shape[1]).transpose(0, 2, 1)

    ike = tiles_t(jnp.concatenate([ik, z], axis=1))
    iko = tiles_t(jnp.concatenate([z, ik], axis=1))
    kt = tiles_t(k)
    v3 = v.reshape(nt, TK, HEAD_DIM)
    full3 = lambda a: pl.BlockSpec(a.shape, lambda n: (0, 0, 0))
    return pl.pallas_call(
        functools.partial(_dsa_kernel, ksel=ksel, idx_bits=max(1, (T - 1).bit_length())),
        out_shape=jax.ShapeDtypeStruct((T, DSA_Q), BF16),
        grid=(T // BLOCK,),
        in_specs=[
            pl.BlockSpec((BLOCK, DSA_Q), lambda n: (n, 0)),
            pl.BlockSpec((BLOCK, IDX_HEADS * IDX_DIM), lambda n: (n, 0)),
            pl.BlockSpec((BLOCK, LANES), lambda n: (n, 0)),
            full3(ike), full3(iko), full3(kt), full3(v3),
        ],
        out_specs=pl.BlockSpec((BLOCK, DSA_Q), lambda n: (n, 0)),
        scratch_shapes=[
            pltpu.VMEM((nt, BLOCK, TK), I32),
            pltpu.VMEM((nt, BLOCK, TK), F32),
            pltpu.VMEM((rows, TK), F32),
            pltpu.VMEM((rows, TK), BF16),
            pltpu.VMEM((rows, LANES), F32),
            pltpu.VMEM((rows, LANES), F32),
            pltpu.VMEM((rows, LANES), F32),
            pltpu.VMEM((rows, HEAD_DIM), F32),
        ],
        compiler_params=_cparams(("arbitrary",)),
        name="dsa",
    )(q, iq, iw, ike, iko, kt, v3)


def _even_layer(x, xb, rope, i, w_in, pool_w, pool_scale, w_out, ln1_g, ln1_b, wg, wu, wd, ln2_g, ln2_b):
    T = x.shape[0]
    cos2, sin2 = rope
    u = _proj(xb, w_in, i, cos2, sin2, col0=0, ncols=POOL_DIM, tn=1024, rope_cols=0, out_dtype=F32)
    qkv = _proj(xb, w_in, i, cos2, sin2, col0=POOL_DIM, ncols=3 * DIL_QKV, tn=1024, rope_cols=2 * DIL_QKV,
                out_dtype=F32, head_major=True)
    pooled = _pool(u, pool_w, i, pool_scale[i])
    attn = _dilated(qkv)
    x1, x1b = _out_ln([pooled, attn], w_out[i].astype(BF16), x, ln1_g[i], ln1_b[i])
    y = _ffn(_dense_visits(T, i), x1b, wg, wu, wd)
    return _res_ln(x1, y, ln2_g[i], ln2_b[i])


def _odd_layer(x, xb, rope, i, w_in, w_out, ln1_g, ln1_b, router, wg, wu, wd, ln2_g, ln2_b):
    T, D = x.shape
    cos2, sin2 = rope
    q = _proj(xb, w_in, i, cos2, sin2, col0=0, ncols=DSA_Q, tn=1024, rope_cols=DSA_Q, out_dtype=BF16)
    n_rest = w_in.shape[2] - DSA_Q
    n_pad = -(-n_rest // LANES) * LANES
    w_rest = jnp.pad(w_in[i, :, DSA_Q:], ((0, 0), (0, n_pad - n_rest)))[None]
    rest = _proj(xb, w_rest, 0, cos2, sin2, col0=0, ncols=n_pad, tn=n_pad, rope_cols=HEAD_DIM, out_dtype=F32)
    o = 0
    k = rest[:, o:o + HEAD_DIM].astype(BF16); o += HEAD_DIM
    v = rest[:, o:o + HEAD_DIM].astype(BF16); o += HEAD_DIM
    iq = rest[:, o:o + IDX_HEADS * IDX_DIM].astype(BF16); o += IDX_HEADS * IDX_DIM
    ik = rest[:, o:o + IDX_DIM].astype(BF16); o += IDX_DIM
    iw = jnp.pad(rest[:, o:o + IDX_HEADS], ((0, 0), (0, LANES - IDX_HEADS)))
    attn = _dsa(q, iq, iw, ik, k, v)
    x1, x1b = _out_ln([attn], w_out[i].astype(BF16), x, ln1_g[i], ln1_b[i])
    route = _router(x1, router[i])
    R, visits, p0, p1 = _moe_plan(route, T, i)
    S = D // LANES
    xs = _dispatch(x1b.reshape(T, S, LANES), p0, p1, R)
    E, F = wg.shape[1], wg.shape[3]
    y = _ffn(visits, xs.reshape(R, D), wg.reshape(-1, D, F), wu.reshape(-1, D, F), wd.reshape(-1, F, D))
    x2, x2b = _combine_ln(y.reshape(R, S, LANES), p0, p1, route, x1.reshape(T, S, LANES), ln2_g[i], ln2_b[i])
    return x2.reshape(T, D), x2b.reshape(T, D)


def kernel(x, ev_w_in, ev_pool_w, ev_pool_scale, ev_w_out, ev_ln1_g, ev_ln1_b, ev_ffn_w_gate, ev_ffn_w_up,
           ev_ffn_w_down, ev_ln2_g, ev_ln2_b, od_w_in, od_w_out, od_ln1_g, od_ln1_b, od_router,
           od_exp_w_gate, od_exp_w_up, od_exp_w_down, od_ln2_g, od_ln2_b):
    B, T, D = x.shape
    rope = _rope_tables(T)
    outs = []
    for bi in range(B):
        h = x[bi]
        hb = h.astype(BF16)
        for layer in range(DEPTH):
            i = layer // 2
            if layer % 2 == 0:
                h, hb = _even_layer(h, hb, rope, i, ev_w_in, ev_pool_w, ev_pool_scale, ev_w_out,
                                    ev_ln1_g, ev_ln1_b, ev_ffn_w_gate, ev_ffn_w_up,
                                    ev_ffn_w_down, ev_ln2_g, ev_ln2_b)
            else:
                h, hb = _odd_layer(h, hb, rope, i, od_w_in, od_w_out, od_ln1_g, od_ln1_b,
                                   od_router, od_exp_w_gate, od_exp_w_up, od_exp_w_down,
                                   od_ln2_g, od_ln2_b)
        outs.append(h)
    return jnp.stack(outs, axis=0)
```

```python
import functools

import jax
import jax.numpy as jnp
from jax import lax
from jax.experimental import pallas as pl
from jax.experimental.pallas import tpu as pltpu

F32 = jnp.float32
BF16 = jnp.bfloat16
I32 = jnp.int32

D_MODEL = 2048
DEPTH = 4
HEAD_DIM = 128
ROPE_THETA = 10000.0
LN_EPS = 1e-5
BLOCK = 128
ALPHA = (2 * DEPTH) ** 0.25
POOL_WINDOWS = (2, 4, 8, 16)
POOL_GROUP_DIM = D_MODEL // 8
POOL_DIM = len(POOL_WINDOWS) * POOL_GROUP_DIM
DIL_PATTERNS = ((128, 1), (512, 4), (2048, 16))
DIL_HEADS_PER_GROUP = D_MODEL // 256
DIL_HEADS = len(DIL_PATTERNS) * DIL_HEADS_PER_GROUP
DIL_QKV = DIL_HEADS * HEAD_DIM
DIL_OUT = DIL_HEADS_PER_GROUP * HEAD_DIM
DSA_HEADS = D_MODEL // HEAD_DIM
IDX_HEADS = 16
IDX_DIM = 64
DSA_TOPK = 256
DSA_Q = DSA_HEADS * HEAD_DIM
D_FF = 5632
N_EXPERTS = 8
TOP_K = 2

LANES = 128
VMEM_LIMIT = 56 * 1024 * 1024

NEG = -1e30
LOG2_E = 1.4426950408889634
INT_MIN = -(2 ** 31)


def _cparams(sem):
    return pltpu.CompilerParams(dimension_semantics=sem, vmem_limit_bytes=VMEM_LIMIT)


def _layer_norm(z, g, b):
    mu = jnp.mean(z, axis=-1, keepdims=True)
    zc = z - mu
    var = jnp.mean(zc * zc, axis=-1, keepdims=True)
    return zc * lax.rsqrt(var + LN_EPS) * g + b


def _proj_kernel(x_ref, w_ref, cos_ref, sin_ref, o_ref, *, rope_cols, n_tiles, head_major):
    acc = jnp.dot(x_ref[...], w_ref[...].astype(BF16), preferred_element_type=F32)
    tn = acc.shape[1]
    nh = tn // LANES

    def write(rope_heads):
        for c in range(nh):
            a = acc[:, c * LANES:(c + 1) * LANES]
            if c < rope_heads:
                a = a * cos_ref[...] + pltpu.roll(a, HEAD_DIM // 2, 1) * sin_ref[...]
            if head_major:
                o_ref[c] = a.astype(o_ref.dtype)
            else:
                o_ref[:, c * LANES:(c + 1) * LANES] = a.astype(o_ref.dtype)

    if n_tiles == 1:
        write(rope_cols // LANES)
    else:
        rope_tiles = rope_cols // tn
        j = pl.program_id(0)
        pl.when(j < rope_tiles)(lambda: write(nh))
        pl.when(j >= rope_tiles)(lambda: write(0))


def _proj(xb, w, layer, cos2, sin2, *, col0, ncols, tn, rope_cols, out_dtype, head_major=False, tm=1024):
    T, K = xb.shape
    tm = min(tm, T)
    assert T % tm == 0 and ncols % tn == 0 and col0 % tn == 0 and tn % LANES == 0
    assert rope_cols % (LANES if ncols == tn else tn) == 0
    n_tiles = ncols // tn
    c0 = col0 // tn
    if head_major:
        out_shape = jax.ShapeDtypeStruct((ncols // LANES, T, LANES), out_dtype)
        out_spec = pl.BlockSpec((tn // LANES, tm, LANES), lambda j, i: (j, i, 0))
    else:
        out_shape = jax.ShapeDtypeStruct((T, ncols), out_dtype)
        out_spec = pl.BlockSpec((tm, tn), lambda j, i: (i, j))
    return pl.pallas_call(
        functools.partial(_proj_kernel, rope_cols=rope_cols, n_tiles=n_tiles, head_major=head_major),
        out_shape=out_shape,
        grid=(n_tiles, T // tm),
        in_specs=[
            pl.BlockSpec((tm, K), lambda j, i: (i, 0)),
            pl.BlockSpec((None, K, tn), lambda j, i: (layer, 0, j + c0)),
            pl.BlockSpec((tm, LANES), lambda j, i: (i, 0)),
            pl.BlockSpec((tm, LANES), lambda j, i: (i, 0)),
        ],
        out_specs=out_spec,
        compiler_params=_cparams(("arbitrary", "arbitrary")),
        name="proj",
    )(xb, w, cos2, sin2)


def _rope_tables(T):
    inv_freq = ROPE_THETA ** (-jnp.arange(0, HEAD_DIM, 2, dtype=F32) / HEAD_DIM)
    ang = jnp.arange(T, dtype=F32)[:, None] * inv_freq[None, :]
    cos, sin = jnp.cos(ang), jnp.sin(ang)
    return jnp.concatenate([cos, cos], axis=-1), jnp.concatenate([-sin, sin], axis=-1)


POOL_HALO = 16


def _pool_kernel(u_ref, h_ref, w_ref, sc_ref, o_ref):
    i = pl.program_id(0)
    tm = u_ref.shape[0]
    cur = u_ref[...]
    halo = jnp.where(i > 0, h_ref[...], 0.0)
    cat = jnp.concatenate([halo, cur], axis=0)
    t = i * tm + lax.broadcasted_iota(I32, (tm, 1), 0)
    G = POOL_GROUP_DIM
    for g, w in enumerate(POOL_WINDOWS):
        s = cat[:, g * G:(g + 1) * G]
        k = 1
        while k < w:
            s = s + pltpu.roll(s, k, 0)
            k *= 2
        win = s[POOL_HALO:]
        cnt = jnp.minimum(t + 1, w).astype(F32)
        p = win / cnt - cur[:, g * G:(g + 1) * G]
        y = jnp.dot(p.astype(BF16), w_ref[g].astype(BF16), preferred_element_type=F32)
        o_ref[:, g * G:(g + 1) * G] = (y * sc_ref[:, g * G:(g + 1) * G]).astype(o_ref.dtype)


def _pool(u, pool_w, layer, pool_scale, tm=512):
    T = u.shape[0]
    tm = min(tm, T)
    assert T % tm == 0 and tm % POOL_HALO == 0 and max(POOL_WINDOWS) <= POOL_HALO
    hb = tm // POOL_HALO
    return pl.pallas_call(
        _pool_kernel,
        out_shape=jax.ShapeDtypeStruct((T, POOL_DIM), BF16),
        grid=(T // tm,),
        in_specs=[
            pl.BlockSpec((tm, POOL_DIM), lambda i: (i, 0)),
            pl.BlockSpec((POOL_HALO, POOL_DIM), lambda i: (jnp.maximum(i * hb - 1, 0), 0)),
            pl.BlockSpec((None,) + pool_w.shape[1:], lambda i: (layer, 0, 0, 0)),
            pl.BlockSpec((1, POOL_DIM), lambda i: (0, 0)),
        ],
        out_specs=pl.BlockSpec((tm, POOL_DIM), lambda i: (i, 0)),
        compiler_params=_cparams(("arbitrary",)),
        name="pool",
    )(u, u, pool_w, pool_scale.reshape(1, POOL_DIM))


DIL_SPAN = BLOCK * max(d for _, d in DIL_PATTERNS)
DIL_UNROLL = 8


def _dil_kernel(*refs):
    ng = len(DIL_PATTERNS)
    in_refs = refs[:5 * ng]
    o_ref = refs[5 * ng]
    kf_ref, vf_ref, os_ref, ms_ref, ls_ref = refs[5 * ng + 1:]
    b = pl.program_id(1)
    scale = HEAD_DIM ** -0.5

    qi = lax.broadcasted_iota(I32, (BLOCK, 2 * BLOCK), 0)
    kj = lax.broadcasted_iota(I32, (BLOCK, 2 * BLOCK), 1)
    rel = BLOCK + qi - kj
    for g, (window, d) in enumerate(DIL_PATTERNS):
        q_ref, kc_ref, kp_ref, vc_ref, vp_ref = in_refs[5 * g:5 * g + 5]
        count = window // d
        band = (rel >= 0) & (rel <= count)
        bias_std = jnp.where(band, 0.0, NEG)
        bias_first = jnp.where(band & (kj >= BLOCK), 0.0, NEG)
        hist = BLOCK * d
        nsb = DIL_SPAN // hist
        kf_ref[0:hist, :] = kp_ref[...]
        kf_ref[hist:hist + DIL_SPAN, :] = kc_ref[...]
        vf_ref[0:hist, :] = vp_ref[...]
        vf_ref[hist:hist + DIL_SPAN, :] = vc_ref[...]

        def block(idx, g=g, d=d, hist=hist, nsb=nsb, q_ref=q_ref, bias_std=bias_std, bias_first=bias_first):
            r = idx // nsb
            sb = idx % nsb
            start = r + hist * sb
            q = q_ref[pl.ds(start, BLOCK, stride=d), :].astype(BF16)
            k = kf_ref[pl.ds(start, 2 * BLOCK, stride=d), :].astype(BF16)
            v = vf_ref[pl.ds(start, 2 * BLOCK, stride=d), :].astype(BF16)
            s = lax.dot_general(q, k, (((1,), (1,)), ((), ())), preferred_element_type=F32) * scale
            first = jnp.logical_and(b == 0, sb == 0)
            s = s + jnp.where(first, bias_first, bias_std)
            m = jnp.max(s, axis=-1, keepdims=True)
            p = jnp.exp(s - m)
            l = jnp.sum(p, axis=-1, keepdims=True)
            o = jnp.dot(p.astype(BF16), v, preferred_element_type=F32)
            os_ref[g, pl.ds(start, BLOCK, stride=d), :] = o
            ms_ref[g, pl.ds(start, BLOCK, stride=d), :] = jnp.broadcast_to(m, (BLOCK, LANES))
            ls_ref[g, pl.ds(start, BLOCK, stride=d), :] = jnp.broadcast_to(l, (BLOCK, LANES))

        def body(i, _, block=block):
            for u in range(DIL_UNROLL):
                block(i * DIL_UNROLL + u)
            return 0

        assert (d * nsb) % DIL_UNROLL == 0
        lax.fori_loop(0, d * nsb // DIL_UNROLL, body, 0)

    mx = ms_ref[0]
    for g in range(1, ng):
        mx = jnp.maximum(mx, ms_ref[g])
    num = jnp.zeros((DIL_SPAN, LANES), F32)
    den = jnp.zeros((DIL_SPAN, LANES), F32)
    for g in range(ng):
        e = jnp.exp(ms_ref[g] - mx)
        num = num + e * os_ref[g]
        den = den + e * ls_ref[g]
    o_ref[...] = (num / den).astype(o_ref.dtype)


def _dilated(qkv):
    T = qkv.shape[1]
    assert T % DIL_SPAN == 0 and HEAD_DIM == LANES
    H = DIL_HEADS_PER_GROUP
    in_specs, args = [], []
    max_hist = 0
    for g, (window, d) in enumerate(DIL_PATTERNS):
        assert window // d <= BLOCK and DIL_SPAN % (BLOCK * d) == 0
        hist = BLOCK * d
        max_hist = max(max_hist, hist)
        ratio = DIL_SPAN // hist

        def cur(j, b, base):
            return (base + j, b, 0)

        def prev(j, b, base, ratio=ratio):
            return (base + j, jnp.maximum(b * ratio - 1, 0), 0)

        qh, kh, vh = g * H, DIL_HEADS + g * H, 2 * DIL_HEADS + g * H
        in_specs += [
            pl.BlockSpec((None, DIL_SPAN, LANES), functools.partial(cur, base=qh)),
            pl.BlockSpec((None, DIL_SPAN, LANES), functools.partial(cur, base=kh)),
            pl.BlockSpec((None, hist, LANES), functools.partial(prev, base=kh)),
            pl.BlockSpec((None, DIL_SPAN, LANES), functools.partial(cur, base=vh)),
            pl.BlockSpec((None, hist, LANES), functools.partial(prev, base=vh)),
        ]
        args += [qkv] * 5
    ng = len(DIL_PATTERNS)
    return pl.pallas_call(
        _dil_kernel,
        out_shape=jax.ShapeDtypeStruct((T, DIL_OUT), BF16),
        grid=(H, T // DIL_SPAN),
        in_specs=in_specs,
        out_specs=pl.BlockSpec((DIL_SPAN, LANES), lambda j, b: (b, j)),
        scratch_shapes=[
            pltpu.VMEM((max_hist + DIL_SPAN, LANES), F32),
            pltpu.VMEM((max_hist + DIL_SPAN, LANES), F32),
            pltpu.VMEM((ng, DIL_SPAN, LANES), F32),
            pltpu.VMEM((ng, DIL_SPAN, LANES), F32),
            pltpu.VMEM((ng, DIL_SPAN, LANES), F32),
        ],
        compiler_params=_cparams(("arbitrary", "arbitrary")),
        name="dilated",
    )(*args)


def _out_ln_kernel(*refs, n_in):
    a_refs = refs[:n_in]
    w_refs = refs[n_in:2 * n_in]
    x_ref, g_ref, b_ref, o_ref, ob_ref = refs[2 * n_in:]
    acc = ALPHA * x_ref[...]
    for a_ref, w_ref in zip(a_refs, w_refs):
        acc = acc + jnp.dot(a_ref[...], w_ref[...], preferred_element_type=F32)
    y = _layer_norm(acc, g_ref[...], b_ref[...])
    o_ref[...] = y
    ob_ref[...] = y.astype(BF16)


def _out_ln(a_list, wb, x, g, b, tm=512):
    T, D = x.shape
    tm = min(tm, T)
    n_in = len(a_list)
    kk = a_list[0].shape[1]
    assert all(a.shape[1] == kk for a in a_list) and wb.shape[0] == n_in * kk and T % tm == 0
    w3 = wb.reshape(n_in, kk, D)
    in_specs = [pl.BlockSpec((tm, kk), lambda i: (i, 0)) for _ in range(n_in)]
    in_specs += [pl.BlockSpec((None, kk, D), functools.partial(lambda i, c: (c, 0, 0), c=c)) for c in range(n_in)]
    in_specs += [pl.BlockSpec((tm, D), lambda i: (i, 0)),
                 pl.BlockSpec((1, D), lambda i: (0, 0)),
                 pl.BlockSpec((1, D), lambda i: (0, 0))]
    return pl.pallas_call(
        functools.partial(_out_ln_kernel, n_in=n_in),
        out_shape=(jax.ShapeDtypeStruct((T, D), F32), jax.ShapeDtypeStruct((T, D), BF16)),
        grid=(T // tm,),
        in_specs=in_specs,
        out_specs=(pl.BlockSpec((tm, D), lambda i: (i, 0)), pl.BlockSpec((tm, D), lambda i: (i, 0))),
        compiler_params=_cparams(("arbitrary",)),
        name="out_ln",
    )(*a_list, *([w3] * n_in), x, g.reshape(1, D), b.reshape(1, D))


def _res_ln_kernel(x_ref, y_ref, g_ref, b_ref, o_ref, ob_ref):
    y = _layer_norm(ALPHA * x_ref[...] + y_ref[...], g_ref[...], b_ref[...])
    o_ref[...] = y
    ob_ref[...] = y.astype(BF16)


def _res_ln(x, y, g, b, tm=512):
    T, D = x.shape
    tm = min(tm, T)
    row = pl.BlockSpec((tm, D), lambda i: (i, 0))
    vec = pl.BlockSpec((1, D), lambda i: (0, 0))
    return pl.pallas_call(
        _res_ln_kernel,
        out_shape=(jax.ShapeDtypeStruct((T, D), F32), jax.ShapeDtypeStruct((T, D), BF16)),
        grid=(T // tm,),
        in_specs=[row, row, vec, vec],
        out_specs=(row, row),
        compiler_params=_cparams(("arbitrary",)),
        name="res_ln",
    )(x, y, g.reshape(1, D), b.reshape(1, D))


FFN_TM = 1024
FFN_TF = 512
FFN_CHUNK = 256


def _ffn_kernel(vt_ref, ve_ref, vlo_ref, vhi_ref, vok_ref, x_ref, wg_ref, wu_ref, wd_ref, o_ref):
    v = pl.program_id(0)
    f = pl.program_id(1)
    lo = vlo_ref[v]
    hi = vhi_ref[v]

    @pl.when(jnp.logical_and(vok_ref[v] == 2, f == 0))
    def _():
        o_ref[...] = jnp.zeros(o_ref.shape, o_ref.dtype)

    def rows(c):
        return pl.ds(pl.multiple_of(c * FFN_CHUNK, FFN_CHUNK), FFN_CHUNK)

    @pl.when(f == 0)
    def _():
        def zero(c, _):
            o_ref[rows(c), :] = jnp.zeros((FFN_CHUNK, o_ref.shape[1]), o_ref.dtype)
            return 0
        lax.fori_loop(lo, hi, zero, 0)

    def chunk(c):
        x = x_ref[rows(c), :]
        g = jnp.dot(x, wg_ref[...].astype(BF16), preferred_element_type=F32)
        u = jnp.dot(x, wu_ref[...].astype(BF16), preferred_element_type=F32)
        h = (g * jax.nn.sigmoid(g) * u).astype(BF16)
        o_ref[rows(c), :] += jnp.dot(h, wd_ref[...].astype(BF16), preferred_element_type=F32)

    def pair(i, _):
        chunk(lo + 2 * i)
        chunk(lo + 2 * i + 1)
        return 0

    n = hi - lo
    lax.fori_loop(0, n // 2, pair, 0)

    @pl.when(n % 2 == 1)
    def _():
        chunk(hi - 1)


def _ffn(visits, xs, wg, wu, wd):
    vt, ve, vlo, vhi, vok = visits
    R, D = xs.shape
    F = wg.shape[2]
    assert R % FFN_TM == 0 and F % FFN_TF == 0 and FFN_TM % FFN_CHUNK == 0
    nf = F // FFN_TF
    V = vt.shape[0]

    def fidx(v, f, vok):
        return jnp.where(vok[v] == 1, f, nf - 1)

    return pl.pallas_call(
        _ffn_kernel,
        out_shape=jax.ShapeDtypeStruct((R, D), F32),
        grid_spec=pltpu.PrefetchScalarGridSpec(
            num_scalar_prefetch=5,
            grid=(V, nf),
            in_specs=[
                pl.BlockSpec((FFN_TM, D), lambda v, f, vt, ve, vlo, vhi, vok: (vt[v], 0)),
                pl.BlockSpec((None, D, FFN_TF), lambda v, f, vt, ve, vlo, vhi, vok: (ve[v], 0, fidx(v, f, vok))),
                pl.BlockSpec((None, D, FFN_TF), lambda v, f, vt, ve, vlo, vhi, vok: (ve[v], 0, fidx(v, f, vok))),
                pl.BlockSpec((None, FFN_TF, D), lambda v, f, vt, ve, vlo, vhi, vok: (ve[v], fidx(v, f, vok), 0)),
            ],
            out_specs=pl.BlockSpec((FFN_TM, D), lambda v, f, vt, ve, vlo, vhi, vok: (vt[v], 0)),
        ),
        compiler_params=_cparams(("arbitrary", "arbitrary")),
        name="ffn",
    )(vt, ve, vlo, vhi, vok, xs, wg, wu, wd)


def _dense_visits(T, layer):
    nt = T // FFN_TM
    cpt = FFN_TM // FFN_CHUNK
    z = jnp.zeros((nt,), I32)
    return (jnp.arange(nt, dtype=I32), z + layer, z, z + cpt, z + 1)


def _router_kernel(x_ref, r_ref, o_ref):
    logits = jnp.dot(x_ref[...], r_ref[...], preferred_element_type=F32, precision=lax.Precision.HIGHEST)
    lane = lax.broadcasted_iota(I32, logits.shape, 1)
    logits = jnp.where(lane < N_EXPERTS, logits, -jnp.inf)
    v1 = jnp.max(logits, axis=-1, keepdims=True)
    i1 = jnp.min(jnp.where(logits == v1, lane, LANES), axis=-1, keepdims=True)
    rest = jnp.where(lane == i1, -jnp.inf, logits)
    v2 = jnp.max(rest, axis=-1, keepdims=True)
    i2 = jnp.min(jnp.where(rest == v2, lane, LANES), axis=-1, keepdims=True)
    e2 = jnp.exp(v2 - v1)
    g1 = 1.0 / (1.0 + e2)
    g2 = e2 / (1.0 + e2)
    out = jnp.where(lane == 0, i1.astype(F32), 0.0)
    out = jnp.where(lane == 1, i2.astype(F32), out)
    out = jnp.where(lane == 2, g1, out)
    out = jnp.where(lane == 3, g2, out)
    o_ref[...] = out


def _router(x, router, tm=512):
    T, D = x.shape
    tm = min(tm, T)
    rp = jnp.pad(router, ((0, 0), (0, LANES - N_EXPERTS)))
    return pl.pallas_call(
        _router_kernel,
        out_shape=jax.ShapeDtypeStruct((T, LANES), F32),
        grid=(T // tm,),
        in_specs=[pl.BlockSpec((tm, D), lambda i: (i, 0)), pl.BlockSpec((D, LANES), lambda i: (0, 0))],
        out_specs=pl.BlockSpec((tm, LANES), lambda i: (i, 0)),
        compiler_params=_cparams(("arbitrary",)),
        name="router",
    )(x, rp)


GATHER_ROWS = 256


def _dispatch_kernel(p0_ref, p1_ref, x_ref, init_hbm, o_hbm, sem):
    del init_hbm
    i = pl.program_id(0)

    def issue(r, _):
        t = i * GATHER_ROWS + r
        pltpu.make_async_copy(x_ref.at[r], o_hbm.at[p0_ref[t]], sem).start()
        pltpu.make_async_copy(x_ref.at[r], o_hbm.at[p1_ref[t]], sem).start()
        return 0

    lax.fori_loop(0, GATHER_ROWS, issue, 0)
    for _ in range(TOP_K):
        pltpu.make_async_copy(x_ref, o_hbm.at[pl.ds(0, GATHER_ROWS)], sem).wait()


def _dispatch(x3, p0, p1, R):
    T, S, _ = x3.shape
    assert T % GATHER_ROWS == 0 and R >= GATHER_ROWS
    init = jnp.zeros((R, S, LANES), x3.dtype)
    return pl.pallas_call(
        _dispatch_kernel,
        out_shape=jax.ShapeDtypeStruct((R, S, LANES), x3.dtype),
        grid_spec=pltpu.PrefetchScalarGridSpec(
            num_scalar_prefetch=2,
            grid=(T // GATHER_ROWS,),
            in_specs=[pl.BlockSpec((GATHER_ROWS, S, LANES), lambda i, p0, p1: (i, 0, 0)),
                      pl.BlockSpec(memory_space=pl.ANY)],
            out_specs=pl.BlockSpec(memory_space=pl.ANY),
            scratch_shapes=[pltpu.SemaphoreType.DMA(())],
        ),
        input_output_aliases={3: 0},
        compiler_params=_cparams(("arbitrary",)),
        name="dispatch",
    )(p0, p1, x3, init)


def _combine_kernel(p0_ref, p1_ref, y_hbm, r_ref, x_ref, g_ref, b_ref, o_ref, ob_ref, buf, sem):
    i = pl.program_id(0)
    slot = i % 2

    def issue(step, s):
        def body(r, _):
            t = step * GATHER_ROWS + r
            pltpu.make_async_copy(y_hbm.at[p0_ref[t]], buf.at[s, 0, r], sem.at[s]).start()
            pltpu.make_async_copy(y_hbm.at[p1_ref[t]], buf.at[s, 1, r], sem.at[s]).start()
            return 0
        lax.fori_loop(0, GATHER_ROWS, body, 0)

    @pl.when(i == 0)
    def _():
        issue(0, 0)

    @pl.when(i + 1 < pl.num_programs(0))
    def _():
        issue(i + 1, 1 - slot)

    for k in range(TOP_K):
        pltpu.make_async_copy(y_hbm.at[pl.ds(0, GATHER_ROWS)], buf.at[slot, k], sem.at[slot]).wait()
    route = r_ref[...]
    g0 = route[:, :, TOP_K:TOP_K + 1]
    g1 = route[:, :, TOP_K + 1:TOP_K + 2]
    z = ALPHA * x_ref[...] + (g0 * buf[slot, 0] + g1 * buf[slot, 1])
    n = z.shape[1] * z.shape[2]
    mu = jnp.sum(z, axis=(1, 2), keepdims=True) / n
    zc = z - mu
    var = jnp.sum(zc * zc, axis=(1, 2), keepdims=True) / n
    y = zc * lax.rsqrt(var + LN_EPS) * g_ref[...] + b_ref[...]
    o_ref[...] = y
    ob_ref[...] = y.astype(BF16)


def _combine_ln(y3, p0, p1, route, x3, g, b):
    T, S, _ = x3.shape
    assert T % GATHER_ROWS == 0
    row = pl.BlockSpec((GATHER_ROWS, S, LANES), lambda i, p0, p1: (i, 0, 0))
    vec = pl.BlockSpec((1, S, LANES), lambda i, p0, p1: (0, 0, 0))
    rspec = pl.BlockSpec((GATHER_ROWS, 1, LANES), lambda i, p0, p1: (i, 0, 0))
    return pl.pallas_call(
        _combine_kernel,
        out_shape=(jax.ShapeDtypeStruct((T, S, LANES), F32), jax.ShapeDtypeStruct((T, S, LANES), BF16)),
        grid_spec=pltpu.PrefetchScalarGridSpec(
            num_scalar_prefetch=2,
            grid=(T // GATHER_ROWS,),
            in_specs=[pl.BlockSpec(memory_space=pl.ANY), rspec, row, vec, vec],
            out_specs=(row, row),
            scratch_shapes=[pltpu.VMEM((2, TOP_K, GATHER_ROWS, S, LANES), F32), pltpu.SemaphoreType.DMA((2,))],
        ),
        compiler_params=_cparams(("arbitrary",)),
        name="combine_ln",
    )(p0, p1, y3, route.reshape(T, 1, LANES), x3, g.reshape(1, S, LANES), b.reshape(1, S, LANES))


def _moe_plan(route, T, layer):
    E = N_EXPERTS
    e = route[:, :TOP_K].astype(I32)
    ef = e.reshape(-1)
    onehot = (ef[:, None] == jnp.arange(E, dtype=I32)[None, :]).astype(I32)
    rank = jnp.cumsum(onehot, axis=0) - onehot
    counts = jnp.sum(onehot, axis=0)
    chunks = (counts + FFN_CHUNK - 1) // FFN_CHUNK
    seg_end = jnp.cumsum(chunks)
    seg_start = seg_end - chunks
    pos = seg_start[ef] * FFN_CHUNK + jnp.sum(rank * onehot, axis=1)
    R = -(-(TOP_K * T + E * FFN_CHUNK) // FFN_TM) * FFN_TM
    cpt = FFN_TM // FFN_CHUNK
    nt = R // FFN_TM
    n_slots = nt + E
    slot = jnp.arange(n_slots, dtype=I32)

    def place(vals, dest):
        return jnp.sum(jnp.where(dest[None, :] == slot[:, None], vals[None, :], 0), axis=1).astype(I32)

    tile_pts = jnp.arange(nt, dtype=I32) * cpt
    seg_pts = seg_start.astype(I32)
    tile_rank = jnp.arange(nt, dtype=I32) + jnp.sum(seg_pts[None, :] < tile_pts[:, None], axis=1)
    seg_rank = jnp.arange(E, dtype=I32) + jnp.sum(tile_pts[None, :] <= seg_pts[:, None], axis=1)
    pts = place(jnp.concatenate([tile_pts, seg_pts]), jnp.concatenate([tile_rank, seg_rank]))
    used = seg_end[-1]
    used_tiles_end = (used + cpt - 1) // cpt * cpt
    nxt = jnp.concatenate([pts[1:], jnp.array([nt * cpt], I32)])
    ok = (pts < used) & (nxt > pts)
    nxt = jnp.where(nxt >= used, jnp.maximum(used_tiles_end, pts), nxt)
    tile = pts // cpt
    nxt = jnp.minimum(nxt, (tile + 1) * cpt)
    exp = jnp.minimum(jnp.sum(seg_end[None, :] <= pts[:, None], axis=1), E - 1).astype(I32)
    last = jnp.max(jnp.where(ok, slot, 0))
    vt = jnp.where(ok, tile, jnp.sum(jnp.where(slot == last, tile, 0)))
    ve = jnp.where(ok, exp, jnp.sum(jnp.where(slot == last, exp, 0)))
    vlo = jnp.where(ok, pts - tile * cpt, 0)
    vhi = jnp.where(ok, nxt - tile * cpt, 0)
    oki = ok.astype(I32)
    dest = jnp.where(ok, jnp.cumsum(oki) - oki, jnp.sum(oki) + jnp.cumsum(1 - oki) - (1 - oki))
    vt, ve, vlo, vhi, vok = (place(a, dest) for a in (vt, ve, vlo, vhi, oki))
    used_tiles = used_tiles_end // cpt
    fill = slot - (n_slots - (nt - used_tiles))
    vt = jnp.where(fill >= 0, used_tiles + fill, vt).astype(I32)
    vok = jnp.where(fill >= 0, 2, vok).astype(I32)
    visits = (vt, (ve + layer * E).astype(I32), vlo, vhi, vok)
    p = pos.reshape(T, TOP_K).astype(I32)
    return R, visits, p[:, 0], p[:, 1]


DSA_TK = 512
DSA_SLAB = 128
NT_DIMS = (((1,), (1,)), ((), ()))


def _dsa_kernel(q_ref, iq_ref, iw_ref, ike_ref, iko_ref, kt_ref, v_ref, o_ref,
                key_s, bias_s, s_s, p_s, m_s, l_s, al_s, acc_s, *, ksel, idx_bits):
    n = pl.program_id(0)
    TK = DSA_TK
    nkt = (n * BLOCK + BLOCK + TK - 1) // TK
    qpos = n * BLOCK + lax.broadcasted_iota(I32, (BLOCK, 1), 0)
    lane_k = lax.broadcasted_iota(I32, (1, TK), 1)

    iqb = iq_ref[...]
    npair = IDX_HEADS // 2
    lhs = jnp.concatenate([iqb[:, p * LANES:(p + 1) * LANES] for p in range(npair)], axis=0)
    wts = iw_ref[...] * ((IDX_DIM ** -0.5) * (IDX_HEADS ** -0.5))

    def score_tile(j, _):
        le = lax.dot_general(lhs, ike_ref[j], NT_DIMS, preferred_element_type=F32)
        lo = lax.dot_general(lhs, iko_ref[j], NT_DIMS, preferred_element_type=F32)
        sc = jnp.zeros((BLOCK, TK), F32)
        for p in range(npair):
            sc = sc + jnp.maximum(le[p * BLOCK:(p + 1) * BLOCK], 0.0) * wts[:, 2 * p:2 * p + 1]
            sc = sc + jnp.maximum(lo[p * BLOCK:(p + 1) * BLOCK], 0.0) * wts[:, 2 * p + 1:2 * p + 2]
        bits = pltpu.bitcast(sc, I32)
        key = bits ^ ((bits >> 31) & 0x7FFFFFFF)
        key_s[j] = jnp.where(j * TK + lane_k <= qpos, key, INT_MIN)
        return 0

    lax.fori_loop(0, nkt, score_tile, 0)

    def count(pred):
        def body(j, acc):
            hit = jnp.where(pred(key_s[j], j * TK + lane_k), 1.0, 0.0)
            for c in range(TK // LANES):
                acc = acc + hit[:, c * LANES:(c + 1) * LANES]
            return acc
        acc = lax.fori_loop(0, nkt, body, jnp.zeros((BLOCK, LANES), F32))
        return jnp.sum(acc, axis=1, keepdims=True)

    searching = (n + 1) * BLOCK > ksel

    def write_bias(thr, pidx):
        def body(j, _):
            key = key_s[j]
            sel = (key > thr) | ((key == thr) & (j * TK + lane_k <= pidx))
            bias_s[j] = jnp.where(sel, 0.0, NEG)
            return 0
        lax.fori_loop(0, nkt, body, 0)

    @pl.when(jnp.logical_not(searching))
    def _():
        write_bias(jnp.full((BLOCK, 1), INT_MIN, I32), jnp.full((BLOCK, 1), -1, I32))

    @pl.when(searching)
    def _():
        def bit_step(carry):
            i, thr, cnt_thr, _ = carry
            cand = thr + jnp.left_shift(jnp.int32(1), 31 - i)
            cnt = count(lambda key, kidx: key >= cand)
            ok = cnt >= ksel
            thr = jnp.where(ok, cand, thr)
            cnt_thr = jnp.where(ok, cnt, cnt_thr)
            return i + 1, thr, cnt_thr, jnp.max(jnp.abs(cnt_thr - ksel))

        thr0 = jnp.full((BLOCK, 1), INT_MIN, I32)
        _, thr, _, miss = lax.while_loop(
            lambda c: jnp.logical_and(c[0] < 32, c[3] > 0.0), bit_step,
            (jnp.int32(0), thr0, jnp.full((BLOCK, 1), -1.0, F32), jnp.float32(1.0)))
        exact = miss == 0.0

        @pl.when(exact)
        def _():
            write_bias(thr, jnp.full((BLOCK, 1), 2 ** 30, I32))

        @pl.when(jnp.logical_not(exact))
        def _():
            need = ksel - count(lambda key, kidx: key > thr)

            def idx_step(i, pidx):
                cand = pidx + jnp.left_shift(jnp.int32(1), idx_bits - 1 - i)
                c = count(lambda key, kidx: (key == thr) & (kidx < cand))
                return jnp.where(c < need, cand, pidx)

            pidx = lax.fori_loop(0, idx_bits, idx_step, jnp.zeros((BLOCK, 1), I32))
            write_bias(thr, pidx)

    qb = q_ref[...]
    qs = jnp.concatenate([qb[:, h * LANES:(h + 1) * LANES] for h in range(DSA_HEADS)], axis=0)
    c2 = (HEAD_DIM ** -0.5) * LOG2_E
    ncol = TK // LANES
    m_s[...] = jnp.full(m_s.shape, NEG, F32)
    l_s[...] = jnp.zeros(l_s.shape, F32)
    acc_s[...] = jnp.zeros(acc_s.shape, F32)

    def attend(j, _):
        s_s[...] = lax.dot_general(qs, kt_ref[j], NT_DIMS, preferred_element_type=F32)
        for slab in range(DSA_HEADS * BLOCK // DSA_SLAB):
            r = slice(slab * DSA_SLAB, (slab + 1) * DSA_SLAB)
            q0 = (slab * DSA_SLAB) % BLOCK
            t = [s_s[r, c * LANES:(c + 1) * LANES] * c2 + bias_s[j, q0:q0 + DSA_SLAB, c * LANES:(c + 1) * LANES]
                 for c in range(ncol)]
            mx = t[0]
            for c in range(1, ncol):
                mx = jnp.maximum(mx, t[c])
            m_old = m_s[r, :]
            m_new = jnp.maximum(m_old, jnp.max(mx, axis=-1, keepdims=True))
            alpha = jnp.exp2(m_old - m_new)
            psum = jnp.zeros((DSA_SLAB, LANES), F32)
            for c in range(ncol):
                p = jnp.exp2(t[c] - m_new)
                psum = psum + p
                p_s[r, c * LANES:(c + 1) * LANES] = p.astype(BF16)
            m_s[r, :] = m_new
            l_s[r, :] = alpha * l_s[r, :] + jnp.sum(psum, axis=-1, keepdims=True)
            al_s[r, :] = alpha
        acc_s[...] = al_s[...] * acc_s[...] + jnp.dot(p_s[...], v_ref[j], preferred_element_type=F32)
        return 0

    lax.fori_loop(0, nkt, attend, 0)
    for h in range(DSA_HEADS):
        r = slice(h * BLOCK, (h + 1) * BLOCK)
        o_ref[:, h * LANES:(h + 1) * LANES] = (acc_s[r, :] / l_s[r, :]).astype(o_ref.dtype)


def _dsa(q, iq, iw, ik, k, v):
    T = q.shape[0]
    TK = DSA_TK
    assert T % TK == 0 and TK % BLOCK == 0 and HEAD_DIM == LANES and 2 * IDX_DIM == LANES
    ksel = min(DSA_TOPK, T // 4)
    assert ksel % BLOCK == 0
    nt = T // TK
    rows = DSA_HEADS * BLOCK
    z = jnp.zeros_like(ik)

    def tiles_t(a):
        return a.reshape(nt, TK, a.shape[1])

    ike = tiles_t(jnp.concatenate([ik, z], axis=1))
    iko = tiles_t(jnp.concatenate([z, ik], axis=1))
    kt = tiles_t(k)
    v3 = v.reshape(nt, TK, HEAD_DIM)
    full3 = lambda a: pl.BlockSpec(a.shape, lambda n: (0, 0, 0))
    return pl.pallas_call(
        functools.partial(_dsa_kernel, ksel=ksel, idx_bits=max(1, (T - 1).bit_length())),
        out_shape=jax.ShapeDtypeStruct((T, DSA_Q), BF16),
        grid=(T // BLOCK,),
        in_specs=[
            pl.BlockSpec((BLOCK, DSA_Q), lambda n: (n, 0)),
            pl.BlockSpec((BLOCK, IDX_HEADS * IDX_DIM), lambda n: (n, 0)),
            pl.BlockSpec((BLOCK, LANES), lambda n: (n, 0)),
            full3(ike), full3(iko), full3(kt), full3(v3),
        ],
        out_specs=pl.BlockSpec((BLOCK, DSA_Q), lambda n: (n, 0)),
        scratch_shapes=[
            pltpu.VMEM((nt, BLOCK, TK), I32),
            pltpu.VMEM((nt, BLOCK, TK), F32),
            pltpu.VMEM((rows, TK), F32),
            pltpu.VMEM((rows, TK), BF16),
            pltpu.VMEM((rows, LANES), F32),
            pltpu.VMEM((rows, LANES), F32),
            pltpu.VMEM((rows, LANES), F32),
            pltpu.VMEM((rows, HEAD_DIM), F32),
        ],
        compiler_params=_cparams(("arbitrary",)),
        name="dsa",
    )(q, iq, iw, ike, iko, kt, v3)


def _even_layer(x, xb, rope, i, w_in, pool_w, pool_scale, w_out, ln1_g, ln1_b, wg, wu, wd, ln2_g, ln2_b):
    T = x.shape[0]
    cos2, sin2 = rope
    u = _proj(xb, w_in, i, cos2, sin2, col0=0, ncols=POOL_DIM, tn=1024, rope_cols=0, out_dtype=F32)
    qkv = _proj(xb, w_in, i, cos2, sin2, col0=POOL_DIM, ncols=3 * DIL_QKV, tn=1024, rope_cols=2 * DIL_QKV,
                out_dtype=F32, head_major=True)
    pooled = _pool(u, pool_w, i, pool_scale[i])
    attn = _dilated(qkv)
    x1, x1b = _out_ln([pooled, attn], w_out[i].astype(BF16), x, ln1_g[i], ln1_b[i])
    y = _ffn(_dense_visits(T, i), x1b, wg, wu, wd)
    return _res_ln(x1, y, ln2_g[i], ln2_b[i])


def _odd_layer(x, xb, rope, i, w_in, w_out, ln1_g, ln1_b, router, wg, wu, wd, ln2_g, ln2_b):
    T, D = x.shape
    cos2, sin2 = rope
    q = _proj(xb, w_in, i, cos2, sin2, col0=0, ncols=DSA_Q, tn=1024, rope_cols=DSA_Q, out_dtype=BF16)
    n_rest = w_in.shape[2] - DSA_Q
    n_pad = -(-n_rest // LANES) * LANES
    w_rest = jnp.pad(w_in[i, :, DSA_Q:], ((0, 0), (0, n_pad - n_rest)))[None]
    rest = _proj(xb, w_rest, 0, cos2, sin2, col0=0, ncols=n_pad, tn=n_pad, rope_cols=HEAD_DIM, out_dtype=F32)
    o = 0
    k = rest[:, o:o + HEAD_DIM].astype(BF16); o += HEAD_DIM
    v = rest[:, o:o + HEAD_DIM].astype(BF16); o += HEAD_DIM
    iq = rest[:, o:o + IDX_HEADS * IDX_DIM].astype(BF16); o += IDX_HEADS * IDX_DIM
    ik = rest[:, o:o + IDX_DIM].astype(BF16); o += IDX_DIM
    iw = jnp.pad(rest[:, o:o + IDX_HEADS], ((0, 0), (0, LANES - IDX_HEADS)))
    attn = _dsa(q, iq, iw, ik, k, v)
    x1, x1b = _out_ln([attn], w_out[i].astype(BF16), x, ln1_g[i], ln1_b[i])
    route = _router(x1, router[i])
    R, visits, p0, p1 = _moe_plan(route, T, i)
    S = D // LANES
    xs = _dispatch(x1b.reshape(T, S, LANES), p0, p1, R)
    E, F = wg.shape[1], wg.shape[3]
    y = _ffn(visits, xs.reshape(R, D), wg.reshape(-1, D, F), wu.reshape(-1, D, F), wd.reshape(-1, F, D))
    x2, x2b = _combine_ln(y.reshape(R, S, LANES), p0, p1, route, x1.reshape(T, S, LANES), ln2_g[i], ln2_b[i])
    return x2.reshape(T, D), x2b.reshape(T, D)


def kernel(x, ev_w_in, ev_pool_w, ev_pool_scale, ev_w_out, ev_ln1_g, ev_ln1_b, ev_ffn_w_gate, ev_ffn_w_up,
           ev_ffn_w_down, ev_ln2_g, ev_ln2_b, od_w_in, od_w_out, od_ln1_g, od_ln1_b, od_router,
           od_exp_w_gate, od_exp_w_up, od_exp_w_down, od_ln2_g, od_ln2_b):
    B, T, D = x.shape
    rope = _rope_tables(T)
    outs = []
    for bi in range(B):
        h = x[bi]
        hb = h.astype(BF16)
        for layer in range(DEPTH):
            i = layer // 2
            if layer % 2 == 0:
                h, hb = _even_layer(h, hb, rope, i, ev_w_in, ev_pool_w, ev_pool_scale, ev_w_out,
                                    ev_ln1_g, ev_ln1_b, ev_ffn_w_gate, ev_ffn_w_up,
                                    ev_ffn_w_down, ev_ln2_g, ev_ln2_b)
            else:
                h, hb = _odd_layer(h, hb, rope, i, od_w_in, od_w_out, od_ln1_g, od_ln1_b,
                                   od_router, od_exp_w_gate, od_exp_w_up, od_exp_w_down,
                                   od_ln2_g, od_ln2_b)
        outs.append(h)
    return jnp.stack(outs, axis=0)
```

```python
import functools

import jax
import jax.numpy as jnp
from jax import lax
from jax.experimental import pallas as pl
from jax.experimental.pallas import tpu as pltpu

F32 = jnp.float32
BF16 = jnp.bfloat16
I32 = jnp.int32

D_MODEL = 2048
DEPTH = 4
HEAD_DIM = 128
ROPE_THETA = 10000.0
LN_EPS = 1e-5
BLOCK = 128
ALPHA = (2 * DEPTH) ** 0.25
POOL_WINDOWS = (2, 4, 8, 16)
POOL_GROUP_DIM = D_MODEL // 8
POOL_DIM = len(POOL_WINDOWS) * POOL_GROUP_DIM
DIL_PATTERNS = ((128, 1), (512, 4), (2048, 16))
DIL_HEADS_PER_GROUP = D_MODEL // 256
DIL_HEADS = len(DIL_PATTERNS) * DIL_HEADS_PER_GROUP
DIL_QKV = DIL_HEADS * HEAD_DIM
DIL_OUT = DIL_HEADS_PER_GROUP * HEAD_DIM
DSA_HEADS = D_MODEL // HEAD_DIM
IDX_HEADS = 16
IDX_DIM = 64
DSA_TOPK = 256
DSA_Q = DSA_HEADS * HEAD_DIM
D_FF = 5632
N_EXPERTS = 8
TOP_K = 2

LANES = 128
VMEM_LIMIT = 56 * 1024 * 1024

NEG = -1e30
LOG2_E = 1.4426950408889634
INT_MIN = -(2 ** 31)


def _cparams(sem):
    return pltpu.CompilerParams(dimension_semantics=sem, vmem_limit_bytes=VMEM_LIMIT)


def _layer_norm(z, g, b):
    mu = jnp.mean(z, axis=-1, keepdims=True)
    zc = z - mu
    var = jnp.mean(zc * zc, axis=-1, keepdims=True)
    return zc * lax.rsqrt(var + LN_EPS) * g + b


PROJ_HEADS_PER_DOT = 2


def _proj_kernel(x_ref, w_ref, cos_ref, sin_ref, o_ref, wb_ref, *, rope, n_tiles, head_major, out_scale):
    @pl.when(pl.program_id(1) == 0)
    def _():
        wb_ref[...] = w_ref[...].astype(BF16)

    tn = wb_ref.shape[1]
    nh = tn // LANES
    if n_tiles == 1:
        lo, hi = rope[0] // LANES, rope[1] // LANES
        cos, sin = cos_ref[...], sin_ref[...]
    else:
        j = pl.program_id(0)
        rot = jnp.logical_and(j >= rope[0] // tn, j < rope[1] // tn)
        lo, hi = (0, nh) if rope[1] > rope[0] else (0, 0)
        cos = jnp.where(rot, cos_ref[...], 1.0)
        sin = jnp.where(rot, sin_ref[...], 0.0)

    x = x_ref[...]
    step = min(nh, PROJ_HEADS_PER_DOT)
    for c0 in range(0, nh, step):
        n = min(step, nh - c0)
        acc = jnp.dot(x, wb_ref[:, c0 * LANES:(c0 + n) * LANES], preferred_element_type=F32)
        for c in range(c0, c0 + n):
            a = acc[:, (c - c0) * LANES:(c - c0 + 1) * LANES]
            if lo <= c < hi:
                a = a * cos + pltpu.roll(a, HEAD_DIM // 2, 1) * sin
            if out_scale != 1.0:
                a = a * out_scale
            if head_major:
                o_ref[c] = a.astype(o_ref.dtype)
            else:
                o_ref[:, c * LANES:(c + 1) * LANES] = a.astype(o_ref.dtype)


def _proj(xb, w, layer, cos2, sin2, *, col0, ncols, tn, rope, out_dtype, head_major=False, out_scale=1.0,
          tm=1024):
    T, K = xb.shape
    tm = min(tm, T)
    assert T % tm == 0 and ncols % tn == 0 and col0 % tn == 0 and tn % LANES == 0
    assert all(r % (LANES if ncols == tn else tn) == 0 for r in rope)
    n_tiles = ncols // tn
    c0 = col0 // tn
    if head_major:
        out_shape = jax.ShapeDtypeStruct((ncols // LANES, T, LANES), out_dtype)
        out_spec = pl.BlockSpec((tn // LANES, tm, LANES), lambda j, i: (j, i, 0))
    else:
        out_shape = jax.ShapeDtypeStruct((T, ncols), out_dtype)
        out_spec = pl.BlockSpec((tm, tn), lambda j, i: (i, j))
    return pl.pallas_call(
        functools.partial(_proj_kernel, rope=rope, n_tiles=n_tiles, head_major=head_major,
                          out_scale=float(out_scale)),
        out_shape=out_shape,
        grid=(n_tiles, T // tm),
        in_specs=[
            pl.BlockSpec((tm, K), lambda j, i: (i, 0)),
            pl.BlockSpec((None, K, tn), lambda j, i: (layer, 0, j + c0)),
            pl.BlockSpec((tm, LANES), lambda j, i: (i, 0)),
            pl.BlockSpec((tm, LANES), lambda j, i: (i, 0)),
        ],
        out_specs=out_spec,
        scratch_shapes=[pltpu.VMEM((K, tn), BF16)],
        compiler_params=_cparams(("arbitrary", "arbitrary")),
        name="proj",
    )(xb, w, cos2, sin2)


def _rope_tables(T):
    inv_freq = ROPE_THETA ** (-jnp.arange(0, HEAD_DIM, 2, dtype=F32) / HEAD_DIM)
    ang = jnp.arange(T, dtype=F32)[:, None] * inv_freq[None, :]
    cos, sin = jnp.cos(ang), jnp.sin(ang)
    return jnp.concatenate([cos, cos], axis=-1), jnp.concatenate([-sin, sin], axis=-1)


POOL_HALO = 16


def _pool_kernel(u_ref, h_ref, w_ref, sc_ref, o_ref):
    i = pl.program_id(0)
    tm = u_ref.shape[0]
    cur = u_ref[...]
    halo = jnp.where(i > 0, h_ref[...], 0.0)
    cat = jnp.concatenate([halo, cur], axis=0)
    t = i * tm + lax.broadcasted_iota(I32, (tm, 1), 0)
    G = POOL_GROUP_DIM
    for g, w in enumerate(POOL_WINDOWS):
        s = cat[:, g * G:(g + 1) * G]
        k = 1
        while k < w:
            s = s + pltpu.roll(s, k, 0)
            k *= 2
        win = s[POOL_HALO:]
        cnt = jnp.minimum(t + 1, w).astype(F32)
        p = win / cnt - cur[:, g * G:(g + 1) * G]
        y = jnp.dot(p.astype(BF16), w_ref[g].astype(BF16), preferred_element_type=F32)
        o_ref[:, g * G:(g + 1) * G] = (y * sc_ref[:, g * G:(g + 1) * G]).astype(o_ref.dtype)


def _pool(u, pool_w, layer, pool_scale, tm=512):
    T = u.shape[0]
    tm = min(tm, T)
    assert T % tm == 0 and tm % POOL_HALO == 0 and max(POOL_WINDOWS) <= POOL_HALO
    hb = tm // POOL_HALO
    return pl.pallas_call(
        _pool_kernel,
        out_shape=jax.ShapeDtypeStruct((T, POOL_DIM), BF16),
        grid=(T // tm,),
        in_specs=[
            pl.BlockSpec((tm, POOL_DIM), lambda i: (i, 0)),
            pl.BlockSpec((POOL_HALO, POOL_DIM), lambda i: (jnp.maximum(i * hb - 1, 0), 0)),
            pl.BlockSpec((None,) + pool_w.shape[1:], lambda i: (layer, 0, 0, 0)),
            pl.BlockSpec((1, POOL_DIM), lambda i: (0, 0)),
        ],
        out_specs=pl.BlockSpec((tm, POOL_DIM), lambda i: (i, 0)),
        compiler_params=_cparams(("arbitrary",)),
        name="pool",
    )(u, u, pool_w, pool_scale.reshape(1, POOL_DIM))


DIL_SPAN = BLOCK * max(d for _, d in DIL_PATTERNS)
DIL_UNROLL = 8


def _dil_kernel(*refs):
    ng = len(DIL_PATTERNS)
    in_refs = refs[:5 * ng]
    o_ref = refs[5 * ng]
    kf_ref, vf_ref, os_ref, ms_ref, ls_ref = refs[5 * ng + 1:]
    b = pl.program_id(1)
    scale = HEAD_DIM ** -0.5

    qi = lax.broadcasted_iota(I32, (BLOCK, 2 * BLOCK), 0)
    kj = lax.broadcasted_iota(I32, (BLOCK, 2 * BLOCK), 1)
    rel = BLOCK + qi - kj
    for g, (window, d) in enumerate(DIL_PATTERNS):
        q_ref, kc_ref, kp_ref, vc_ref, vp_ref = in_refs[5 * g:5 * g + 5]
        count = window // d
        band = (rel >= 0) & (rel <= count)
        bias_std = jnp.where(band, 0.0, NEG)
        bias_first = jnp.where(band & (kj >= BLOCK), 0.0, NEG)
        hist = BLOCK * d
        nsb = DIL_SPAN // hist
        kf_ref[0:hist, :] = kp_ref[...]
        kf_ref[hist:hist + DIL_SPAN, :] = kc_ref[...]
        vf_ref[0:hist, :] = vp_ref[...]
        vf_ref[hist:hist + DIL_SPAN, :] = vc_ref[...]

        def block(idx, g=g, d=d, hist=hist, nsb=nsb, q_ref=q_ref, bias_std=bias_std, bias_first=bias_first):
            r = idx // nsb
            sb = idx % nsb
            start = r + hist * sb
            q = q_ref[pl.ds(start, BLOCK, stride=d), :].astype(BF16)
            k = kf_ref[pl.ds(start, 2 * BLOCK, stride=d), :].astype(BF16)
            v = vf_ref[pl.ds(start, 2 * BLOCK, stride=d), :].astype(BF16)
            s = lax.dot_general(q, k, (((1,), (1,)), ((), ())), preferred_element_type=F32) * scale
            first = jnp.logical_and(b == 0, sb == 0)
            s = s + jnp.where(first, bias_first, bias_std)
            m = jnp.max(s, axis=-1, keepdims=True)
            p = jnp.exp(s - m)
            l = jnp.sum(p, axis=-1, keepdims=True)
            o = jnp.dot(p.astype(BF16), v, preferred_element_type=F32)
            os_ref[g, pl.ds(start, BLOCK, stride=d), :] = o
            ms_ref[g, pl.ds(start, BLOCK, stride=d), :] = jnp.broadcast_to(m, (BLOCK, LANES))
            ls_ref[g, pl.ds(start, BLOCK, stride=d), :] = jnp.broadcast_to(l, (BLOCK, LANES))

        def body(i, _, block=block):
            for u in range(DIL_UNROLL):
                block(i * DIL_UNROLL + u)
            return 0

        assert (d * nsb) % DIL_UNROLL == 0
        lax.fori_loop(0, d * nsb // DIL_UNROLL, body, 0)

    mx = ms_ref[0]
    for g in range(1, ng):
        mx = jnp.maximum(mx, ms_ref[g])
    num = jnp.zeros((DIL_SPAN, LANES), F32)
    den = jnp.zeros((DIL_SPAN, LANES), F32)
    for g in range(ng):
        e = jnp.exp(ms_ref[g] - mx)
        num = num + e * os_ref[g]
        den = den + e * ls_ref[g]
    o_ref[...] = (num / den).astype(o_ref.dtype)


def _dilated(qkv):
    T = qkv.shape[1]
    assert T % DIL_SPAN == 0 and HEAD_DIM == LANES
    H = DIL_HEADS_PER_GROUP
    in_specs, args = [], []
    max_hist = 0
    for g, (window, d) in enumerate(DIL_PATTERNS):
        assert window // d <= BLOCK and DIL_SPAN % (BLOCK * d) == 0
        hist = BLOCK * d
        max_hist = max(max_hist, hist)
        ratio = DIL_SPAN // hist

        def cur(j, b, base):
            return (base + j, b, 0)

        def prev(j, b, base, ratio=ratio):
            return (base + j, jnp.maximum(b * ratio - 1, 0), 0)

        qh, kh, vh = g * H, DIL_HEADS + g * H, 2 * DIL_HEADS + g * H
        in_specs += [
            pl.BlockSpec((None, DIL_SPAN, LANES), functools.partial(cur, base=qh)),
            pl.BlockSpec((None, DIL_SPAN, LANES), functools.partial(cur, base=kh)),
            pl.BlockSpec((None, hist, LANES), functools.partial(prev, base=kh)),
            pl.BlockSpec((None, DIL_SPAN, LANES), functools.partial(cur, base=vh)),
            pl.BlockSpec((None, hist, LANES), functools.partial(prev, base=vh)),
        ]
        args += [qkv] * 5
    ng = len(DIL_PATTERNS)
    return pl.pallas_call(
        _dil_kernel,
        out_shape=jax.ShapeDtypeStruct((T, DIL_OUT), BF16),
        grid=(H, T // DIL_SPAN),
        in_specs=in_specs,
        out_specs=pl.BlockSpec((DIL_SPAN, LANES), lambda j, b: (b, j)),
        scratch_shapes=[
            pltpu.VMEM((max_hist + DIL_SPAN, LANES), F32),
            pltpu.VMEM((max_hist + DIL_SPAN, LANES), F32),
            pltpu.VMEM((ng, DIL_SPAN, LANES), F32),
            pltpu.VMEM((ng, DIL_SPAN, LANES), F32),
            pltpu.VMEM((ng, DIL_SPAN, LANES), F32),
        ],
        compiler_params=_cparams(("arbitrary", "arbitrary")),
        name="dilated",
    )(*args)


def _out_ln_kernel(*refs, n_in, slabs):
    a_refs = refs[:n_in]
    w_refs = refs[n_in:2 * n_in]
    x_ref, g_ref, b_ref, o_ref, o2_ref = refs[2 * n_in:]
    acc = ALPHA * x_ref[...]
    for a_ref, w_ref in zip(a_refs, w_refs):
        acc = acc + jnp.dot(a_ref[...], w_ref[...], preferred_element_type=F32)
    y = _layer_norm(acc, g_ref[...], b_ref[...])
    o_ref[...] = y
    if slabs:
        tm, D = y.shape
        S = D // LANES
        for s in range(S):
            o2_ref[pl.ds(s, tm, stride=S), :] = y[:, s * LANES:(s + 1) * LANES]
    else:
        o2_ref[...] = y.astype(BF16)


def _out_ln(a_list, wb, x, g, b, *, slabs=False, tm=512):
    T, D = x.shape
    tm = min(tm, T)
    n_in = len(a_list)
    kk = a_list[0].shape[1]
    assert all(a.shape[1] == kk for a in a_list) and wb.shape[0] == n_in * kk and T % tm == 0
    w3 = wb.reshape(n_in, kk, D)
    in_specs = [pl.BlockSpec((tm, kk), lambda i: (i, 0)) for _ in range(n_in)]
    in_specs += [pl.BlockSpec((None, kk, D), functools.partial(lambda i, c: (c, 0, 0), c=c)) for c in range(n_in)]
    in_specs += [pl.BlockSpec((tm, D), lambda i: (i, 0)),
                 pl.BlockSpec((1, D), lambda i: (0, 0)),
                 pl.BlockSpec((1, D), lambda i: (0, 0))]
    S = D // LANES
    if slabs:
        out2 = jax.ShapeDtypeStruct((T * S, LANES), F32)
        spec2 = pl.BlockSpec((tm * S, LANES), lambda i: (i, 0))
    else:
        out2 = jax.ShapeDtypeStruct((T, D), BF16)
        spec2 = pl.BlockSpec((tm, D), lambda i: (i, 0))
    return pl.pallas_call(
        functools.partial(_out_ln_kernel, n_in=n_in, slabs=slabs),
        out_shape=(jax.ShapeDtypeStruct((T, D), F32), out2),
        grid=(T // tm,),
        in_specs=in_specs,
        out_specs=(pl.BlockSpec((tm, D), lambda i: (i, 0)), spec2),
        compiler_params=_cparams(("arbitrary",)),
        name="out_ln",
    )(*a_list, *([w3] * n_in), x, g.reshape(1, D), b.reshape(1, D))


def _res_ln_kernel(x_ref, y_ref, g_ref, b_ref, o_ref, ob_ref):
    y = _layer_norm(ALPHA * x_ref[...] + y_ref[...], g_ref[...], b_ref[...])
    o_ref[...] = y
    ob_ref[...] = y.astype(BF16)


def _res_ln(x, y, g, b, tm=512):
    T, D = x.shape
    tm = min(tm, T)
    row = pl.BlockSpec((tm, D), lambda i: (i, 0))
    vec = pl.BlockSpec((1, D), lambda i: (0, 0))
    return pl.pallas_call(
        _res_ln_kernel,
        out_shape=(jax.ShapeDtypeStruct((T, D), F32), jax.ShapeDtypeStruct((T, D), BF16)),
        grid=(T // tm,),
        in_specs=[row, row, vec, vec],
        out_specs=(row, row),
        compiler_params=_cparams(("arbitrary",)),
        name="res_ln",
    )(x, y, g.reshape(1, D), b.reshape(1, D))


FFN_TM = 1024
FFN_TM_MOE = 2048
FFN_TF = 512
FFN_CHUNK = 256


def _ffn_kernel(vt_ref, ve_ref, vlo_ref, vhi_ref, vok_ref, x_ref, wg_ref, wu_ref, wd_ref, o_ref):
    v = pl.program_id(0)
    f = pl.program_id(1)
    lo = vlo_ref[v]
    hi = vhi_ref[v]

    @pl.when(jnp.logical_and(vok_ref[v] == 2, f == 0))
    def _():
        o_ref[...] = jnp.zeros(o_ref.shape, o_ref.dtype)

    def rows(c):
        return pl.ds(pl.multiple_of(c * FFN_CHUNK, FFN_CHUNK), FFN_CHUNK)

    @pl.when(f == 0)
    def _():
        def zero(c, _):
            o_ref[rows(c), :] = jnp.zeros((FFN_CHUNK, o_ref.shape[1]), o_ref.dtype)
            return 0
        lax.fori_loop(lo, hi, zero, 0)

    def swiglu(c, nchunks):
        r = pl.ds(pl.multiple_of(c * FFN_CHUNK, FFN_CHUNK), nchunks * FFN_CHUNK)
        x = x_ref[r, :]
        g = jnp.dot(x, wg_ref[...].astype(BF16), preferred_element_type=F32)
        u = jnp.dot(x, wu_ref[...].astype(BF16), preferred_element_type=F32)
        h = (g * jax.nn.sigmoid(g) * u).astype(BF16)
        o_ref[r, :] += jnp.dot(h, wd_ref[...].astype(BF16), preferred_element_type=F32)

    def pair(i, _):
        swiglu(lo + 2 * i, 2)
        return 0

    n = hi - lo
    lax.fori_loop(0, n // 2, pair, 0)

    @pl.when(n % 2 == 1)
    def _():
        swiglu(hi - 1, 1)


def _ffn(visits, xs, wg, wu, wd, *, tm):
    vt, ve, vlo, vhi, vok = visits
    R, D = xs.shape
    F = wg.shape[2]
    assert R % tm == 0 and F % FFN_TF == 0 and tm % FFN_CHUNK == 0
    nf = F // FFN_TF
    V = vt.shape[0]
    row_mode = {} if tm <= FFN_TM else dict(pipeline_mode=pl.Buffered(1))

    def fidx(v, f, vok):
        return jnp.where(vok[v] == 1, f, nf - 1)

    return pl.pallas_call(
        _ffn_kernel,
        out_shape=jax.ShapeDtypeStruct((R, D), F32),
        grid_spec=pltpu.PrefetchScalarGridSpec(
            num_scalar_prefetch=5,
            grid=(V, nf),
            in_specs=[
                pl.BlockSpec((tm, D), lambda v, f, vt, ve, vlo, vhi, vok: (vt[v], 0), **row_mode),
                pl.BlockSpec((None, D, FFN_TF), lambda v, f, vt, ve, vlo, vhi, vok: (ve[v], 0, fidx(v, f, vok))),
                pl.BlockSpec((None, D, FFN_TF), lambda v, f, vt, ve, vlo, vhi, vok: (ve[v], 0, fidx(v, f, vok))),
                pl.BlockSpec((None, FFN_TF, D), lambda v, f, vt, ve, vlo, vhi, vok: (ve[v], fidx(v, f, vok), 0)),
            ],
            out_specs=pl.BlockSpec((tm, D), lambda v, f, vt, ve, vlo, vhi, vok: (vt[v], 0), **row_mode),
        ),
        compiler_params=_cparams(("arbitrary", "arbitrary")),
        name="ffn",
    )(vt, ve, vlo, vhi, vok, xs, wg, wu, wd)


def _dense_visits(T, layer, tm):
    nt = T // tm
    cpt = tm // FFN_CHUNK
    z = jnp.zeros((nt,), I32)
    return (jnp.arange(nt, dtype=I32), z + layer, z, z + cpt, z + 1)


def _router_kernel(x_ref, r_ref, o_ref):
    logits = jnp.dot(x_ref[...], r_ref[...], preferred_element_type=F32, precision=lax.Precision.HIGHEST)
    lane = lax.broadcasted_iota(I32, logits.shape, 1)
    logits = jnp.where(lane < N_EXPERTS, logits, -jnp.inf)
    v1 = jnp.max(logits, axis=-1, keepdims=True)
    i1 = jnp.min(jnp.where(logits == v1, lane, LANES), axis=-1, keepdims=True)
    rest = jnp.where(lane == i1, -jnp.inf, logits)
    v2 = jnp.max(rest, axis=-1, keepdims=True)
    i2 = jnp.min(jnp.where(rest == v2, lane, LANES), axis=-1, keepdims=True)
    e2 = jnp.exp(v2 - v1)
    g1 = 1.0 / (1.0 + e2)
    g2 = e2 / (1.0 + e2)
    out = jnp.where(lane == 0, i1.astype(F32), 0.0)
    out = jnp.where(lane == 1, i2.astype(F32), out)
    out = jnp.where(lane == 2, g1, out)
    out = jnp.where(lane == 3, g2, out)
    o_ref[...] = out


def _router(x, router, tm=512):
    T, D = x.shape
    tm = min(tm, T)
    rp = jnp.pad(router, ((0, 0), (0, LANES - N_EXPERTS)))
    return pl.pallas_call(
        _router_kernel,
        out_shape=jax.ShapeDtypeStruct((T, LANES), F32),
        grid=(T // tm,),
        in_specs=[pl.BlockSpec((tm, D), lambda i: (i, 0)), pl.BlockSpec((D, LANES), lambda i: (0, 0))],
        out_specs=pl.BlockSpec((tm, LANES), lambda i: (i, 0)),
        compiler_params=_cparams(("arbitrary",)),
        name="router",
    )(x, rp)


GATHER_ROWS = 256


def _row_copy(src_hbm, row, dst, r, sem):
    return pltpu.make_async_copy(src_hbm.at[pl.ds(row, 1), :], dst.at[pl.ds(r, 1), :], sem)


def _prefetch_rows(issue, slot):
    i = pl.program_id(0)

    @pl.when(i == 0)
    def _():
        issue(0, 0)

    @pl.when(i + 1 < pl.num_programs(0))
    def _():
        issue(i + 1, 1 - slot)


def _gather_kernel(tok_ref, x_hbm, o_ref, buf, sem, *, S):
    slot = pl.program_id(0) % 2

    def issue(step, s):
        def body(r, _):
            src = pl.ds(pl.multiple_of(tok_ref[step * GATHER_ROWS + r] * S, S), S)
            dst = pl.ds(pl.multiple_of(r * S, S), S)
            pltpu.make_async_copy(x_hbm.at[src, :], buf.at[s, dst, :], sem.at[s]).start()
            return 0
        lax.fori_loop(0, GATHER_ROWS, body, 0)

    _prefetch_rows(issue, slot)
    pltpu.make_async_copy(x_hbm.at[pl.ds(0, GATHER_ROWS * S), :], buf.at[slot], sem.at[slot]).wait()
    for s in range(S):
        o_ref[:, s * LANES:(s + 1) * LANES] = buf[slot, pl.ds(s, GATHER_ROWS, stride=S), :].astype(o_ref.dtype)


def _gather_rows(xslab, tok, D):
    R = tok.shape[0]
    S = D // LANES
    assert R % GATHER_ROWS == 0 and xslab.shape[0] >= GATHER_ROWS * S and S % 8 == 0
    return pl.pallas_call(
        functools.partial(_gather_kernel, S=S),
        out_shape=jax.ShapeDtypeStruct((R, D), BF16),
        grid_spec=pltpu.PrefetchScalarGridSpec(
            num_scalar_prefetch=1,
            grid=(R // GATHER_ROWS,),
            in_specs=[pl.BlockSpec(memory_space=pl.ANY)],
            out_specs=pl.BlockSpec((GATHER_ROWS, D), lambda i, tok: (i, 0)),
            scratch_shapes=[pltpu.VMEM((2, GATHER_ROWS * S, LANES), F32), pltpu.SemaphoreType.DMA((2,))],
        ),
        compiler_params=_cparams(("arbitrary",)),
        name="gather_rows",
    )(tok, xslab)


def _combine_kernel(p0_ref, p1_ref, y_hbm, r_ref, x_ref, g_ref, b_ref, o_ref, ob_ref, buf, sem):
    slot = pl.program_id(0) % 2

    def issue(step, s):
        def body(r, _):
            t = step * GATHER_ROWS + r
            _row_copy(y_hbm, p0_ref[t], buf.at[s, 0], r, sem.at[s]).start()
            _row_copy(y_hbm, p1_ref[t], buf.at[s, 1], r, sem.at[s]).start()
            return 0
        lax.fori_loop(0, GATHER_ROWS, body, 0)

    _prefetch_rows(issue, slot)
    for k in range(TOP_K):
        pltpu.make_async_copy(y_hbm.at[pl.ds(0, GATHER_ROWS), :], buf.at[slot, k], sem.at[slot]).wait()
    route = r_ref[...]
    g0 = route[:, TOP_K:TOP_K + 1]
    g1 = route[:, TOP_K + 1:TOP_K + 2]
    z = ALPHA * x_ref[...] + (g0 * buf[slot, 0] + g1 * buf[slot, 1])
    y = _layer_norm(z, g_ref[...], b_ref[...])
    o_ref[...] = y
    ob_ref[...] = y.astype(BF16)


def _combine_ln(y, p0, p1, route, x, g, b):
    T, D = x.shape
    assert T % GATHER_ROWS == 0 and y.shape[0] >= GATHER_ROWS
    row = pl.BlockSpec((GATHER_ROWS, D), lambda i, p0, p1: (i, 0))
    vec = pl.BlockSpec((1, D), lambda i, p0, p1: (0, 0))
    rspec = pl.BlockSpec((GATHER_ROWS, LANES), lambda i, p0, p1: (i, 0))
    return pl.pallas_call(
        _combine_kernel,
        out_shape=(jax.ShapeDtypeStruct((T, D), F32), jax.ShapeDtypeStruct((T, D), BF16)),
        grid_spec=pltpu.PrefetchScalarGridSpec(
            num_scalar_prefetch=2,
            grid=(T // GATHER_ROWS,),
            in_specs=[pl.BlockSpec(memory_space=pl.ANY), rspec, row, vec, vec],
            out_specs=(row, row),
            scratch_shapes=[pltpu.VMEM((2, TOP_K, GATHER_ROWS, D), F32), pltpu.SemaphoreType.DMA((2,))],
        ),
        compiler_params=_cparams(("arbitrary",)),
        name="combine_ln",
    )(p0, p1, y, route, x, g.reshape(1, D), b.reshape(1, D))


def _moe_plan(route, T, layer, tm):
    E = N_EXPERTS
    e = route[:, :TOP_K].astype(I32)
    ef = e.reshape(-1)
    onehot = (ef[:, None] == jnp.arange(E, dtype=I32)[None, :]).astype(I32)
    rank = jnp.cumsum(onehot, axis=0) - onehot
    counts = jnp.sum(onehot, axis=0)
    chunks = (counts + FFN_CHUNK - 1) // FFN_CHUNK
    seg_end = jnp.cumsum(chunks)
    seg_start = seg_end - chunks
    pos = seg_start[ef] * FFN_CHUNK + jnp.sum(rank * onehot, axis=1)
    R = -(-(TOP_K * T + E * FFN_CHUNK) // tm) * tm
    cpt = tm // FFN_CHUNK
    nt = R // tm
    n_slots = nt + E
    slot = jnp.arange(n_slots, dtype=I32)

    def place(vals, dest):
        return jnp.sum(jnp.where(dest[None, :] == slot[:, None], vals[None, :], 0), axis=1).astype(I32)

    tile_pts = jnp.arange(nt, dtype=I32) * cpt
    seg_pts = seg_start.astype(I32)
    tile_rank = jnp.arange(nt, dtype=I32) + jnp.sum(seg_pts[None, :] < tile_pts[:, None], axis=1)
    seg_rank = jnp.arange(E, dtype=I32) + jnp.sum(tile_pts[None, :] <= seg_pts[:, None], axis=1)
    pts = place(jnp.concatenate([tile_pts, seg_pts]), jnp.concatenate([tile_rank, seg_rank]))
    used = seg_end[-1]
    used_tiles_end = (used + cpt - 1) // cpt * cpt
    nxt = jnp.concatenate([pts[1:], jnp.array([nt * cpt], I32)])
    ok = (pts < used) & (nxt > pts)
    nxt = jnp.where(nxt >= used, jnp.maximum(used_tiles_end, pts), nxt)
    tile = pts // cpt
    nxt = jnp.minimum(nxt, (tile + 1) * cpt)
    exp = jnp.minimum(jnp.sum(seg_end[None, :] <= pts[:, None], axis=1), E - 1).astype(I32)
    last = jnp.max(jnp.where(ok, slot, 0))
    vt = jnp.where(ok, tile, jnp.sum(jnp.where(slot == last, tile, 0)))
    ve = jnp.where(ok, exp, jnp.sum(jnp.where(slot == last, exp, 0)))
    vlo = jnp.where(ok, pts - tile * cpt, 0)
    vhi = jnp.where(ok, nxt - tile * cpt, 0)
    oki = ok.astype(I32)
    dest = jnp.where(ok, jnp.cumsum(oki) - oki, jnp.sum(oki) + jnp.cumsum(1 - oki) - (1 - oki))
    vt, ve, vlo, vhi, vok = (place(a, dest) for a in (vt, ve, vlo, vhi, oki))
    used_tiles = used_tiles_end // cpt
    fill = slot - (n_slots - (nt - used_tiles))
    vt = jnp.where(fill >= 0, used_tiles + fill, vt).astype(I32)
    vok = jnp.where(fill >= 0, 2, vok).astype(I32)
    visits = (vt, (ve + layer * E).astype(I32), vlo, vhi, vok)
    p = pos.reshape(T, TOP_K).astype(I32)
    tok = jnp.zeros((R,), I32).at[pos].set(jnp.arange(TOP_K * T, dtype=I32) // TOP_K)
    return tok, visits, p[:, 0], p[:, 1]


DSA_TK = 512
NT_DIMS = (((1,), (1,)), ((), ()))
DSA_Q_SCALE = (HEAD_DIM ** -0.5) * LOG2_E


def _dsa_kernel(q_ref, iq_ref, iw_ref, ike_ref, iko_ref, kt_ref, v_ref, o_ref,
                key_s, bias_s, s_s, p_s, m_s, l_s, al_s, acc_s, *, ksel, idx_bits):
    n = pl.program_id(0)
    TK = DSA_TK
    nkt = (n * BLOCK + BLOCK + TK - 1) // TK
    qpos = n * BLOCK + lax.broadcasted_iota(I32, (BLOCK, 1), 0)
    lane_k = lax.broadcasted_iota(I32, (1, TK), 1)

    def keys(j):
        return pl.ds(pl.multiple_of(j * TK, TK), TK)

    iqb = iq_ref[...]
    npair = IDX_HEADS // 2
    lhs = jnp.concatenate([iqb[:, p * LANES:(p + 1) * LANES] for p in range(npair)], axis=0)
    wts = iw_ref[...] * ((IDX_DIM ** -0.5) * (IDX_HEADS ** -0.5))

    def score_tile(j, _):
        le = lax.dot_general(lhs, ike_ref[keys(j), :], NT_DIMS, preferred_element_type=F32)
        lo = lax.dot_general(lhs, iko_ref[keys(j), :], NT_DIMS, preferred_element_type=F32)
        sc = jnp.zeros((BLOCK, TK), F32)
        for p in range(npair):
            sc = sc + jnp.maximum(le[p * BLOCK:(p + 1) * BLOCK], 0.0) * wts[:, 2 * p:2 * p + 1]
            sc = sc + jnp.maximum(lo[p * BLOCK:(p + 1) * BLOCK], 0.0) * wts[:, 2 * p + 1:2 * p + 2]
        bits = pltpu.bitcast(sc, I32)
        key = bits ^ ((bits >> 31) & 0x7FFFFFFF)
        key_s[j] = jnp.where(j * TK + lane_k <= qpos, key, INT_MIN)
        return 0

    lax.fori_loop(0, nkt, score_tile, 0)

    def count(pred):
        def body(j, acc):
            hit = jnp.where(pred(key_s[j], j * TK + lane_k), 1.0, 0.0)
            for c in range(TK // LANES):
                acc = acc + hit[:, c * LANES:(c + 1) * LANES]
            return acc
        acc = lax.fori_loop(0, nkt, body, jnp.zeros((BLOCK, LANES), F32))
        return jnp.sum(acc, axis=1, keepdims=True)

    searching = (n + 1) * BLOCK > ksel

    def write_bias(thr, pidx):
        def body(j, _):
            key = key_s[j]
            sel = (key > thr) | ((key == thr) & (j * TK + lane_k <= pidx))
            bias_s[j] = jnp.where(sel, 0.0, NEG)
            return 0
        lax.fori_loop(0, nkt, body, 0)

    @pl.when(jnp.logical_not(searching))
    def _():
        write_bias(jnp.full((BLOCK, 1), INT_MIN, I32), jnp.full((BLOCK, 1), -1, I32))

    @pl.when(searching)
    def _():
        def bit_step(carry):
            i, thr, cnt_thr, _ = carry
            cand = thr + jnp.left_shift(jnp.int32(1), 31 - i)
            cnt = count(lambda key, kidx: key >= cand)
            ok = cnt >= ksel
            thr = jnp.where(ok, cand, thr)
            cnt_thr = jnp.where(ok, cnt, cnt_thr)
            return i + 1, thr, cnt_thr, jnp.max(jnp.abs(cnt_thr - ksel))

        thr0 = jnp.full((BLOCK, 1), INT_MIN, I32)
        _, thr, _, miss = lax.while_loop(
            lambda c: jnp.logical_and(c[0] < 32, c[3] > 0.0), bit_step,
            (jnp.int32(0), thr0, jnp.full((BLOCK, 1), -1.0, F32), jnp.float32(1.0)))
        exact = miss == 0.0

        @pl.when(exact)
        def _():
            write_bias(thr, jnp.full((BLOCK, 1), 2 ** 30, I32))

        @pl.when(jnp.logical_not(exact))
        def _():
            need = ksel - count(lambda key, kidx: key > thr)

            def idx_step(i, pidx):
                cand = pidx + jnp.left_shift(jnp.int32(1), idx_bits - 1 - i)
                c = count(lambda key, kidx: (key == thr) & (kidx < cand))
                return jnp.where(c < need, cand, pidx)

            pidx = lax.fori_loop(0, idx_bits, idx_step, jnp.zeros((BLOCK, 1), I32))
            write_bias(thr, pidx)

    qb = q_ref[...]
    qs = jnp.concatenate([qb[:, h * LANES:(h + 1) * LANES] for h in range(DSA_HEADS)], axis=0)
    ncol = TK // LANES
    m_s[...] = jnp.full(m_s.shape, NEG, F32)
    l_s[...] = jnp.zeros(l_s.shape, F32)
    acc_s[...] = jnp.zeros(acc_s.shape, F32)

    def attend(j, _):
        s_s[...] = lax.dot_general(qs, kt_ref[keys(j), :], NT_DIMS, preferred_element_type=F32)
        for h in range(DSA_HEADS):
            r = slice(h * BLOCK, (h + 1) * BLOCK)
            t = [s_s[r, c * LANES:(c + 1) * LANES] + bias_s[j, :, c * LANES:(c + 1) * LANES] for c in range(ncol)]
            mx = t[0]
            for c in range(1, ncol):
                mx = jnp.maximum(mx, t[c])
            m_old = m_s[r, :]
            m_new = jnp.maximum(m_old, jnp.max(mx, axis=-1, keepdims=True))
            alpha = jnp.exp2(m_old - m_new)
            psum = jnp.zeros((BLOCK, LANES), F32)
            for c in range(ncol):
                p = jnp.exp2(t[c] - m_new)
                psum = psum + p
                p_s[r, c * LANES:(c + 1) * LANES] = p.astype(BF16)
            m_s[r, :] = m_new
            l_s[r, :] = alpha * l_s[r, :] + jnp.sum(psum, axis=-1, keepdims=True)
            al_s[r, :] = alpha
        acc_s[...] = al_s[...] * acc_s[...] + jnp.dot(p_s[...], v_ref[keys(j), :], preferred_element_type=F32)
        return 0

    lax.fori_loop(0, nkt, attend, 0)
    for h in range(DSA_HEADS):
        r = slice(h * BLOCK, (h + 1) * BLOCK)
        o_ref[:, h * LANES:(h + 1) * LANES] = (acc_s[r, :] / l_s[r, :]).astype(o_ref.dtype)


DSA_SIDE_IQ = IDX_HEADS * IDX_DIM
DSA_SIDE_COLS = DSA_SIDE_IQ + 4 * LANES


def _dsa_side_weights(w_in_l):
    o = DSA_Q
    wk = w_in_l[:, o:o + HEAD_DIM]; o += HEAD_DIM
    wv = w_in_l[:, o:o + HEAD_DIM]; o += HEAD_DIM
    wiq = w_in_l[:, o:o + DSA_SIDE_IQ]; o += DSA_SIDE_IQ
    wik = w_in_l[:, o:o + IDX_DIM]; o += IDX_DIM
    wiw = w_in_l[:, o:o + IDX_HEADS]
    z = jnp.zeros_like(wik)
    side = jnp.concatenate([wiq, wk, wv, wik, z, z, wik], axis=1)
    return side, jnp.pad(wiw, ((0, 0), (0, LANES - IDX_HEADS)))


def _dsa(q, side, iw):
    T = q.shape[0]
    TK = DSA_TK
    assert T % TK == 0 and TK % BLOCK == 0 and HEAD_DIM == LANES and 2 * IDX_DIM == LANES
    assert DSA_SIDE_IQ % LANES == 0
    ksel = min(DSA_TOPK, T // 4)
    assert ksel % BLOCK == 0
    nt = T // TK
    rows = DSA_HEADS * BLOCK
    g0 = DSA_SIDE_IQ // LANES

    def key_side(group):
        return pl.BlockSpec((T, LANES), lambda n: (0, group))

    return pl.pallas_call(
        functools.partial(_dsa_kernel, ksel=ksel, idx_bits=max(1, (T - 1).bit_length())),
        out_shape=jax.ShapeDtypeStruct((T, DSA_Q), BF16),
        grid=(T // BLOCK,),
        in_specs=[
            pl.BlockSpec((BLOCK, DSA_Q), lambda n: (n, 0)),
            pl.BlockSpec((BLOCK, DSA_SIDE_IQ), lambda n: (n, 0)),
            pl.BlockSpec((BLOCK, LANES), lambda n: (n, 0)),
            key_side(g0 + 2), key_side(g0 + 3), key_side(g0), key_side(g0 + 1),
        ],
        out_specs=pl.BlockSpec((BLOCK, DSA_Q), lambda n: (n, 0)),
        scratch_shapes=[
            pltpu.VMEM((nt, BLOCK, TK), I32),
            pltpu.VMEM((nt, BLOCK, TK), F32),
            pltpu.VMEM((rows, TK), F32),
            pltpu.VMEM((rows, TK), BF16),
            pltpu.VMEM((rows, LANES), F32),
            pltpu.VMEM((rows, LANES), F32),
            pltpu.VMEM((rows, LANES), F32),
            pltpu.VMEM((rows, HEAD_DIM), F32),
        ],
        compiler_params=_cparams(("arbitrary",)),
        name="dsa",
    )(q, side, iw, side, side, side, side)


def _even_layer(x, xb, rope, i, w_in, pool_w, pool_scale, w_out, ln1_g, ln1_b, wg, wu, wd, ln2_g, ln2_b):
    T = x.shape[0]
    cos2, sin2 = rope
    u = _proj(xb, w_in, i, cos2, sin2, col0=0, ncols=POOL_DIM, tn=1024, rope=(0, 0), out_dtype=F32)
    qkv = _proj(xb, w_in, i, cos2, sin2, col0=POOL_DIM, ncols=3 * DIL_QKV, tn=1024, rope=(0, 2 * DIL_QKV),
                out_dtype=F32, head_major=True)
    pooled = _pool(u, pool_w, i, pool_scale[i])
    attn = _dilated(qkv)
    x1, x1b = _out_ln([pooled, attn], w_out[i].astype(BF16), x, ln1_g[i], ln1_b[i])
    tm = min(FFN_TM, T)
    y = _ffn(_dense_visits(T, i, tm), x1b, wg, wu, wd, tm=tm)
    return _res_ln(x1, y, ln2_g[i], ln2_b[i])


def _odd_layer(x, xb, rope, i, w_in, w_out, ln1_g, ln1_b, router, wg, wu, wd, ln2_g, ln2_b):
    T, D = x.shape
    cos2, sin2 = rope
    q = _proj(xb, w_in, i, cos2, sin2, col0=0, ncols=DSA_Q, tn=1024, rope=(0, DSA_Q), out_dtype=BF16,
              out_scale=DSA_Q_SCALE)
    w_side, w_iw = _dsa_side_weights(w_in[i])
    side = _proj(xb, w_side[None], 0, cos2, sin2, col0=0, ncols=DSA_SIDE_COLS, tn=DSA_SIDE_COLS,
                 rope=(DSA_SIDE_IQ, DSA_SIDE_IQ + HEAD_DIM), out_dtype=BF16)
    iw = _proj(xb, w_iw[None], 0, cos2, sin2, col0=0, ncols=LANES, tn=LANES, rope=(0, 0), out_dtype=F32)
    attn = _dsa(q, side, iw)
    x1, x1_slabs = _out_ln([attn], w_out[i].astype(BF16), x, ln1_g[i], ln1_b[i], slabs=True)
    route = _router(x1, router[i])
    tok, visits, p0, p1 = _moe_plan(route, T, i, FFN_TM_MOE)
    xs = _gather_rows(x1_slabs, tok, D)
    F = wg.shape[3]
    y = _ffn(visits, xs, wg.reshape(-1, D, F), wu.reshape(-1, D, F), wd.reshape(-1, F, D), tm=FFN_TM_MOE)
    return _combine_ln(y, p0, p1, route, x1, ln2_g[i], ln2_b[i])


def kernel(x, ev_w_in, ev_pool_w, ev_pool_scale, ev_w_out, ev_ln1_g, ev_ln1_b, ev_ffn_w_gate, ev_ffn_w_up,
           ev_ffn_w_down, ev_ln2_g, ev_ln2_b, od_w_in, od_w_out, od_ln1_g, od_ln1_b, od_router,
           od_exp_w_gate, od_exp_w_up, od_exp_w_down, od_ln2_g, od_ln2_b):
    B, T, D = x.shape
    rope = _rope_tables(T)
    outs = []
    for bi in range(B):
        h = x[bi]
        hb = h.astype(BF16)
        for layer in range(DEPTH):
            i = layer // 2
            if layer % 2 == 0:
                h, hb = _even_layer(h, hb, rope, i, ev_w_in, ev_pool_w, ev_pool_scale, ev_w_out,
                                    ev_ln1_g, ev_ln1_b, ev_ffn_w_gate, ev_ffn_w_up,
                                    ev_ffn_w_down, ev_ln2_g, ev_ln2_b)
            else:
                h, hb = _odd_layer(h, hb, rope, i, od_w_in, od_w_out, od_ln1_g, od_ln1_b,
                                   od_router, od_exp_w_gate, od_exp_w_up, od_exp_w_down,
                                   od_ln2_g, od_ln2_b)
        outs.append(h)
    return jnp.stack(outs, axis=0)
```

```python
import functools

import jax
import jax.numpy as jnp
from jax import lax
from jax.experimental import pallas as pl
from jax.experimental.pallas import tpu as pltpu

F32 = jnp.float32
BF16 = jnp.bfloat16
I32 = jnp.int32

D_MODEL = 2048
DEPTH = 4
HEAD_DIM = 128
ROPE_THETA = 10000.0
LN_EPS = 1e-5
BLOCK = 128
ALPHA = (2 * DEPTH) ** 0.25
POOL_WINDOWS = (2, 4, 8, 16)
POOL_GROUP_DIM = D_MODEL // 8
POOL_DIM = len(POOL_WINDOWS) * POOL_GROUP_DIM
DIL_PATTERNS = ((128, 1), (512, 4), (2048, 16))
DIL_HEADS_PER_GROUP = D_MODEL // 256
DIL_HEADS = len(DIL_PATTERNS) * DIL_HEADS_PER_GROUP
DIL_QKV = DIL_HEADS * HEAD_DIM
DIL_OUT = DIL_HEADS_PER_GROUP * HEAD_DIM
DSA_HEADS = D_MODEL // HEAD_DIM
IDX_HEADS = 16
IDX_DIM = 64
DSA_TOPK = 256
DSA_Q = DSA_HEADS * HEAD_DIM
D_FF = 5632
N_EXPERTS = 8
TOP_K = 2

LANES = 128
VMEM_LIMIT = 56 * 1024 * 1024

NEG = -1e30
LOG2_E = 1.4426950408889634
INT_MIN = -(2 ** 31)


def _cparams(sem):
    return pltpu.CompilerParams(dimension_semantics=sem, vmem_limit_bytes=VMEM_LIMIT)


def _layer_norm(z, g, b):
    mu = jnp.mean(z, axis=-1, keepdims=True)
    zc = z - mu
    var = jnp.mean(zc * zc, axis=-1, keepdims=True)
    return zc * lax.rsqrt(var + LN_EPS) * g + b


PROJ_HEADS_PER_DOT = 2


def _proj_kernel(x_ref, w_ref, cos_ref, sin_ref, o_ref, wb_ref, *, rope, n_tiles, head_major, out_scale):
    @pl.when(pl.program_id(1) == 0)
    def _():
        wb_ref[...] = w_ref[...].astype(BF16)

    tn = wb_ref.shape[1]
    nh = tn // LANES
    if n_tiles == 1:
        lo, hi = rope[0] // LANES, rope[1] // LANES
        cos, sin = cos_ref[...], sin_ref[...]
    else:
        j = pl.program_id(0)
        rot = jnp.logical_and(j >= rope[0] // tn, j < rope[1] // tn)
        lo, hi = (0, nh) if rope[1] > rope[0] else (0, 0)
        cos = jnp.where(rot, cos_ref[...], 1.0)
        sin = jnp.where(rot, sin_ref[...], 0.0)

    x = x_ref[...]
    step = min(nh, PROJ_HEADS_PER_DOT)
    for c0 in range(0, nh, step):
        n = min(step, nh - c0)
        acc = jnp.dot(x, wb_ref[:, c0 * LANES:(c0 + n) * LANES], preferred_element_type=F32)
        for c in range(c0, c0 + n):
            a = acc[:, (c - c0) * LANES:(c - c0 + 1) * LANES]
            if lo <= c < hi:
                a = a * cos + pltpu.roll(a, HEAD_DIM // 2, 1) * sin
            if out_scale != 1.0:
                a = a * out_scale
            if head_major:
                o_ref[c] = a.astype(o_ref.dtype)
            else:
                o_ref[:, c * LANES:(c + 1) * LANES] = a.astype(o_ref.dtype)


def _proj(xb, w, layer, cos2, sin2, *, col0, ncols, tn, rope, out_dtype, head_major=False, out_scale=1.0,
          tm=1024):
    T, K = xb.shape
    tm = min(tm, T)
    assert T % tm == 0 and ncols % tn == 0 and col0 % tn == 0 and tn % LANES == 0
    assert all(r % (LANES if ncols == tn else tn) == 0 for r in rope)
    n_tiles = ncols // tn
    c0 = col0 // tn
    if head_major:
        out_shape = jax.ShapeDtypeStruct((ncols // LANES, T, LANES), out_dtype)
        out_spec = pl.BlockSpec((tn // LANES, tm, LANES), lambda j, i: (j, i, 0))
    else:
        out_shape = jax.ShapeDtypeStruct((T, ncols), out_dtype)
        out_spec = pl.BlockSpec((tm, tn), lambda j, i: (i, j))
    return pl.pallas_call(
        functools.partial(_proj_kernel, rope=rope, n_tiles=n_tiles, head_major=head_major,
                          out_scale=float(out_scale)),
        out_shape=out_shape,
        grid=(n_tiles, T // tm),
        in_specs=[
            pl.BlockSpec((tm, K), lambda j, i: (i, 0)),
            pl.BlockSpec((None, K, tn), lambda j, i: (layer, 0, j + c0)),
            pl.BlockSpec((tm, LANES), lambda j, i: (i, 0)),
            pl.BlockSpec((tm, LANES), lambda j, i: (i, 0)),
        ],
        out_specs=out_spec,
        scratch_shapes=[pltpu.VMEM((K, tn), BF16)],
        compiler_params=_cparams(("arbitrary", "arbitrary")),
        name="proj",
    )(xb, w, cos2, sin2)


def _rope_tables(T):
    inv_freq = ROPE_THETA ** (-jnp.arange(0, HEAD_DIM, 2, dtype=F32) / HEAD_DIM)
    ang = jnp.arange(T, dtype=F32)[:, None] * inv_freq[None, :]
    cos, sin = jnp.cos(ang), jnp.sin(ang)
    return jnp.concatenate([cos, cos], axis=-1), jnp.concatenate([-sin, sin], axis=-1)


POOL_HALO = 16


def _pool_kernel(u_ref, h_ref, w_ref, sc_ref, o_ref):
    i = pl.program_id(0)
    tm = u_ref.shape[0]
    cur = u_ref[...]
    halo = jnp.where(i > 0, h_ref[...], 0.0)
    cat = jnp.concatenate([halo, cur], axis=0)
    t = i * tm + lax.broadcasted_iota(I32, (tm, 1), 0)
    G = POOL_GROUP_DIM
    for g, w in enumerate(POOL_WINDOWS):
        s = cat[:, g * G:(g + 1) * G]
        k = 1
        while k < w:
            s = s + pltpu.roll(s, k, 0)
            k *= 2
        win = s[POOL_HALO:]
        cnt = jnp.minimum(t + 1, w).astype(F32)
        p = win / cnt - cur[:, g * G:(g + 1) * G]
        y = jnp.dot(p.astype(BF16), w_ref[g].astype(BF16), preferred_element_type=F32)
        o_ref[:, g * G:(g + 1) * G] = (y * sc_ref[:, g * G:(g + 1) * G]).astype(o_ref.dtype)


def _pool(u, pool_w, layer, pool_scale, tm=512):
    T = u.shape[0]
    tm = min(tm, T)
    assert T % tm == 0 and tm % POOL_HALO == 0 and max(POOL_WINDOWS) <= POOL_HALO
    hb = tm // POOL_HALO
    return pl.pallas_call(
        _pool_kernel,
        out_shape=jax.ShapeDtypeStruct((T, POOL_DIM), BF16),
        grid=(T // tm,),
        in_specs=[
            pl.BlockSpec((tm, POOL_DIM), lambda i: (i, 0)),
            pl.BlockSpec((POOL_HALO, POOL_DIM), lambda i: (jnp.maximum(i * hb - 1, 0), 0)),
            pl.BlockSpec((None,) + pool_w.shape[1:], lambda i: (layer, 0, 0, 0)),
            pl.BlockSpec((1, POOL_DIM), lambda i: (0, 0)),
        ],
        out_specs=pl.BlockSpec((tm, POOL_DIM), lambda i: (i, 0)),
        compiler_params=_cparams(("arbitrary",)),
        name="pool",
    )(u, u, pool_w, pool_scale.reshape(1, POOL_DIM))


DIL_SPAN = BLOCK * max(d for _, d in DIL_PATTERNS)
DIL_UNROLL = 8


def _dil_kernel(*refs):
    ng = len(DIL_PATTERNS)
    in_refs = refs[:5 * ng]
    o_ref = refs[5 * ng]
    kf_ref, vf_ref, os_ref, ms_ref, ls_ref = refs[5 * ng + 1:]
    b = pl.program_id(1)
    scale = HEAD_DIM ** -0.5

    qi = lax.broadcasted_iota(I32, (BLOCK, 2 * BLOCK), 0)
    kj = lax.broadcasted_iota(I32, (BLOCK, 2 * BLOCK), 1)
    rel = BLOCK + qi - kj
    for g, (window, d) in enumerate(DIL_PATTERNS):
        q_ref, kc_ref, kp_ref, vc_ref, vp_ref = in_refs[5 * g:5 * g + 5]
        count = window // d
        band = (rel >= 0) & (rel <= count)
        bias_std = jnp.where(band, 0.0, NEG)
        bias_first = jnp.where(band & (kj >= BLOCK), 0.0, NEG)
        hist = BLOCK * d
        nsb = DIL_SPAN // hist
        kf_ref[0:hist, :] = kp_ref[...]
        kf_ref[hist:hist + DIL_SPAN, :] = kc_ref[...]
        vf_ref[0:hist, :] = vp_ref[...]
        vf_ref[hist:hist + DIL_SPAN, :] = vc_ref[...]

        def block(idx, g=g, d=d, hist=hist, nsb=nsb, q_ref=q_ref, bias_std=bias_std, bias_first=bias_first):
            r = idx // nsb
            sb = idx % nsb
            start = r + hist * sb
            q = q_ref[pl.ds(start, BLOCK, stride=d), :].astype(BF16)
            k = kf_ref[pl.ds(start, 2 * BLOCK, stride=d), :].astype(BF16)
            v = vf_ref[pl.ds(start, 2 * BLOCK, stride=d), :].astype(BF16)
            s = lax.dot_general(q, k, (((1,), (1,)), ((), ())), preferred_element_type=F32) * scale
            first = jnp.logical_and(b == 0, sb == 0)
            s = s + jnp.where(first, bias_first, bias_std)
            m = jnp.max(s, axis=-1, keepdims=True)
            p = jnp.exp(s - m)
            l = jnp.sum(p, axis=-1, keepdims=True)
            o = jnp.dot(p.astype(BF16), v, preferred_element_type=F32)
            os_ref[g, pl.ds(start, BLOCK, stride=d), :] = o
            ms_ref[g, pl.ds(start, BLOCK, stride=d), :] = jnp.broadcast_to(m, (BLOCK, LANES))
            ls_ref[g, pl.ds(start, BLOCK, stride=d), :] = jnp.broadcast_to(l, (BLOCK, LANES))

        def body(i, _, block=block):
            for u in range(DIL_UNROLL):
                block(i * DIL_UNROLL + u)
            return 0

        assert (d * nsb) % DIL_UNROLL == 0
        lax.fori_loop(0, d * nsb // DIL_UNROLL, body, 0)

    mx = ms_ref[0]
    for g in range(1, ng):
        mx = jnp.maximum(mx, ms_ref[g])
    num = jnp.zeros((DIL_SPAN, LANES), F32)
    den = jnp.zeros((DIL_SPAN, LANES), F32)
    for g in range(ng):
        e = jnp.exp(ms_ref[g] - mx)
        num = num + e * os_ref[g]
        den = den + e * ls_ref[g]
    o_ref[...] = (num / den).astype(o_ref.dtype)


def _dilated(qkv):
    T = qkv.shape[1]
    assert T % DIL_SPAN == 0 and HEAD_DIM == LANES
    H = DIL_HEADS_PER_GROUP
    in_specs, args = [], []
    max_hist = 0
    for g, (window, d) in enumerate(DIL_PATTERNS):
        assert window // d <= BLOCK and DIL_SPAN % (BLOCK * d) == 0
        hist = BLOCK * d
        max_hist = max(max_hist, hist)
        ratio = DIL_SPAN // hist

        def cur(j, b, base):
            return (base + j, b, 0)

        def prev(j, b, base, ratio=ratio):
            return (base + j, jnp.maximum(b * ratio - 1, 0), 0)

        qh, kh, vh = g * H, DIL_HEADS + g * H, 2 * DIL_HEADS + g * H
        in_specs += [
            pl.BlockSpec((None, DIL_SPAN, LANES), functools.partial(cur, base=qh)),
            pl.BlockSpec((None, DIL_SPAN, LANES), functools.partial(cur, base=kh)),
            pl.BlockSpec((None, hist, LANES), functools.partial(prev, base=kh)),
            pl.BlockSpec((None, DIL_SPAN, LANES), functools.partial(cur, base=vh)),
            pl.BlockSpec((None, hist, LANES), functools.partial(prev, base=vh)),
        ]
        args += [qkv] * 5
    ng = len(DIL_PATTERNS)
    return pl.pallas_call(
        _dil_kernel,
        out_shape=jax.ShapeDtypeStruct((T, DIL_OUT), BF16),
        grid=(H, T // DIL_SPAN),
        in_specs=in_specs,
        out_specs=pl.BlockSpec((DIL_SPAN, LANES), lambda j, b: (b, j)),
        scratch_shapes=[
            pltpu.VMEM((max_hist + DIL_SPAN, LANES), F32),
            pltpu.VMEM((max_hist + DIL_SPAN, LANES), F32),
            pltpu.VMEM((ng, DIL_SPAN, LANES), F32),
            pltpu.VMEM((ng, DIL_SPAN, LANES), F32),
            pltpu.VMEM((ng, DIL_SPAN, LANES), F32),
        ],
        compiler_params=_cparams(("arbitrary", "arbitrary")),
        name="dilated",
    )(*args)


def _out_ln_kernel(*refs, n_in, slabs):
    a_refs = refs[:n_in]
    w_refs = refs[n_in:2 * n_in]
    x_ref, g_ref, b_ref, o_ref, o2_ref = refs[2 * n_in:]
    acc = ALPHA * x_ref[...]
    for a_ref, w_ref in zip(a_refs, w_refs):
        acc = acc + jnp.dot(a_ref[...], w_ref[...], preferred_element_type=F32)
    y = _layer_norm(acc, g_ref[...], b_ref[...])
    o_ref[...] = y
    if slabs:
        tm, D = y.shape
        S = D // LANES
        for s in range(S):
            o2_ref[pl.ds(s, tm, stride=S), :] = y[:, s * LANES:(s + 1) * LANES]
    else:
        o2_ref[...] = y.astype(BF16)


def _out_ln(a_list, wb, x, g, b, *, slabs=False, tm=512):
    T, D = x.shape
    tm = min(tm, T)
    n_in = len(a_list)
    kk = a_list[0].shape[1]
    assert all(a.shape[1] == kk for a in a_list) and wb.shape[0] == n_in * kk and T % tm == 0
    w3 = wb.reshape(n_in, kk, D)
    in_specs = [pl.BlockSpec((tm, kk), lambda i: (i, 0)) for _ in range(n_in)]
    in_specs += [pl.BlockSpec((None, kk, D), functools.partial(lambda i, c: (c, 0, 0), c=c)) for c in range(n_in)]
    in_specs += [pl.BlockSpec((tm, D), lambda i: (i, 0)),
                 pl.BlockSpec((1, D), lambda i: (0, 0)),
                 pl.BlockSpec((1, D), lambda i: (0, 0))]
    S = D // LANES
    if slabs:
        out2 = jax.ShapeDtypeStruct((T * S, LANES), F32)
        spec2 = pl.BlockSpec((tm * S, LANES), lambda i: (i, 0))
    else:
        out2 = jax.ShapeDtypeStruct((T, D), BF16)
        spec2 = pl.BlockSpec((tm, D), lambda i: (i, 0))
    return pl.pallas_call(
        functools.partial(_out_ln_kernel, n_in=n_in, slabs=slabs),
        out_shape=(jax.ShapeDtypeStruct((T, D), F32), out2),
        grid=(T // tm,),
        in_specs=in_specs,
        out_specs=(pl.BlockSpec((tm, D), lambda i: (i, 0)), spec2),
        compiler_params=_cparams(("arbitrary",)),
        name="out_ln",
    )(*a_list, *([w3] * n_in), x, g.reshape(1, D), b.reshape(1, D))


def _res_ln_kernel(x_ref, y_ref, g_ref, b_ref, o_ref, ob_ref):
    y = _layer_norm(ALPHA * x_ref[...] + y_ref[...], g_ref[...], b_ref[...])
    o_ref[...] = y
    ob_ref[...] = y.astype(BF16)


def _res_ln(x, y, g, b, tm=512):
    T, D = x.shape
    tm = min(tm, T)
    row = pl.BlockSpec((tm, D), lambda i: (i, 0))
    vec = pl.BlockSpec((1, D), lambda i: (0, 0))
    return pl.pallas_call(
        _res_ln_kernel,
        out_shape=(jax.ShapeDtypeStruct((T, D), F32), jax.ShapeDtypeStruct((T, D), BF16)),
        grid=(T // tm,),
        in_specs=[row, row, vec, vec],
        out_specs=(row, row),
        compiler_params=_cparams(("arbitrary",)),
        name="res_ln",
    )(x, y, g.reshape(1, D), b.reshape(1, D))


FFN_TM = 1024
FFN_TM_MOE = 2048
FFN_TF = 512
FFN_CHUNK = 256


def _ffn_kernel(vt_ref, ve_ref, vlo_ref, vhi_ref, vok_ref, x_ref, wg_ref, wu_ref, wd_ref, o_ref):
    v = pl.program_id(0)
    f = pl.program_id(1)
    lo = vlo_ref[v]
    hi = vhi_ref[v]

    @pl.when(jnp.logical_and(vok_ref[v] == 2, f == 0))
    def _():
        o_ref[...] = jnp.zeros(o_ref.shape, o_ref.dtype)

    def rows(c):
        return pl.ds(pl.multiple_of(c * FFN_CHUNK, FFN_CHUNK), FFN_CHUNK)

    @pl.when(f == 0)
    def _():
        def zero(c, _):
            o_ref[rows(c), :] = jnp.zeros((FFN_CHUNK, o_ref.shape[1]), o_ref.dtype)
            return 0
        lax.fori_loop(lo, hi, zero, 0)

    def swiglu(c, nchunks):
        r = pl.ds(pl.multiple_of(c * FFN_CHUNK, FFN_CHUNK), nchunks * FFN_CHUNK)
        x = x_ref[r, :]
        g = jnp.dot(x, wg_ref[...].astype(BF16), preferred_element_type=F32)
        u = jnp.dot(x, wu_ref[...].astype(BF16), preferred_element_type=F32)
        h = (g * jax.nn.sigmoid(g) * u).astype(BF16)
        o_ref[r, :] += jnp.dot(h, wd_ref[...].astype(BF16), preferred_element_type=F32)

    def pair(i, _):
        swiglu(lo + 2 * i, 2)
        return 0

    n = hi - lo
    lax.fori_loop(0, n // 2, pair, 0)

    @pl.when(n % 2 == 1)
    def _():
        swiglu(hi - 1, 1)


def _ffn(visits, xs, wg, wu, wd, *, tm):
    vt, ve, vlo, vhi, vok = visits
    R, D = xs.shape
    F = wg.shape[2]
    assert R % tm == 0 and F % FFN_TF == 0 and tm % FFN_CHUNK == 0
    nf = F // FFN_TF
    V = vt.shape[0]
    row_mode = {} if tm <= FFN_TM else dict(pipeline_mode=pl.Buffered(1))

    def fidx(v, f, vok):
        return jnp.where(vok[v] == 1, f, nf - 1)

    return pl.pallas_call(
        _ffn_kernel,
        out_shape=jax.ShapeDtypeStruct((R, D), F32),
        grid_spec=pltpu.PrefetchScalarGridSpec(
            num_scalar_prefetch=5,
            grid=(V, nf),
            in_specs=[
                pl.BlockSpec((tm, D), lambda v, f, vt, ve, vlo, vhi, vok: (vt[v], 0), **row_mode),
                pl.BlockSpec((None, D, FFN_TF), lambda v, f, vt, ve, vlo, vhi, vok: (ve[v], 0, fidx(v, f, vok))),
                pl.BlockSpec((None, D, FFN_TF), lambda v, f, vt, ve, vlo, vhi, vok: (ve[v], 0, fidx(v, f, vok))),
                pl.BlockSpec((None, FFN_TF, D), lambda v, f, vt, ve, vlo, vhi, vok: (ve[v], fidx(v, f, vok), 0)),
            ],
            out_specs=pl.BlockSpec((tm, D), lambda v, f, vt, ve, vlo, vhi, vok: (vt[v], 0), **row_mode),
        ),
        compiler_params=_cparams(("arbitrary", "arbitrary")),
        name="ffn",
    )(vt, ve, vlo, vhi, vok, xs, wg, wu, wd)


def _dense_visits(T, layer, tm):
    nt = T // tm
    cpt = tm // FFN_CHUNK
    z = jnp.zeros((nt,), I32)
    return (jnp.arange(nt, dtype=I32), z + layer, z, z + cpt, z + 1)


def _router_kernel(x_ref, r_ref, o_ref):
    logits = jnp.dot(x_ref[...], r_ref[...], preferred_element_type=F32, precision=lax.Precision.HIGHEST)
    lane = lax.broadcasted_iota(I32, logits.shape, 1)
    logits = jnp.where(lane < N_EXPERTS, logits, -jnp.inf)
    v1 = jnp.max(logits, axis=-1, keepdims=True)
    i1 = jnp.min(jnp.where(logits == v1, lane, LANES), axis=-1, keepdims=True)
    rest = jnp.where(lane == i1, -jnp.inf, logits)
    v2 = jnp.max(rest, axis=-1, keepdims=True)
    i2 = jnp.min(jnp.where(rest == v2, lane, LANES), axis=-1, keepdims=True)
    e2 = jnp.exp(v2 - v1)
    g1 = 1.0 / (1.0 + e2)
    g2 = e2 / (1.0 + e2)
    out = jnp.where(lane == 0, i1.astype(F32), 0.0)
    out = jnp.where(lane == 1, i2.astype(F32), out)
    out = jnp.where(lane == 2, g1, out)
    out = jnp.where(lane == 3, g2, out)
    o_ref[...] = out


def _router(x, router, tm=512):
    T, D = x.shape
    tm = min(tm, T)
    rp = jnp.pad(router, ((0, 0), (0, LANES - N_EXPERTS)))
    return pl.pallas_call(
        _router_kernel,
        out_shape=jax.ShapeDtypeStruct((T, LANES), F32),
        grid=(T // tm,),
        in_specs=[pl.BlockSpec((tm, D), lambda i: (i, 0)), pl.BlockSpec((D, LANES), lambda i: (0, 0))],
        out_specs=pl.BlockSpec((tm, LANES), lambda i: (i, 0)),
        compiler_params=_cparams(("arbitrary",)),
        name="router",
    )(x, rp)


GATHER_ROWS = 256


def _row_copy(src_hbm, row, dst, r, sem):
    return pltpu.make_async_copy(src_hbm.at[pl.ds(row, 1), :], dst.at[pl.ds(r, 1), :], sem)


def _prefetch_rows(issue, slot):
    i = pl.program_id(0)

    @pl.when(i == 0)
    def _():
        issue(0, 0)

    @pl.when(i + 1 < pl.num_programs(0))
    def _():
        issue(i + 1, 1 - slot)


def _gather_kernel(tok_ref, x_hbm, o_ref, buf, sem, *, S):
    slot = pl.program_id(0) % 2

    def issue(step, s):
        def body(r2, _):
            for k in range(2):
                r = 2 * r2 + k
                src = pl.ds(pl.multiple_of(tok_ref[step * GATHER_ROWS + r] * S, S), S)
                dst = pl.ds(pl.multiple_of(r * S, S), S)
                pltpu.make_async_copy(x_hbm.at[src, :], buf.at[s, dst, :], sem.at[s]).start(priority=k)
            return 0
        lax.fori_loop(0, GATHER_ROWS // 2, body, 0)

    _prefetch_rows(issue, slot)
    pltpu.make_async_copy(x_hbm.at[pl.ds(0, GATHER_ROWS * S), :], buf.at[slot], sem.at[slot]).wait()
    for s in range(S):
        o_ref[:, s * LANES:(s + 1) * LANES] = buf[slot, pl.ds(s, GATHER_ROWS, stride=S), :].astype(o_ref.dtype)


def _gather_rows(xslab, tok, D):
    R = tok.shape[0]
    S = D // LANES
    assert R % GATHER_ROWS == 0 and xslab.shape[0] >= GATHER_ROWS * S and S % 8 == 0
    return pl.pallas_call(
        functools.partial(_gather_kernel, S=S),
        out_shape=jax.ShapeDtypeStruct((R, D), BF16),
        grid_spec=pltpu.PrefetchScalarGridSpec(
            num_scalar_prefetch=1,
            grid=(R // GATHER_ROWS,),
            in_specs=[pl.BlockSpec(memory_space=pl.ANY)],
            out_specs=pl.BlockSpec((GATHER_ROWS, D), lambda i, tok: (i, 0)),
            scratch_shapes=[pltpu.VMEM((2, GATHER_ROWS * S, LANES), F32), pltpu.SemaphoreType.DMA((2,))],
        ),
        compiler_params=_cparams(("arbitrary",)),
        name="gather_rows",
    )(tok, xslab)


def _combine_kernel(p0_ref, p1_ref, y_hbm, r_ref, x_ref, g_ref, b_ref, o_ref, ob_ref, buf, sem):
    slot = pl.program_id(0) % 2

    def issue(step, s):
        def body(r, _):
            t = step * GATHER_ROWS + r
            _row_copy(y_hbm, p0_ref[t], buf.at[s, 0], r, sem.at[s]).start(priority=0)
            _row_copy(y_hbm, p1_ref[t], buf.at[s, 1], r, sem.at[s]).start(priority=1)
            return 0
        lax.fori_loop(0, GATHER_ROWS, body, 0)

    _prefetch_rows(issue, slot)
    for k in range(TOP_K):
        pltpu.make_async_copy(y_hbm.at[pl.ds(0, GATHER_ROWS), :], buf.at[slot, k], sem.at[slot]).wait()
    route = r_ref[...]
    g0 = route[:, TOP_K:TOP_K + 1]
    g1 = route[:, TOP_K + 1:TOP_K + 2]
    z = ALPHA * x_ref[...] + (g0 * buf[slot, 0] + g1 * buf[slot, 1])
    y = _layer_norm(z, g_ref[...], b_ref[...])
    o_ref[...] = y
    ob_ref[...] = y.astype(BF16)


def _combine_ln(y, p0, p1, route, x, g, b):
    T, D = x.shape
    assert T % GATHER_ROWS == 0 and y.shape[0] >= GATHER_ROWS
    row = pl.BlockSpec((GATHER_ROWS, D), lambda i, p0, p1: (i, 0))
    vec = pl.BlockSpec((1, D), lambda i, p0, p1: (0, 0))
    rspec = pl.BlockSpec((GATHER_ROWS, LANES), lambda i, p0, p1: (i, 0))
    return pl.pallas_call(
        _combine_kernel,
        out_shape=(jax.ShapeDtypeStruct((T, D), F32), jax.ShapeDtypeStruct((T, D), BF16)),
        grid_spec=pltpu.PrefetchScalarGridSpec(
            num_scalar_prefetch=2,
            grid=(T // GATHER_ROWS,),
            in_specs=[pl.BlockSpec(memory_space=pl.ANY), rspec, row, vec, vec],
            out_specs=(row, row),
            scratch_shapes=[pltpu.VMEM((2, TOP_K, GATHER_ROWS, D), F32), pltpu.SemaphoreType.DMA((2,))],
        ),
        compiler_params=_cparams(("arbitrary",)),
        name="combine_ln",
    )(p0, p1, y, route, x, g.reshape(1, D), b.reshape(1, D))


def _moe_plan(route, T, layer, tm):
    E = N_EXPERTS
    e = route[:, :TOP_K].astype(I32)
    ef = e.reshape(-1)
    onehot = (ef[:, None] == jnp.arange(E, dtype=I32)[None, :]).astype(I32)
    rank = jnp.cumsum(onehot, axis=0) - onehot
    counts = jnp.sum(onehot, axis=0)
    chunks = (counts + FFN_CHUNK - 1) // FFN_CHUNK
    seg_end = jnp.cumsum(chunks)
    seg_start = seg_end - chunks
    pos = seg_start[ef] * FFN_CHUNK + jnp.sum(rank * onehot, axis=1)
    R = -(-(TOP_K * T + E * FFN_CHUNK) // tm) * tm
    cpt = tm // FFN_CHUNK
    nt = R // tm
    n_slots = nt + E
    slot = jnp.arange(n_slots, dtype=I32)

    def place(vals, dest):
        return jnp.sum(jnp.where(dest[None, :] == slot[:, None], vals[None, :], 0), axis=1).astype(I32)

    tile_pts = jnp.arange(nt, dtype=I32) * cpt
    seg_pts = seg_start.astype(I32)
    tile_rank = jnp.arange(nt, dtype=I32) + jnp.sum(seg_pts[None, :] < tile_pts[:, None], axis=1)
    seg_rank = jnp.arange(E, dtype=I32) + jnp.sum(tile_pts[None, :] <= seg_pts[:, None], axis=1)
    pts = place(jnp.concatenate([tile_pts, seg_pts]), jnp.concatenate([tile_rank, seg_rank]))
    used = seg_end[-1]
    used_tiles_end = (used + cpt - 1) // cpt * cpt
    nxt = jnp.concatenate([pts[1:], jnp.array([nt * cpt], I32)])
    ok = (pts < used) & (nxt > pts)
    nxt = jnp.where(nxt >= used, jnp.maximum(used_tiles_end, pts), nxt)
    tile = pts // cpt
    nxt = jnp.minimum(nxt, (tile + 1) * cpt)
    exp = jnp.minimum(jnp.sum(seg_end[None, :] <= pts[:, None], axis=1), E - 1).astype(I32)
    last = jnp.max(jnp.where(ok, slot, 0))
    vt = jnp.where(ok, tile, jnp.sum(jnp.where(slot == last, tile, 0)))
    ve = jnp.where(ok, exp, jnp.sum(jnp.where(slot == last, exp, 0)))
    vlo = jnp.where(ok, pts - tile * cpt, 0)
    vhi = jnp.where(ok, nxt - tile * cpt, 0)
    oki = ok.astype(I32)
    dest = jnp.where(ok, jnp.cumsum(oki) - oki, jnp.sum(oki) + jnp.cumsum(1 - oki) - (1 - oki))
    vt, ve, vlo, vhi, vok = (place(a, dest) for a in (vt, ve, vlo, vhi, oki))
    used_tiles = used_tiles_end // cpt
    fill = slot - (n_slots - (nt - used_tiles))
    vt = jnp.where(fill >= 0, used_tiles + fill, vt).astype(I32)
    vok = jnp.where(fill >= 0, 2, vok).astype(I32)
    visits = (vt, (ve + layer * E).astype(I32), vlo, vhi, vok)
    p = pos.reshape(T, TOP_K).astype(I32)
    tok = jnp.zeros((R,), I32).at[pos].set(jnp.arange(TOP_K * T, dtype=I32) // TOP_K)
    return tok, visits, p[:, 0], p[:, 1]


DSA_TK = 512
NT_DIMS = (((1,), (1,)), ((), ()))
DSA_Q_SCALE = (HEAD_DIM ** -0.5) * LOG2_E


DSA_DIGIT = 15


def _srl(x, k):
    return lax.shift_right_logical(x, jnp.full(x.shape, k, x.dtype))


def _dsa_kernel(q_ref, iq_ref, iw_ref, ike_ref, iko_ref, kt_ref, v_ref, o_ref,
                key_s, pk_s, bias_s, s_s, p_s, m_s, l_s, al_s, acc_s, *, ksel, idx_bits):
    n = pl.program_id(0)
    TK = DSA_TK
    nkt = (n * BLOCK + BLOCK + TK - 1) // TK
    qpos = n * BLOCK + lax.broadcasted_iota(I32, (BLOCK, 1), 0)
    lane_k = lax.broadcasted_iota(I32, (1, TK), 1)

    def keys(j):
        return pl.ds(pl.multiple_of(j * TK, TK), TK)

    iqb = iq_ref[...]
    npair = IDX_HEADS // 2
    lhs = jnp.concatenate([iqb[:, p * LANES:(p + 1) * LANES] for p in range(npair)], axis=0)
    wts = iw_ref[...] * ((IDX_DIM ** -0.5) * (IDX_HEADS ** -0.5))

    def score_tile(j, _):
        le = lax.dot_general(lhs, ike_ref[keys(j), :], NT_DIMS, preferred_element_type=F32)
        lo = lax.dot_general(lhs, iko_ref[keys(j), :], NT_DIMS, preferred_element_type=F32)
        sc = jnp.zeros((BLOCK, TK), F32)
        for p in range(npair):
            sc = sc + jnp.maximum(le[p * BLOCK:(p + 1) * BLOCK], 0.0) * wts[:, 2 * p:2 * p + 1]
            sc = sc + jnp.maximum(lo[p * BLOCK:(p + 1) * BLOCK], 0.0) * wts[:, 2 * p + 1:2 * p + 2]
        bits = pltpu.bitcast(sc, I32)
        key = bits ^ ((bits >> 31) & 0x7FFFFFFF)
        key = jnp.where(j * TK + lane_k <= qpos, key, INT_MIN)
        key_s[j] = key
        pk_s[j] = pack(_srl(key ^ INT_MIN, 32 - DSA_DIGIT))
        return 0

    half = TK // 2
    guard = jnp.int32(-2147450880)
    ones2 = jnp.int32(0x00010001)

    def pack(d):
        return d[:, :half] | jnp.left_shift(d[:, half:], 16) | guard

    def packed_count(c):
        cc = c | jnp.left_shift(c, 16)

        def body(j, acc):
            hit = _srl(pk_s[j] - cc, DSA_DIGIT) & ones2
            for k in range(half // LANES):
                acc = acc + hit[:, k * LANES:(k + 1) * LANES]
            return acc
        acc = lax.fori_loop(0, nkt, body, jnp.zeros((BLOCK, LANES), I32))
        tot = (acc & 0xFFFF) + _srl(acc, 16)
        return jnp.sum(tot.astype(F32), axis=1, keepdims=True)

    lax.fori_loop(0, nkt, score_tile, 0)

    def count(pred):
        def body(j, acc):
            hit = jnp.where(pred(key_s[j], j * TK + lane_k), 1.0, 0.0)
            for c in range(TK // LANES):
                acc = acc + hit[:, c * LANES:(c + 1) * LANES]
            return acc
        acc = lax.fori_loop(0, nkt, body, jnp.zeros((BLOCK, LANES), F32))
        return jnp.sum(acc, axis=1, keepdims=True)

    searching = (n + 1) * BLOCK > ksel

    def write_bias(thr, pidx):
        def body(j, _):
            key = key_s[j]
            sel = (key > thr) | ((key == thr) & (j * TK + lane_k <= pidx))
            bias_s[j] = jnp.where(sel, 0.0, NEG)
            return 0
        lax.fori_loop(0, nkt, body, 0)

    @pl.when(jnp.logical_not(searching))
    def _():
        write_bias(jnp.full((BLOCK, 1), INT_MIN, I32), jnp.full((BLOCK, 1), -1, I32))

    @pl.when(searching)
    def _():
        def digit_search(base, cnt_thr, miss):
            def step(carry):
                i, d, cnt_thr, _ = carry
                c = d | jnp.left_shift(jnp.int32(1), DSA_DIGIT - 1 - i)
                cnt = base + packed_count(c)
                ok = cnt >= ksel
                cnt_thr = jnp.where(ok, cnt, cnt_thr)
                return i + 1, jnp.where(ok, c, d), cnt_thr, jnp.max(jnp.abs(cnt_thr - ksel))

            _, d, cnt_thr, miss = lax.while_loop(
                lambda c: jnp.logical_and(c[0] < DSA_DIGIT, c[3] > 0.0), step,
                (jnp.int32(0), jnp.zeros((BLOCK, 1), I32), cnt_thr, miss))
            return d, cnt_thr, miss

        zero = jnp.zeros((BLOCK, 1), F32)
        d_hi, cnt_thr, miss = digit_search(zero, zero - 1.0, jnp.float32(1.0))

        top = (1 << DSA_DIGIT) - 1
        above = jnp.where(d_hi == top, 0.0, packed_count(jnp.minimum(d_hi + 1, top)))

        @pl.when(miss > 0.0)
        def _():
            def body(j, _):
                u = key_s[j] ^ INT_MIN
                mid = _srl(u, 32 - 2 * DSA_DIGIT) & top
                pk_s[j] = pack(jnp.where(_srl(u, 32 - DSA_DIGIT) == d_hi, mid, 0))
                return 0
            lax.fori_loop(0, nkt, body, 0)

        d_mid, cnt_thr, miss = digit_search(above, cnt_thr, miss)
        thr = (jnp.left_shift(d_hi, 32 - DSA_DIGIT) | jnp.left_shift(d_mid, 32 - 2 * DSA_DIGIT)) ^ INT_MIN

        def bit_step(carry):
            i, thr, cnt_thr, _ = carry
            cand = thr + jnp.left_shift(jnp.int32(1), 31 - 2 * DSA_DIGIT - i)
            cnt = count(lambda key, kidx: key >= cand)
            ok = cnt >= ksel
            cnt_thr = jnp.where(ok, cnt, cnt_thr)
            return i + 1, jnp.where(ok, cand, thr), cnt_thr, jnp.max(jnp.abs(cnt_thr - ksel))

        _, thr, _, miss = lax.while_loop(
            lambda c: jnp.logical_and(c[0] < 32 - 2 * DSA_DIGIT, c[3] > 0.0), bit_step,
            (jnp.int32(0), thr, cnt_thr, miss))
        exact = miss == 0.0

        @pl.when(exact)
        def _():
            write_bias(thr, jnp.full((BLOCK, 1), 2 ** 30, I32))

        @pl.when(jnp.logical_not(exact))
        def _():
            need = ksel - count(lambda key, kidx: key > thr)

            def idx_step(i, pidx):
                cand = pidx + jnp.left_shift(jnp.int32(1), idx_bits - 1 - i)
                c = count(lambda key, kidx: (key == thr) & (kidx < cand))
                return jnp.where(c < need, cand, pidx)

            pidx = lax.fori_loop(0, idx_bits, idx_step, jnp.zeros((BLOCK, 1), I32))
            write_bias(thr, pidx)

    qb = q_ref[...]
    qs = jnp.concatenate([qb[:, h * LANES:(h + 1) * LANES] for h in range(DSA_HEADS)], axis=0)
    ncol = TK // LANES
    m_s[...] = jnp.full(m_s.shape, NEG, F32)
    l_s[...] = jnp.zeros(l_s.shape, F32)
    acc_s[...] = jnp.zeros(acc_s.shape, F32)

    def attend(j, _):
        s_s[...] = lax.dot_general(qs, kt_ref[keys(j), :], NT_DIMS, preferred_element_type=F32)
        for h in range(DSA_HEADS):
            r = slice(h * BLOCK, (h + 1) * BLOCK)
            t = [s_s[r, c * LANES:(c + 1) * LANES] + bias_s[j, :, c * LANES:(c + 1) * LANES] for c in range(ncol)]
            mx = t[0]
            for c in range(1, ncol):
                mx = jnp.maximum(mx, t[c])
            m_old = m_s[r, :]
            m_new = jnp.maximum(m_old, jnp.max(mx, axis=-1, keepdims=True))
            alpha = jnp.exp2(m_old - m_new)
            psum = jnp.zeros((BLOCK, LANES), F32)
            for c in range(ncol):
                p = jnp.exp2(t[c] - m_new)
                psum = psum + p
                p_s[r, c * LANES:(c + 1) * LANES] = p.astype(BF16)
            m_s[r, :] = m_new
            l_s[r, :] = alpha * l_s[r, :] + jnp.sum(psum, axis=-1, keepdims=True)
            al_s[r, :] = alpha
        acc_s[...] = al_s[...] * acc_s[...] + jnp.dot(p_s[...], v_ref[keys(j), :], preferred_element_type=F32)
        return 0

    lax.fori_loop(0, nkt, attend, 0)
    for h in range(DSA_HEADS):
        r = slice(h * BLOCK, (h + 1) * BLOCK)
        o_ref[:, h * LANES:(h + 1) * LANES] = (acc_s[r, :] / l_s[r, :]).astype(o_ref.dtype)


DSA_SIDE_IQ = IDX_HEADS * IDX_DIM
DSA_SIDE_COLS = DSA_SIDE_IQ + 4 * LANES


def _dsa_side_weights(w_in_l):
    o = DSA_Q
    wk = w_in_l[:, o:o + HEAD_DIM]; o += HEAD_DIM
    wv = w_in_l[:, o:o + HEAD_DIM]; o += HEAD_DIM
    wiq = w_in_l[:, o:o + DSA_SIDE_IQ]; o += DSA_SIDE_IQ
    wik = w_in_l[:, o:o + IDX_DIM]; o += IDX_DIM
    wiw = w_in_l[:, o:o + IDX_HEADS]
    z = jnp.zeros_like(wik)
    side = jnp.concatenate([wiq, wk, wv, wik, z, z, wik], axis=1)
    return side, jnp.pad(wiw, ((0, 0), (0, LANES - IDX_HEADS)))


def _dsa(q, side, iw):
    T = q.shape[0]
    TK = DSA_TK
    assert T % TK == 0 and TK % BLOCK == 0 and HEAD_DIM == LANES and 2 * IDX_DIM == LANES
    assert DSA_SIDE_IQ % LANES == 0
    ksel = min(DSA_TOPK, T // 4)
    assert ksel % BLOCK == 0
    nt = T // TK
    rows = DSA_HEADS * BLOCK
    g0 = DSA_SIDE_IQ // LANES

    def key_side(group):
        return pl.BlockSpec((T, LANES), lambda n: (0, group))

    return pl.pallas_call(
        functools.partial(_dsa_kernel, ksel=ksel, idx_bits=max(1, (T - 1).bit_length())),
        out_shape=jax.ShapeDtypeStruct((T, DSA_Q), BF16),
        grid=(T // BLOCK,),
        in_specs=[
            pl.BlockSpec((BLOCK, DSA_Q), lambda n: (n, 0)),
            pl.BlockSpec((BLOCK, DSA_SIDE_IQ), lambda n: (n, 0)),
            pl.BlockSpec((BLOCK, LANES), lambda n: (n, 0)),
            key_side(g0 + 2), key_side(g0 + 3), key_side(g0), key_side(g0 + 1),
        ],
        out_specs=pl.BlockSpec((BLOCK, DSA_Q), lambda n: (n, 0)),
        scratch_shapes=[
            pltpu.VMEM((nt, BLOCK, TK), I32),
            pltpu.VMEM((nt, BLOCK, TK // 2), I32),
            pltpu.VMEM((nt, BLOCK, TK), F32),
            pltpu.VMEM((rows, TK), F32),
            pltpu.VMEM((rows, TK), BF16),
            pltpu.VMEM((rows, LANES), F32),
            pltpu.VMEM((rows, LANES), F32),
            pltpu.VMEM((rows, LANES), F32),
            pltpu.VMEM((rows, HEAD_DIM), F32),
        ],
        compiler_params=_cparams(("arbitrary",)),
        name="dsa",
    )(q, side, iw, side, side, side, side)


def _even_layer(x, xb, rope, i, w_in, pool_w, pool_scale, w_out, ln1_g, ln1_b, wg, wu, wd, ln2_g, ln2_b):
    T = x.shape[0]
    cos2, sin2 = rope
    u = _proj(xb, w_in, i, cos2, sin2, col0=0, ncols=POOL_DIM, tn=1024, rope=(0, 0), out_dtype=F32)
    qkv = _proj(xb, w_in, i, cos2, sin2, col0=POOL_DIM, ncols=3 * DIL_QKV, tn=1024, rope=(0, 2 * DIL_QKV),
                out_dtype=F32, head_major=True)
    pooled = _pool(u, pool_w, i, pool_scale[i])
    attn = _dilated(qkv)
    x1, x1b = _out_ln([pooled, attn], w_out[i].astype(BF16), x, ln1_g[i], ln1_b[i])
    tm = min(FFN_TM, T)
    y = _ffn(_dense_visits(T, i, tm), x1b, wg, wu, wd, tm=tm)
    return _res_ln(x1, y, ln2_g[i], ln2_b[i])


def _odd_layer(x, xb, rope, i, w_in, w_out, ln1_g, ln1_b, router, wg, wu, wd, ln2_g, ln2_b):
    T, D = x.shape
    cos2, sin2 = rope
    q = _proj(xb, w_in, i, cos2, sin2, col0=0, ncols=DSA_Q, tn=1024, rope=(0, DSA_Q), out_dtype=BF16,
              out_scale=DSA_Q_SCALE)
    w_side, w_iw = _dsa_side_weights(w_in[i])
    side = _proj(xb, w_side[None], 0, cos2, sin2, col0=0, ncols=DSA_SIDE_COLS, tn=DSA_SIDE_COLS,
                 rope=(DSA_SIDE_IQ, DSA_SIDE_IQ + HEAD_DIM), out_dtype=BF16)
    iw = _proj(xb, w_iw[None], 0, cos2, sin2, col0=0, ncols=LANES, tn=LANES, rope=(0, 0), out_dtype=F32)
    attn = _dsa(q, side, iw)
    x1, x1_slabs = _out_ln([attn], w_out[i].astype(BF16), x, ln1_g[i], ln1_b[i], slabs=True)
    route = _router(x1, router[i])
    tok, visits, p0, p1 = _moe_plan(route, T, i, FFN_TM_MOE)
    xs = _gather_rows(x1_slabs, tok, D)
    F = wg.shape[3]
    y = _ffn(visits, xs, wg.reshape(-1, D, F), wu.reshape(-1, D, F), wd.reshape(-1, F, D), tm=FFN_TM_MOE)
    return _combine_ln(y, p0, p1, route, x1, ln2_g[i], ln2_b[i])


def kernel(x, ev_w_in, ev_pool_w, ev_pool_scale, ev_w_out, ev_ln1_g, ev_ln1_b, ev_ffn_w_gate, ev_ffn_w_up,
           ev_ffn_w_down, ev_ln2_g, ev_ln2_b, od_w_in, od_w_out, od_ln1_g, od_ln1_b, od_router,
           od_exp_w_gate, od_exp_w_up, od_exp_w_down, od_ln2_g, od_ln2_b):
    B, T, D = x.shape
    rope = _rope_tables(T)
    outs = []
    for bi in range(B):
        h = x[bi]
        hb = h.astype(BF16)
        for layer in range(DEPTH):
            i = layer // 2
            if layer % 2 == 0:
                h, hb = _even_layer(h, hb, rope, i, ev_w_in, ev_pool_w, ev_pool_scale, ev_w_out,
                                    ev_ln1_g, ev_ln1_b, ev_ffn_w_gate, ev_ffn_w_up,
                                    ev_ffn_w_down, ev_ln2_g, ev_ln2_b)
            else:
                h, hb = _odd_layer(h, hb, rope, i, od_w_in, od_w_out, od_ln1_g, od_ln1_b,
                                   od_router, od_exp_w_gate, od_exp_w_up, od_exp_w_down,
                                   od_ln2_g, od_ln2_b)
        outs.append(h)
    return jnp.stack(outs, axis=0)
```

```python
import functools

import jax
import jax.numpy as jnp
from jax import lax
from jax.experimental import pallas as pl
from jax.experimental.pallas import tpu as pltpu

F32 = jnp.float32
BF16 = jnp.bfloat16
I32 = jnp.int32

D_MODEL = 2048
DEPTH = 4
HEAD_DIM = 128
ROPE_THETA = 10000.0
LN_EPS = 1e-5
BLOCK = 128
ALPHA = (2 * DEPTH) ** 0.25
POOL_WINDOWS = (2, 4, 8, 16)
POOL_GROUP_DIM = D_MODEL // 8
POOL_DIM = len(POOL_WINDOWS) * POOL_GROUP_DIM
DIL_PATTERNS = ((128, 1), (512, 4), (2048, 16))
DIL_HEADS_PER_GROUP = D_MODEL // 256
DIL_HEADS = len(DIL_PATTERNS) * DIL_HEADS_PER_GROUP
DIL_QKV = DIL_HEADS * HEAD_DIM
DIL_OUT = DIL_HEADS_PER_GROUP * HEAD_DIM
DSA_HEADS = D_MODEL // HEAD_DIM
IDX_HEADS = 16
IDX_DIM = 64
DSA_TOPK = 256
DSA_Q = DSA_HEADS * HEAD_DIM
D_FF = 5632
N_EXPERTS = 8
TOP_K = 2

LANES = 128
VMEM_LIMIT = 56 * 1024 * 1024

NEG = -1e30
LOG2_E = 1.4426950408889634
INT_MIN = -(2 ** 31)


def _cparams(sem):
    return pltpu.CompilerParams(dimension_semantics=sem, vmem_limit_bytes=VMEM_LIMIT)


def _layer_norm(z, g, b):
    mu = jnp.mean(z, axis=-1, keepdims=True)
    zc = z - mu
    var = jnp.mean(zc * zc, axis=-1, keepdims=True)
    return zc * lax.rsqrt(var + LN_EPS) * g + b


PROJ_HEADS_PER_DOT = 2


def _proj_kernel(x_ref, w_ref, cos_ref, sin_ref, o_ref, wb_ref, *, rope, n_tiles, head_major, out_scale):
    @pl.when(pl.program_id(1) == 0)
    def _():
        wb_ref[...] = w_ref[...].astype(BF16)

    tn = wb_ref.shape[1]
    nh = tn // LANES
    if n_tiles == 1:
        lo, hi = rope[0] // LANES, rope[1] // LANES
        cos, sin = cos_ref[...], sin_ref[...]
    else:
        j = pl.program_id(0)
        rot = jnp.logical_and(j >= rope[0] // tn, j < rope[1] // tn)
        lo, hi = (0, nh) if rope[1] > rope[0] else (0, 0)
        cos = jnp.where(rot, cos_ref[...], 1.0)
        sin = jnp.where(rot, sin_ref[...], 0.0)

    x = x_ref[...]
    step = min(nh, PROJ_HEADS_PER_DOT)
    for c0 in range(0, nh, step):
        n = min(step, nh - c0)
        acc = jnp.dot(x, wb_ref[:, c0 * LANES:(c0 + n) * LANES], preferred_element_type=F32)
        for c in range(c0, c0 + n):
            a = acc[:, (c - c0) * LANES:(c - c0 + 1) * LANES]
            if lo <= c < hi:
                a = a * cos + pltpu.roll(a, HEAD_DIM // 2, 1) * sin
            if out_scale != 1.0:
                a = a * out_scale
            if head_major:
                o_ref[c] = a.astype(o_ref.dtype)
            else:
                o_ref[:, c * LANES:(c + 1) * LANES] = a.astype(o_ref.dtype)


def _proj(xb, w, layer, cos2, sin2, *, col0, ncols, tn, rope, out_dtype, head_major=False, out_scale=1.0,
          tm=1024):
    T, K = xb.shape
    tm = min(tm, T)
    assert T % tm == 0 and ncols % tn == 0 and col0 % tn == 0 and tn % LANES == 0
    assert all(r % (LANES if ncols == tn else tn) == 0 for r in rope)
    n_tiles = ncols // tn
    c0 = col0 // tn
    if head_major:
        out_shape = jax.ShapeDtypeStruct((ncols // LANES, T, LANES), out_dtype)
        out_spec = pl.BlockSpec((tn // LANES, tm, LANES), lambda j, i: (j, i, 0))
    else:
        out_shape = jax.ShapeDtypeStruct((T, ncols), out_dtype)
        out_spec = pl.BlockSpec((tm, tn), lambda j, i: (i, j))
    return pl.pallas_call(
        functools.partial(_proj_kernel, rope=rope, n_tiles=n_tiles, head_major=head_major,
                          out_scale=float(out_scale)),
        out_shape=out_shape,
        grid=(n_tiles, T // tm),
        in_specs=[
            pl.BlockSpec((tm, K), lambda j, i: (i, 0)),
            pl.BlockSpec((None, K, tn), lambda j, i: (layer, 0, j + c0)),
            pl.BlockSpec((tm, LANES), lambda j, i: (i, 0)),
            pl.BlockSpec((tm, LANES), lambda j, i: (i, 0)),
        ],
        out_specs=out_spec,
        scratch_shapes=[pltpu.VMEM((K, tn), BF16)],
        compiler_params=_cparams(("arbitrary", "arbitrary")),
        name="proj",
    )(xb, w, cos2, sin2)


def _rope_tables(T):
    inv_freq = ROPE_THETA ** (-jnp.arange(0, HEAD_DIM, 2, dtype=F32) / HEAD_DIM)
    ang = jnp.arange(T, dtype=F32)[:, None] * inv_freq[None, :]
    cos, sin = jnp.cos(ang), jnp.sin(ang)
    return jnp.concatenate([cos, cos], axis=-1), jnp.concatenate([-sin, sin], axis=-1)


POOL_HALO = 16


def _pool_kernel(u_ref, h_ref, w_ref, sc_ref, o_ref):
    i = pl.program_id(0)
    tm = u_ref.shape[0]
    cur = u_ref[...]
    halo = jnp.where(i > 0, h_ref[...], 0.0)
    cat = jnp.concatenate([halo, cur], axis=0)
    t = i * tm + lax.broadcasted_iota(I32, (tm, 1), 0)
    G = POOL_GROUP_DIM
    for g, w in enumerate(POOL_WINDOWS):
        s = cat[:, g * G:(g + 1) * G]
        k = 1
        while k < w:
            s = s + pltpu.roll(s, k, 0)
            k *= 2
        win = s[POOL_HALO:]
        cnt = jnp.minimum(t + 1, w).astype(F32)
        p = win / cnt - cur[:, g * G:(g + 1) * G]
        y = jnp.dot(p.astype(BF16), w_ref[g].astype(BF16), preferred_element_type=F32)
        o_ref[:, g * G:(g + 1) * G] = (y * sc_ref[:, g * G:(g + 1) * G]).astype(o_ref.dtype)


def _pool(u, pool_w, layer, pool_scale, tm=512):
    T = u.shape[0]
    tm = min(tm, T)
    assert T % tm == 0 and tm % POOL_HALO == 0 and max(POOL_WINDOWS) <= POOL_HALO
    hb = tm // POOL_HALO
    return pl.pallas_call(
        _pool_kernel,
        out_shape=jax.ShapeDtypeStruct((T, POOL_DIM), BF16),
        grid=(T // tm,),
        in_specs=[
            pl.BlockSpec((tm, POOL_DIM), lambda i: (i, 0)),
            pl.BlockSpec((POOL_HALO, POOL_DIM), lambda i: (jnp.maximum(i * hb - 1, 0), 0)),
            pl.BlockSpec((None,) + pool_w.shape[1:], lambda i: (layer, 0, 0, 0)),
            pl.BlockSpec((1, POOL_DIM), lambda i: (0, 0)),
        ],
        out_specs=pl.BlockSpec((tm, POOL_DIM), lambda i: (i, 0)),
        compiler_params=_cparams(("arbitrary",)),
        name="pool",
    )(u, u, pool_w, pool_scale.reshape(1, POOL_DIM))


DIL_SPAN = BLOCK * max(d for _, d in DIL_PATTERNS)
DIL_UNROLL = 4


def _dil_kernel(*refs):
    ng = len(DIL_PATTERNS)
    in_refs = refs[:5 * ng]
    o_ref = refs[5 * ng]
    kf_ref, vf_ref, os_ref, ms_ref, ls_ref = refs[5 * ng + 1:]
    b = pl.program_id(1)
    scale = HEAD_DIM ** -0.5

    qi = lax.broadcasted_iota(I32, (BLOCK, 2 * BLOCK), 0)
    kj = lax.broadcasted_iota(I32, (BLOCK, 2 * BLOCK), 1)
    rel = BLOCK + qi - kj
    blocks = []
    n_blocks = DIL_SPAN // BLOCK
    for g, (window, d) in enumerate(DIL_PATTERNS):
        q_ref, kc_ref, kp_ref, vc_ref, vp_ref = in_refs[5 * g:5 * g + 5]
        count = window // d
        band = (rel >= 0) & (rel <= count)
        bias_std = jnp.where(band, 0.0, NEG)
        bias_first = jnp.where(band & (kj >= BLOCK), 0.0, NEG)
        hist = BLOCK * d
        nsb = DIL_SPAN // hist
        assert d * nsb == n_blocks
        kf_ref[g, 0:hist, :] = kp_ref[...]
        kf_ref[g, hist:hist + DIL_SPAN, :] = kc_ref[...]
        vf_ref[g, 0:hist, :] = vp_ref[...]
        vf_ref[g, hist:hist + DIL_SPAN, :] = vc_ref[...]

        def block(idx, g=g, d=d, hist=hist, nsb=nsb, q_ref=q_ref, bias_std=bias_std, bias_first=bias_first):
            r = idx // nsb
            sb = idx % nsb
            start = r + hist * sb
            q = q_ref[pl.ds(start, BLOCK, stride=d), :].astype(BF16)
            k = kf_ref[g, pl.ds(start, 2 * BLOCK, stride=d), :].astype(BF16)
            v = vf_ref[g, pl.ds(start, 2 * BLOCK, stride=d), :].astype(BF16)
            s = lax.dot_general(q, k, (((1,), (1,)), ((), ())), preferred_element_type=F32) * scale
            first = jnp.logical_and(b == 0, sb == 0)
            s = s + jnp.where(first, bias_first, bias_std)
            m = jnp.max(s, axis=-1, keepdims=True)
            p = jnp.exp(s - m)
            l = jnp.sum(p, axis=-1, keepdims=True)
            o = jnp.dot(p.astype(BF16), v, preferred_element_type=F32)
            os_ref[g, pl.ds(start, BLOCK, stride=d), :] = o
            ms_ref[g, pl.ds(start, BLOCK, stride=d), :] = jnp.broadcast_to(m, (BLOCK, LANES))
            ls_ref[g, pl.ds(start, BLOCK, stride=d), :] = jnp.broadcast_to(l, (BLOCK, LANES))

        blocks.append(block)

    def body(i, _):
        for u in range(DIL_UNROLL):
            for block in blocks:
                block(i * DIL_UNROLL + u)
        return 0

    assert n_blocks % DIL_UNROLL == 0
    lax.fori_loop(0, n_blocks // DIL_UNROLL, body, 0)

    mx = ms_ref[0]
    for g in range(1, ng):
        mx = jnp.maximum(mx, ms_ref[g])
    num = jnp.zeros((DIL_SPAN, LANES), F32)
    den = jnp.zeros((DIL_SPAN, LANES), F32)
    for g in range(ng):
        e = jnp.exp(ms_ref[g] - mx)
        num = num + e * os_ref[g]
        den = den + e * ls_ref[g]
    o_ref[...] = (num / den).astype(o_ref.dtype)


def _dilated(qkv):
    T = qkv.shape[1]
    assert T % DIL_SPAN == 0 and HEAD_DIM == LANES
    H = DIL_HEADS_PER_GROUP
    in_specs, args = [], []
    max_hist = 0
    for g, (window, d) in enumerate(DIL_PATTERNS):
        assert window // d <= BLOCK and DIL_SPAN % (BLOCK * d) == 0
        hist = BLOCK * d
        max_hist = max(max_hist, hist)
        ratio = DIL_SPAN // hist

        def cur(j, b, base):
            return (base + j, b, 0)

        def prev(j, b, base, ratio=ratio):
            return (base + j, jnp.maximum(b * ratio - 1, 0), 0)

        qh, kh, vh = g * H, DIL_HEADS + g * H, 2 * DIL_HEADS + g * H
        in_specs += [
            pl.BlockSpec((None, DIL_SPAN, LANES), functools.partial(cur, base=qh)),
            pl.BlockSpec((None, DIL_SPAN, LANES), functools.partial(cur, base=kh)),
            pl.BlockSpec((None, hist, LANES), functools.partial(prev, base=kh)),
            pl.BlockSpec((None, DIL_SPAN, LANES), functools.partial(cur, base=vh)),
            pl.BlockSpec((None, hist, LANES), functools.partial(prev, base=vh)),
        ]
        args += [qkv] * 5
    ng = len(DIL_PATTERNS)
    return pl.pallas_call(
        _dil_kernel,
        out_shape=jax.ShapeDtypeStruct((T, DIL_OUT), BF16),
        grid=(H, T // DIL_SPAN),
        in_specs=in_specs,
        out_specs=pl.BlockSpec((DIL_SPAN, LANES), lambda j, b: (b, j)),
        scratch_shapes=[
            pltpu.VMEM((ng, max_hist + DIL_SPAN, LANES), F32),
            pltpu.VMEM((ng, max_hist + DIL_SPAN, LANES), F32),
            pltpu.VMEM((ng, DIL_SPAN, LANES), F32),
            pltpu.VMEM((ng, DIL_SPAN, LANES), F32),
            pltpu.VMEM((ng, DIL_SPAN, LANES), F32),
        ],
        compiler_params=_cparams(("arbitrary", "arbitrary")),
        name="dilated",
    )(*args)


OUT_LN_PARTS = 2


def _out_ln_kernel(*refs, n_in, slabs):
    a_refs = refs[:n_in]
    w_refs = refs[n_in:2 * n_in]
    x_ref, g_ref, b_ref, o_ref, o2_ref = refs[2 * n_in:]
    tm, D = x_ref.shape
    S = D // LANES
    th = tm // OUT_LN_PARTS
    for h in range(OUT_LN_PARTS):
        r = slice(h * th, (h + 1) * th)
        acc = ALPHA * x_ref[r, :]
        for a_ref, w_ref in zip(a_refs, w_refs):
            acc = acc + jnp.dot(a_ref[r, :], w_ref[...], preferred_element_type=F32)
        y = _layer_norm(acc, g_ref[...], b_ref[...])
        o_ref[r, :] = y
        if slabs:
            for s in range(S):
                o2_ref[pl.ds(h * th * S + s, th, stride=S), :] = y[:, s * LANES:(s + 1) * LANES]
        else:
            o2_ref[r, :] = y.astype(BF16)


def _out_ln(a_list, wb, x, g, b, *, slabs=False, tm=512):
    T, D = x.shape
    tm = min(tm, T)
    n_in = len(a_list)
    kk = a_list[0].shape[1]
    assert all(a.shape[1] == kk for a in a_list) and wb.shape[0] == n_in * kk and T % tm == 0
    w3 = wb.reshape(n_in, kk, D)
    in_specs = [pl.BlockSpec((tm, kk), lambda i: (i, 0)) for _ in range(n_in)]
    in_specs += [pl.BlockSpec((None, kk, D), functools.partial(lambda i, c: (c, 0, 0), c=c)) for c in range(n_in)]
    in_specs += [pl.BlockSpec((tm, D), lambda i: (i, 0)),
                 pl.BlockSpec((1, D), lambda i: (0, 0)),
                 pl.BlockSpec((1, D), lambda i: (0, 0))]
    S = D // LANES
    if slabs:
        out2 = jax.ShapeDtypeStruct((T * S, LANES), F32)
        spec2 = pl.BlockSpec((tm * S, LANES), lambda i: (i, 0))
    else:
        out2 = jax.ShapeDtypeStruct((T, D), BF16)
        spec2 = pl.BlockSpec((tm, D), lambda i: (i, 0))
    return pl.pallas_call(
        functools.partial(_out_ln_kernel, n_in=n_in, slabs=slabs),
        out_shape=(jax.ShapeDtypeStruct((T, D), F32), out2),
        grid=(T // tm,),
        in_specs=in_specs,
        out_specs=(pl.BlockSpec((tm, D), lambda i: (i, 0)), spec2),
        compiler_params=_cparams(("arbitrary",)),
        name="out_ln",
    )(*a_list, *([w3] * n_in), x, g.reshape(1, D), b.reshape(1, D))


def _res_ln_kernel(x_ref, y_ref, g_ref, b_ref, o_ref, ob_ref):
    y = _layer_norm(ALPHA * x_ref[...] + y_ref[...], g_ref[...], b_ref[...])
    o_ref[...] = y
    ob_ref[...] = y.astype(BF16)


def _res_ln(x, y, g, b, tm=512):
    T, D = x.shape
    tm = min(tm, T)
    row = pl.BlockSpec((tm, D), lambda i: (i, 0))
    vec = pl.BlockSpec((1, D), lambda i: (0, 0))
    return pl.pallas_call(
        _res_ln_kernel,
        out_shape=(jax.ShapeDtypeStruct((T, D), F32), jax.ShapeDtypeStruct((T, D), BF16)),
        grid=(T // tm,),
        in_specs=[row, row, vec, vec],
        out_specs=(row, row),
        compiler_params=_cparams(("arbitrary",)),
        name="res_ln",
    )(x, y, g.reshape(1, D), b.reshape(1, D))


FFN_TM = 1024
FFN_TM_MOE = 2048
FFN_TF = 512
FFN_CHUNK = 256


def _ffn_kernel(vt_ref, ve_ref, vlo_ref, vhi_ref, vok_ref, x_ref, wg_ref, wu_ref, wd_ref, o_ref):
    v = pl.program_id(0)
    f = pl.program_id(1)
    lo = vlo_ref[v]
    hi = vhi_ref[v]

    @pl.when(jnp.logical_and(vok_ref[v] == 2, f == 0))
    def _():
        o_ref[...] = jnp.zeros(o_ref.shape, o_ref.dtype)

    def rows(c):
        return pl.ds(pl.multiple_of(c * FFN_CHUNK, FFN_CHUNK), FFN_CHUNK)

    @pl.when(f == 0)
    def _():
        def zero(c, _):
            o_ref[rows(c), :] = jnp.zeros((FFN_CHUNK, o_ref.shape[1]), o_ref.dtype)
            return 0
        lax.fori_loop(lo, hi, zero, 0)

    def swiglu(c, nchunks):
        r = pl.ds(pl.multiple_of(c * FFN_CHUNK, FFN_CHUNK), nchunks * FFN_CHUNK)
        x = x_ref[r, :]
        g = jnp.dot(x, wg_ref[...].astype(BF16), preferred_element_type=F32)
        u = jnp.dot(x, wu_ref[...].astype(BF16), preferred_element_type=F32)
        h = (g * jax.nn.sigmoid(g) * u).astype(BF16)
        o_ref[r, :] += jnp.dot(h, wd_ref[...].astype(BF16), preferred_element_type=F32)

    def pair(i, _):
        swiglu(lo + 2 * i, 2)
        return 0

    n = hi - lo
    lax.fori_loop(0, n // 2, pair, 0)

    @pl.when(n % 2 == 1)
    def _():
        swiglu(hi - 1, 1)


def _ffn(visits, xs, wg, wu, wd, *, tm):
    vt, ve, vlo, vhi, vok = visits
    R, D = xs.shape
    F = wg.shape[2]
    assert R % tm == 0 and F % FFN_TF == 0 and tm % FFN_CHUNK == 0
    nf = F // FFN_TF
    V = vt.shape[0]
    row_mode = {} if tm <= FFN_TM else dict(pipeline_mode=pl.Buffered(1))

    def fidx(v, f, vok):
        return jnp.where(vok[v] == 1, f, nf - 1)

    return pl.pallas_call(
        _ffn_kernel,
        out_shape=jax.ShapeDtypeStruct((R, D), F32),
        grid_spec=pltpu.PrefetchScalarGridSpec(
            num_scalar_prefetch=5,
            grid=(V, nf),
            in_specs=[
                pl.BlockSpec((tm, D), lambda v, f, vt, ve, vlo, vhi, vok: (vt[v], 0), **row_mode),
                pl.BlockSpec((None, D, FFN_TF), lambda v, f, vt, ve, vlo, vhi, vok: (ve[v], 0, fidx(v, f, vok))),
                pl.BlockSpec((None, D, FFN_TF), lambda v, f, vt, ve, vlo, vhi, vok: (ve[v], 0, fidx(v, f, vok))),
                pl.BlockSpec((None, FFN_TF, D), lambda v, f, vt, ve, vlo, vhi, vok: (ve[v], fidx(v, f, vok), 0)),
            ],
            out_specs=pl.BlockSpec((tm, D), lambda v, f, vt, ve, vlo, vhi, vok: (vt[v], 0), **row_mode),
        ),
        compiler_params=_cparams(("arbitrary", "arbitrary")),
        name="ffn",
    )(vt, ve, vlo, vhi, vok, xs, wg, wu, wd)


def _dense_visits(T, layer, tm):
    nt = T // tm
    cpt = tm // FFN_CHUNK
    z = jnp.zeros((nt,), I32)
    return (jnp.arange(nt, dtype=I32), z + layer, z, z + cpt, z + 1)


def _router_kernel(x_ref, r_ref, o_ref):
    logits = jnp.dot(x_ref[...], r_ref[...], preferred_element_type=F32, precision=lax.Precision.HIGHEST)
    lane = lax.broadcasted_iota(I32, logits.shape, 1)
    logits = jnp.where(lane < N_EXPERTS, logits, -jnp.inf)
    v1 = jnp.max(logits, axis=-1, keepdims=True)
    i1 = jnp.min(jnp.where(logits == v1, lane, LANES), axis=-1, keepdims=True)
    rest = jnp.where(lane == i1, -jnp.inf, logits)
    v2 = jnp.max(rest, axis=-1, keepdims=True)
    i2 = jnp.min(jnp.where(rest == v2, lane, LANES), axis=-1, keepdims=True)
    e2 = jnp.exp(v2 - v1)
    g1 = 1.0 / (1.0 + e2)
    g2 = e2 / (1.0 + e2)
    out = jnp.where(lane == 0, i1.astype(F32), 0.0)
    out = jnp.where(lane == 1, i2.astype(F32), out)
    out = jnp.where(lane == 2, g1, out)
    out = jnp.where(lane == 3, g2, out)
    o_ref[...] = out


def _router(x, router, tm=512):
    T, D = x.shape
    tm = min(tm, T)
    rp = jnp.pad(router, ((0, 0), (0, LANES - N_EXPERTS)))
    return pl.pallas_call(
        _router_kernel,
        out_shape=jax.ShapeDtypeStruct((T, LANES), F32),
        grid=(T // tm,),
        in_specs=[pl.BlockSpec((tm, D), lambda i: (i, 0)), pl.BlockSpec((D, LANES), lambda i: (0, 0))],
        out_specs=pl.BlockSpec((tm, LANES), lambda i: (i, 0)),
        compiler_params=_cparams(("arbitrary",)),
        name="router",
    )(x, rp)


GATHER_ROWS = 256


def _row_copy(src_hbm, row, dst, r, sem):
    return pltpu.make_async_copy(src_hbm.at[pl.ds(row, 1), :], dst.at[pl.ds(r, 1), :], sem)


def _prefetch_rows(issue, slot):
    i = pl.program_id(0)

    @pl.when(i == 0)
    def _():
        issue(0, 0)

    @pl.when(i + 1 < pl.num_programs(0))
    def _():
        issue(i + 1, 1 - slot)


def _gather_kernel(tok_ref, x_hbm, o_ref, buf, sem, *, S):
    slot = pl.program_id(0) % 2

    def issue(step, s):
        def body(r2, _):
            for k in range(2):
                r = 2 * r2 + k
                src = pl.ds(pl.multiple_of(tok_ref[step * GATHER_ROWS + r] * S, S), S)
                dst = pl.ds(pl.multiple_of(r * S, S), S)
                pltpu.make_async_copy(x_hbm.at[src, :], buf.at[s, dst, :], sem.at[s]).start(priority=k)
            return 0
        lax.fori_loop(0, GATHER_ROWS // 2, body, 0)

    _prefetch_rows(issue, slot)
    pltpu.make_async_copy(x_hbm.at[pl.ds(0, GATHER_ROWS * S), :], buf.at[slot], sem.at[slot]).wait()
    for s in range(S):
        o_ref[:, s * LANES:(s + 1) * LANES] = buf[slot, pl.ds(s, GATHER_ROWS, stride=S), :].astype(o_ref.dtype)


def _gather_rows(xslab, tok, D):
    R = tok.shape[0]
    S = D // LANES
    assert R % GATHER_ROWS == 0 and xslab.shape[0] >= GATHER_ROWS * S and S % 8 == 0
    return pl.pallas_call(
        functools.partial(_gather_kernel, S=S),
        out_shape=jax.ShapeDtypeStruct((R, D), BF16),
        grid_spec=pltpu.PrefetchScalarGridSpec(
            num_scalar_prefetch=1,
            grid=(R // GATHER_ROWS,),
            in_specs=[pl.BlockSpec(memory_space=pl.ANY)],
            out_specs=pl.BlockSpec((GATHER_ROWS, D), lambda i, tok: (i, 0)),
            scratch_shapes=[pltpu.VMEM((2, GATHER_ROWS * S, LANES), F32), pltpu.SemaphoreType.DMA((2,))],
        ),
        compiler_params=_cparams(("arbitrary",)),
        name="gather_rows",
    )(tok, xslab)


def _combine_kernel(p0_ref, p1_ref, y_hbm, r_ref, x_ref, g_ref, b_ref, o_ref, ob_ref, buf, sem):
    slot = pl.program_id(0) % 2

    def issue(step, s):
        def body(r, _):
            t = step * GATHER_ROWS + r
            _row_copy(y_hbm, p0_ref[t], buf.at[s, 0], r, sem.at[s]).start(priority=0)
            _row_copy(y_hbm, p1_ref[t], buf.at[s, 1], r, sem.at[s]).start(priority=1)
            return 0
        lax.fori_loop(0, GATHER_ROWS, body, 0)

    _prefetch_rows(issue, slot)
    for k in range(TOP_K):
        pltpu.make_async_copy(y_hbm.at[pl.ds(0, GATHER_ROWS), :], buf.at[slot, k], sem.at[slot]).wait()
    route = r_ref[...]
    g0 = route[:, TOP_K:TOP_K + 1]
    g1 = route[:, TOP_K + 1:TOP_K + 2]
    z = ALPHA * x_ref[...] + (g0 * buf[slot, 0] + g1 * buf[slot, 1])
    y = _layer_norm(z, g_ref[...], b_ref[...])
    o_ref[...] = y
    ob_ref[...] = y.astype(BF16)


def _combine_ln(y, p0, p1, route, x, g, b):
    T, D = x.shape
    assert T % GATHER_ROWS == 0 and y.shape[0] >= GATHER_ROWS
    row = pl.BlockSpec((GATHER_ROWS, D), lambda i, p0, p1: (i, 0))
    vec = pl.BlockSpec((1, D), lambda i, p0, p1: (0, 0))
    rspec = pl.BlockSpec((GATHER_ROWS, LANES), lambda i, p0, p1: (i, 0))
    return pl.pallas_call(
        _combine_kernel,
        out_shape=(jax.ShapeDtypeStruct((T, D), F32), jax.ShapeDtypeStruct((T, D), BF16)),
        grid_spec=pltpu.PrefetchScalarGridSpec(
            num_scalar_prefetch=2,
            grid=(T // GATHER_ROWS,),
            in_specs=[pl.BlockSpec(memory_space=pl.ANY), rspec, row, vec, vec],
            out_specs=(row, row),
            scratch_shapes=[pltpu.VMEM((2, TOP_K, GATHER_ROWS, D), F32), pltpu.SemaphoreType.DMA((2,))],
        ),
        compiler_params=_cparams(("arbitrary",)),
        name="combine_ln",
    )(p0, p1, y, route, x, g.reshape(1, D), b.reshape(1, D))


def _moe_plan(route, T, layer, tm):
    E = N_EXPERTS
    e = route[:, :TOP_K].astype(I32)
    ef = e.reshape(-1)
    onehot = (ef[:, None] == jnp.arange(E, dtype=I32)[None, :]).astype(I32)
    rank = jnp.cumsum(onehot, axis=0) - onehot
    counts = jnp.sum(onehot, axis=0)
    chunks = (counts + FFN_CHUNK - 1) // FFN_CHUNK
    seg_end = jnp.cumsum(chunks)
    seg_start = seg_end - chunks
    pos = seg_start[ef] * FFN_CHUNK + jnp.sum(rank * onehot, axis=1)
    R = -(-(TOP_K * T + E * FFN_CHUNK) // tm) * tm
    cpt = tm // FFN_CHUNK
    nt = R // tm
    n_slots = nt + E
    slot = jnp.arange(n_slots, dtype=I32)

    def place(vals, dest):
        return jnp.sum(jnp.where(dest[None, :] == slot[:, None], vals[None, :], 0), axis=1).astype(I32)

    tile_pts = jnp.arange(nt, dtype=I32) * cpt
    seg_pts = seg_start.astype(I32)
    tile_rank = jnp.arange(nt, dtype=I32) + jnp.sum(seg_pts[None, :] < tile_pts[:, None], axis=1)
    seg_rank = jnp.arange(E, dtype=I32) + jnp.sum(tile_pts[None, :] <= seg_pts[:, None], axis=1)
    pts = place(jnp.concatenate([tile_pts, seg_pts]), jnp.concatenate([tile_rank, seg_rank]))
    used = seg_end[-1]
    used_tiles_end = (used + cpt - 1) // cpt * cpt
    nxt = jnp.concatenate([pts[1:], jnp.array([nt * cpt], I32)])
    ok = (pts < used) & (nxt > pts)
    nxt = jnp.where(nxt >= used, jnp.maximum(used_tiles_end, pts), nxt)
    tile = pts // cpt
    nxt = jnp.minimum(nxt, (tile + 1) * cpt)
    exp = jnp.minimum(jnp.sum(seg_end[None, :] <= pts[:, None], axis=1), E - 1).astype(I32)
    last = jnp.max(jnp.where(ok, slot, 0))
    vt = jnp.where(ok, tile, jnp.sum(jnp.where(slot == last, tile, 0)))
    ve = jnp.where(ok, exp, jnp.sum(jnp.where(slot == last, exp, 0)))
    vlo = jnp.where(ok, pts - tile * cpt, 0)
    vhi = jnp.where(ok, nxt - tile * cpt, 0)
    oki = ok.astype(I32)
    dest = jnp.where(ok, jnp.cumsum(oki) - oki, jnp.sum(oki) + jnp.cumsum(1 - oki) - (1 - oki))
    vt, ve, vlo, vhi, vok = (place(a, dest) for a in (vt, ve, vlo, vhi, oki))
    used_tiles = used_tiles_end // cpt
    fill = slot - (n_slots - (nt - used_tiles))
    vt = jnp.where(fill >= 0, used_tiles + fill, vt).astype(I32)
    vok = jnp.where(fill >= 0, 2, vok).astype(I32)
    visits = (vt, (ve + layer * E).astype(I32), vlo, vhi, vok)
    p = pos.reshape(T, TOP_K).astype(I32)
    tok = jnp.zeros((R,), I32).at[pos].set(jnp.arange(TOP_K * T, dtype=I32) // TOP_K)
    return tok, visits, p[:, 0], p[:, 1]


DSA_TK = 512
NT_DIMS = (((1,), (1,)), ((), ()))
DSA_Q_SCALE = (HEAD_DIM ** -0.5) * LOG2_E


DSA_DIGIT = 15


def _srl(x, k):
    return lax.shift_right_logical(x, jnp.full(x.shape, k, x.dtype))


def _dsa_kernel(q_ref, iq_ref, iw_ref, ike_ref, iko_ref, kt_ref, v_ref, o_ref,
                key_s, pk_s, bias_s, s_s, p_s, m_s, l_s, al_s, acc_s, *, ksel, idx_bits):
    n = pl.program_id(0)
    TK = DSA_TK
    nkt = (n * BLOCK + BLOCK + TK - 1) // TK
    qpos = n * BLOCK + lax.broadcasted_iota(I32, (BLOCK, 1), 0)
    lane_k = lax.broadcasted_iota(I32, (1, TK), 1)

    def keys(j):
        return pl.ds(pl.multiple_of(j * TK, TK), TK)

    iqb = iq_ref[...]
    npair = IDX_HEADS // 2
    lhs = jnp.concatenate([iqb[:, p * LANES:(p + 1) * LANES] for p in range(npair)], axis=0)
    wts = iw_ref[...] * ((IDX_DIM ** -0.5) * (IDX_HEADS ** -0.5))

    def score_tile(j, _):
        le = lax.dot_general(lhs, ike_ref[keys(j), :], NT_DIMS, preferred_element_type=F32)
        lo = lax.dot_general(lhs, iko_ref[keys(j), :], NT_DIMS, preferred_element_type=F32)
        sc = jnp.zeros((BLOCK, TK), F32)
        for p in range(npair):
            sc = sc + jnp.maximum(le[p * BLOCK:(p + 1) * BLOCK], 0.0) * wts[:, 2 * p:2 * p + 1]
            sc = sc + jnp.maximum(lo[p * BLOCK:(p + 1) * BLOCK], 0.0) * wts[:, 2 * p + 1:2 * p + 2]
        bits = pltpu.bitcast(sc, I32)
        key = bits ^ ((bits >> 31) & 0x7FFFFFFF)
        key = jnp.where(j * TK + lane_k <= qpos, key, INT_MIN)
        key_s[j] = key
        pk_s[j] = pack(_srl(key ^ INT_MIN, 32 - DSA_DIGIT))
        return 0

    half = TK // 2
    guard = jnp.int32(-2147450880)
    ones2 = jnp.int32(0x00010001)

    def pack(d):
        return d[:, :half] | jnp.left_shift(d[:, half:], 16) | guard

    def packed_count(c):
        cc = c | jnp.left_shift(c, 16)

        def body(j, acc):
            hit = _srl(pk_s[j] - cc, DSA_DIGIT) & ones2
            for k in range(half // LANES):
                acc = acc + hit[:, k * LANES:(k + 1) * LANES]
            return acc
        acc = lax.fori_loop(0, nkt, body, jnp.zeros((BLOCK, LANES), I32))
        tot = (acc & 0xFFFF) + _srl(acc, 16)
        return jnp.sum(tot.astype(F32), axis=1, keepdims=True)

    lax.fori_loop(0, nkt, score_tile, 0)

    def count(pred):
        def body(j, acc):
            hit = jnp.where(pred(key_s[j], j * TK + lane_k), 1.0, 0.0)
            for c in range(TK // LANES):
                acc = acc + hit[:, c * LANES:(c + 1) * LANES]
            return acc
        acc = lax.fori_loop(0, nkt, body, jnp.zeros((BLOCK, LANES), F32))
        return jnp.sum(acc, axis=1, keepdims=True)

    searching = (n + 1) * BLOCK > ksel

    def write_bias(thr, pidx):
        def body(j, _):
            key = key_s[j]
            sel = (key > thr) | ((key == thr) & (j * TK + lane_k <= pidx))
            bias_s[j] = jnp.where(sel, 0.0, NEG)
            return 0
        lax.fori_loop(0, nkt, body, 0)

    @pl.when(jnp.logical_not(searching))
    def _():
        write_bias(jnp.full((BLOCK, 1), INT_MIN, I32), jnp.full((BLOCK, 1), -1, I32))

    @pl.when(searching)
    def _():
        def digit_search(base, cnt_thr, miss):
            def step(carry):
                i, d, cnt_thr, _ = carry
                c = d | jnp.left_shift(jnp.int32(1), DSA_DIGIT - 1 - i)
                cnt = base + packed_count(c)
                ok = cnt >= ksel
                cnt_thr = jnp.where(ok, cnt, cnt_thr)
                return i + 1, jnp.where(ok, c, d), cnt_thr, jnp.max(jnp.abs(cnt_thr - ksel))

            _, d, cnt_thr, miss = lax.while_loop(
                lambda c: jnp.logical_and(c[0] < DSA_DIGIT, c[3] > 0.0), step,
                (jnp.int32(0), jnp.zeros((BLOCK, 1), I32), cnt_thr, miss))
            return d, cnt_thr, miss

        zero = jnp.zeros((BLOCK, 1), F32)
        d_hi, cnt_thr, miss = digit_search(zero, zero - 1.0, jnp.float32(1.0))

        top = (1 << DSA_DIGIT) - 1
        above = jnp.where(d_hi == top, 0.0, packed_count(jnp.minimum(d_hi + 1, top)))

        @pl.when(miss > 0.0)
        def _():
            def body(j, _):
                u = key_s[j] ^ INT_MIN
                mid = _srl(u, 32 - 2 * DSA_DIGIT) & top
                pk_s[j] = pack(jnp.where(_srl(u, 32 - DSA_DIGIT) == d_hi, mid, 0))
                return 0
            lax.fori_loop(0, nkt, body, 0)

        d_mid, cnt_thr, miss = digit_search(above, cnt_thr, miss)
        thr = (jnp.left_shift(d_hi, 32 - DSA_DIGIT) | jnp.left_shift(d_mid, 32 - 2 * DSA_DIGIT)) ^ INT_MIN

        def bit_step(carry):
            i, thr, cnt_thr, _ = carry
            cand = thr + jnp.left_shift(jnp.int32(1), 31 - 2 * DSA_DIGIT - i)
            cnt = count(lambda key, kidx: key >= cand)
            ok = cnt >= ksel
            cnt_thr = jnp.where(ok, cnt, cnt_thr)
            return i + 1, jnp.where(ok, cand, thr), cnt_thr, jnp.max(jnp.abs(cnt_thr - ksel))

        _, thr, _, miss = lax.while_loop(
            lambda c: jnp.logical_and(c[0] < 32 - 2 * DSA_DIGIT, c[3] > 0.0), bit_step,
            (jnp.int32(0), thr, cnt_thr, miss))
        exact = miss == 0.0

        @pl.when(exact)
        def _():
            write_bias(thr, jnp.full((BLOCK, 1), 2 ** 30, I32))

        @pl.when(jnp.logical_not(exact))
        def _():
            need = ksel - count(lambda key, kidx: key > thr)

            def idx_step(i, pidx):
                cand = pidx + jnp.left_shift(jnp.int32(1), idx_bits - 1 - i)
                c = count(lambda key, kidx: (key == thr) & (kidx < cand))
                return jnp.where(c < need, cand, pidx)

            pidx = lax.fori_loop(0, idx_bits, idx_step, jnp.zeros((BLOCK, 1), I32))
            write_bias(thr, pidx)

    qb = q_ref[...]
    qs = jnp.concatenate([qb[:, h * LANES:(h + 1) * LANES] for h in range(DSA_HEADS)], axis=0)
    ncol = TK // LANES
    m_s[...] = jnp.full(m_s.shape, NEG, F32)
    l_s[...] = jnp.zeros(l_s.shape, F32)
    acc_s[...] = jnp.zeros(acc_s.shape, F32)

    def attend(j, _):
        s_s[...] = lax.dot_general(qs, kt_ref[keys(j), :], NT_DIMS, preferred_element_type=F32)
        for h in range(DSA_HEADS):
            r = slice(h * BLOCK, (h + 1) * BLOCK)
            t = [s_s[r, c * LANES:(c + 1) * LANES] + bias_s[j, :, c * LANES:(c + 1) * LANES] for c in range(ncol)]
            mx = t[0]
            for c in range(1, ncol):
                mx = jnp.maximum(mx, t[c])
            m_old = m_s[r, :]
            m_new = jnp.maximum(m_old, jnp.max(mx, axis=-1, keepdims=True))
            alpha = jnp.exp2(m_old - m_new)
            psum = jnp.zeros((BLOCK, LANES), F32)
            for c in range(ncol):
                p = jnp.exp2(t[c] - m_new)
                psum = psum + p
                p_s[r, c * LANES:(c + 1) * LANES] = p.astype(BF16)
            m_s[r, :] = m_new
            l_s[r, :] = alpha * l_s[r, :] + jnp.sum(psum, axis=-1, keepdims=True)
            al_s[r, :] = alpha
        acc_s[...] = al_s[...] * acc_s[...] + jnp.dot(p_s[...], v_ref[keys(j), :], preferred_element_type=F32)
        return 0

    lax.fori_loop(0, nkt, attend, 0)
    for h in range(DSA_HEADS):
        r = slice(h * BLOCK, (h + 1) * BLOCK)
        o_ref[:, h * LANES:(h + 1) * LANES] = (acc_s[r, :] / l_s[r, :]).astype(o_ref.dtype)


DSA_SIDE_IQ = IDX_HEADS * IDX_DIM
DSA_SIDE_COLS = DSA_SIDE_IQ + 4 * LANES


def _dsa_side_weights(w_in_l):
    o = DSA_Q
    wk = w_in_l[:, o:o + HEAD_DIM]; o += HEAD_DIM
    wv = w_in_l[:, o:o + HEAD_DIM]; o += HEAD_DIM
    wiq = w_in_l[:, o:o + DSA_SIDE_IQ]; o += DSA_SIDE_IQ
    wik = w_in_l[:, o:o + IDX_DIM]; o += IDX_DIM
    wiw = w_in_l[:, o:o + IDX_HEADS]
    z = jnp.zeros_like(wik)
    side = jnp.concatenate([wiq, wk, wv, wik, z, z, wik], axis=1)
    return side, jnp.pad(wiw, ((0, 0), (0, LANES - IDX_HEADS)))


def _dsa(q, side, iw):
    T = q.shape[0]
    TK = DSA_TK
    assert T % TK == 0 and TK % BLOCK == 0 and HEAD_DIM == LANES and 2 * IDX_DIM == LANES
    assert DSA_SIDE_IQ % LANES == 0
    ksel = min(DSA_TOPK, T // 4)
    assert ksel % BLOCK == 0
    nt = T // TK
    rows = DSA_HEADS * BLOCK
    g0 = DSA_SIDE_IQ // LANES

    def key_side(group):
        return pl.BlockSpec((T, LANES), lambda n: (0, group))

    return pl.pallas_call(
        functools.partial(_dsa_kernel, ksel=ksel, idx_bits=max(1, (T - 1).bit_length())),
        out_shape=jax.ShapeDtypeStruct((T, DSA_Q), BF16),
        grid=(T // BLOCK,),
        in_specs=[
            pl.BlockSpec((BLOCK, DSA_Q), lambda n: (n, 0)),
            pl.BlockSpec((BLOCK, DSA_SIDE_IQ), lambda n: (n, 0)),
            pl.BlockSpec((BLOCK, LANES), lambda n: (n, 0)),
            key_side(g0 + 2), key_side(g0 + 3), key_side(g0), key_side(g0 + 1),
        ],
        out_specs=pl.BlockSpec((BLOCK, DSA_Q), lambda n: (n, 0)),
        scratch_shapes=[
            pltpu.VMEM((nt, BLOCK, TK), I32),
            pltpu.VMEM((nt, BLOCK, TK // 2), I32),
            pltpu.VMEM((nt, BLOCK, TK), F32),
            pltpu.VMEM((rows, TK), F32),
            pltpu.VMEM((rows, TK), BF16),
            pltpu.VMEM((rows, LANES), F32),
            pltpu.VMEM((rows, LANES), F32),
            pltpu.VMEM((rows, LANES), F32),
            pltpu.VMEM((rows, HEAD_DIM), F32),
        ],
        compiler_params=_cparams(("arbitrary",)),
        name="dsa",
    )(q, side, iw, side, side, side, side)


def _even_layer(x, xb, rope, i, w_in, pool_w, pool_scale, w_out, ln1_g, ln1_b, wg, wu, wd, ln2_g, ln2_b):
    T = x.shape[0]
    cos2, sin2 = rope
    u = _proj(xb, w_in, i, cos2, sin2, col0=0, ncols=POOL_DIM, tn=1024, rope=(0, 0), out_dtype=F32)
    qkv = _proj(xb, w_in, i, cos2, sin2, col0=POOL_DIM, ncols=3 * DIL_QKV, tn=1024, rope=(0, 2 * DIL_QKV),
                out_dtype=F32, head_major=True)
    pooled = _pool(u, pool_w, i, pool_scale[i])
    attn = _dilated(qkv)
    x1, x1b = _out_ln([pooled, attn], w_out[i].astype(BF16), x, ln1_g[i], ln1_b[i])
    tm = min(FFN_TM, T)
    y = _ffn(_dense_visits(T, i, tm), x1b, wg, wu, wd, tm=tm)
    return _res_ln(x1, y, ln2_g[i], ln2_b[i])


def _odd_layer(x, xb, rope, i, w_in, w_out, ln1_g, ln1_b, router, wg, wu, wd, ln2_g, ln2_b):
    T, D = x.shape
    cos2, sin2 = rope
    q = _proj(xb, w_in, i, cos2, sin2, col0=0, ncols=DSA_Q, tn=1024, rope=(0, DSA_Q), out_dtype=BF16,
              out_scale=DSA_Q_SCALE)
    w_side, w_iw = _dsa_side_weights(w_in[i])
    side = _proj(xb, w_side[None], 0, cos2, sin2, col0=0, ncols=DSA_SIDE_COLS, tn=DSA_SIDE_COLS,
                 rope=(DSA_SIDE_IQ, DSA_SIDE_IQ + HEAD_DIM), out_dtype=BF16)
    iw = _proj(xb, w_iw[None], 0, cos2, sin2, col0=0, ncols=LANES, tn=LANES, rope=(0, 0), out_dtype=F32)
    attn = _dsa(q, side, iw)
    x1, x1_slabs = _out_ln([attn], w_out[i].astype(BF16), x, ln1_g[i], ln1_b[i], slabs=True)
    route = _router(x1, router[i])
    tok, visits, p0, p1 = _moe_plan(route, T, i, FFN_TM_MOE)
    xs = _gather_rows(x1_slabs, tok, D)
    F = wg.shape[3]
    y = _ffn(visits, xs, wg.reshape(-1, D, F), wu.reshape(-1, D, F), wd.reshape(-1, F, D), tm=FFN_TM_MOE)
    return _combine_ln(y, p0, p1, route, x1, ln2_g[i], ln2_b[i])


def kernel(x, ev_w_in, ev_pool_w, ev_pool_scale, ev_w_out, ev_ln1_g, ev_ln1_b, ev_ffn_w_gate, ev_ffn_w_up,
           ev_ffn_w_down, ev_ln2_g, ev_ln2_b, od_w_in, od_w_out, od_ln1_g, od_ln1_b, od_router,
           od_exp_w_gate, od_exp_w_up, od_exp_w_down, od_ln2_g, od_ln2_b):
    B, T, D = x.shape
    rope = _rope_tables(T)
    outs = []
    for bi in range(B):
        h = x[bi]
        hb = h.astype(BF16)
        for layer in range(DEPTH):
            i = layer // 2
            if layer % 2 == 0:
                h, hb = _even_layer(h, hb, rope, i, ev_w_in, ev_pool_w, ev_pool_scale, ev_w_out,
                                    ev_ln1_g, ev_ln1_b, ev_ffn_w_gate, ev_ffn_w_up,
                                    ev_ffn_w_down, ev_ln2_g, ev_ln2_b)
            else:
                h, hb = _odd_layer(h, hb, rope, i, od_w_in, od_w_out, od_ln1_g, od_ln1_b,
                                   od_router, od_exp_w_gate, od_exp_w_up, od_exp_w_down,
                                   od_ln2_g, od_ln2_b)
        outs.append(h)
    return jnp.stack(outs, axis=0)
```

```python
import functools

import jax
import jax.numpy as jnp
from jax import lax
from jax.experimental import pallas as pl
from jax.experimental.pallas import tpu as pltpu

F32 = jnp.float32
BF16 = jnp.bfloat16
I32 = jnp.int32

D_MODEL = 2048
DEPTH = 4
HEAD_DIM = 128
ROPE_THETA = 10000.0
LN_EPS = 1e-5
BLOCK = 128
ALPHA = (2 * DEPTH) ** 0.25
POOL_WINDOWS = (2, 4, 8, 16)
POOL_GROUP_DIM = D_MODEL // 8
POOL_DIM = len(POOL_WINDOWS) * POOL_GROUP_DIM
DIL_PATTERNS = ((128, 1), (512, 4), (2048, 16))
DIL_HEADS_PER_GROUP = D_MODEL // 256
DIL_HEADS = len(DIL_PATTERNS) * DIL_HEADS_PER_GROUP
DIL_QKV = DIL_HEADS * HEAD_DIM
DIL_OUT = DIL_HEADS_PER_GROUP * HEAD_DIM
DSA_HEADS = D_MODEL // HEAD_DIM
IDX_HEADS = 16
IDX_DIM = 64
DSA_TOPK = 256
DSA_Q = DSA_HEADS * HEAD_DIM
D_FF = 5632
N_EXPERTS = 8
TOP_K = 2

LANES = 128
VMEM_LIMIT = 56 * 1024 * 1024

NEG = -1e30
LOG2_E = 1.4426950408889634
INT_MIN = -(2 ** 31)


def _cparams(sem):
    return pltpu.CompilerParams(dimension_semantics=sem, vmem_limit_bytes=VMEM_LIMIT)


def _layer_norm(z, g, b):
    mu = jnp.mean(z, axis=-1, keepdims=True)
    zc = z - mu
    var = jnp.mean(zc * zc, axis=-1, keepdims=True)
    return zc * lax.rsqrt(var + LN_EPS) * g + b


PROJ_HEADS_PER_DOT = 2


def _proj_kernel(x_ref, w_ref, cos_ref, sin_ref, o_ref, wb_ref, *, rope, n_tiles, head_major, out_scale):
    @pl.when(pl.program_id(1) == 0)
    def _():
        wb_ref[...] = w_ref[...].astype(BF16)

    tn = wb_ref.shape[1]
    nh = tn // LANES
    if n_tiles == 1:
        lo, hi = rope[0] // LANES, rope[1] // LANES
        cos, sin = cos_ref[...], sin_ref[...]
    else:
        j = pl.program_id(0)
        rot = jnp.logical_and(j >= rope[0] // tn, j < rope[1] // tn)
        lo, hi = (0, nh) if rope[1] > rope[0] else (0, 0)
        cos = jnp.where(rot, cos_ref[...], 1.0)
        sin = jnp.where(rot, sin_ref[...], 0.0)

    x = x_ref[...]
    step = min(nh, PROJ_HEADS_PER_DOT)
    for c0 in range(0, nh, step):
        n = min(step, nh - c0)
        acc = jnp.dot(x, wb_ref[:, c0 * LANES:(c0 + n) * LANES], preferred_element_type=F32)
        for c in range(c0, c0 + n):
            a = acc[:, (c - c0) * LANES:(c - c0 + 1) * LANES]
            if lo <= c < hi:
                a = a * cos + pltpu.roll(a, HEAD_DIM // 2, 1) * sin
            if out_scale != 1.0:
                a = a * out_scale
            if head_major:
                o_ref[c] = a.astype(o_ref.dtype)
            else:
                o_ref[:, c * LANES:(c + 1) * LANES] = a.astype(o_ref.dtype)


def _proj(xb, w, layer, cos2, sin2, *, col0, ncols, tn, rope, out_dtype, head_major=False, out_scale=1.0,
          tm=1024):
    T, K = xb.shape
    tm = min(tm, T)
    assert T % tm == 0 and ncols % tn == 0 and col0 % tn == 0 and tn % LANES == 0
    assert all(r % (LANES if ncols == tn else tn) == 0 for r in rope)
    n_tiles = ncols // tn
    c0 = col0 // tn
    if head_major:
        out_shape = jax.ShapeDtypeStruct((ncols // LANES, T, LANES), out_dtype)
        out_spec = pl.BlockSpec((tn // LANES, tm, LANES), lambda j, i: (j, i, 0))
    else:
        out_shape = jax.ShapeDtypeStruct((T, ncols), out_dtype)
        out_spec = pl.BlockSpec((tm, tn), lambda j, i: (i, j))
    return pl.pallas_call(
        functools.partial(_proj_kernel, rope=rope, n_tiles=n_tiles, head_major=head_major,
                          out_scale=float(out_scale)),
        out_shape=out_shape,
        grid=(n_tiles, T // tm),
        in_specs=[
            pl.BlockSpec((tm, K), lambda j, i: (i, 0)),
            pl.BlockSpec((None, K, tn), lambda j, i: (layer, 0, j + c0)),
            pl.BlockSpec((tm, LANES), lambda j, i: (i, 0)),
            pl.BlockSpec((tm, LANES), lambda j, i: (i, 0)),
        ],
        out_specs=out_spec,
        scratch_shapes=[pltpu.VMEM((K, tn), BF16)],
        compiler_params=_cparams(("arbitrary", "arbitrary")),
        name="proj",
    )(xb, w, cos2, sin2)


def _rope_tables(T):
    inv_freq = ROPE_THETA ** (-jnp.arange(0, HEAD_DIM, 2, dtype=F32) / HEAD_DIM)
    ang = jnp.arange(T, dtype=F32)[:, None] * inv_freq[None, :]
    cos, sin = jnp.cos(ang), jnp.sin(ang)
    return jnp.concatenate([cos, cos], axis=-1), jnp.concatenate([-sin, sin], axis=-1)


POOL_HALO = 16


def _pool_kernel(u_ref, h_ref, w_ref, sc_ref, o_ref):
    i = pl.program_id(0)
    tm = u_ref.shape[0]
    cur = u_ref[...]
    halo = jnp.where(i > 0, h_ref[...], 0.0)
    cat = jnp.concatenate([halo, cur], axis=0)
    t = i * tm + lax.broadcasted_iota(I32, (tm, 1), 0)
    G = POOL_GROUP_DIM
    for g, w in enumerate(POOL_WINDOWS):
        s = cat[:, g * G:(g + 1) * G]
        k = 1
        while k < w:
            s = s + pltpu.roll(s, k, 0)
            k *= 2
        win = s[POOL_HALO:]
        cnt = jnp.minimum(t + 1, w).astype(F32)
        p = win / cnt - cur[:, g * G:(g + 1) * G]
        y = jnp.dot(p.astype(BF16), w_ref[g].astype(BF16), preferred_element_type=F32)
        o_ref[:, g * G:(g + 1) * G] = (y * sc_ref[:, g * G:(g + 1) * G]).astype(o_ref.dtype)


def _pool(u, pool_w, layer, pool_scale, tm=512):
    T = u.shape[0]
    tm = min(tm, T)
    assert T % tm == 0 and tm % POOL_HALO == 0 and max(POOL_WINDOWS) <= POOL_HALO
    hb = tm // POOL_HALO
    return pl.pallas_call(
        _pool_kernel,
        out_shape=jax.ShapeDtypeStruct((T, POOL_DIM), BF16),
        grid=(T // tm,),
        in_specs=[
            pl.BlockSpec((tm, POOL_DIM), lambda i: (i, 0)),
            pl.BlockSpec((POOL_HALO, POOL_DIM), lambda i: (jnp.maximum(i * hb - 1, 0), 0)),
            pl.BlockSpec((None,) + pool_w.shape[1:], lambda i: (layer, 0, 0, 0)),
            pl.BlockSpec((1, POOL_DIM), lambda i: (0, 0)),
        ],
        out_specs=pl.BlockSpec((tm, POOL_DIM), lambda i: (i, 0)),
        compiler_params=_cparams(("arbitrary",)),
        name="pool",
    )(u, u, pool_w, pool_scale.reshape(1, POOL_DIM))


DIL_SPAN = BLOCK * max(d for _, d in DIL_PATTERNS)
DIL_UNROLL = 4


def _dil_kernel(*refs):
    ng = len(DIL_PATTERNS)
    in_refs = refs[:5 * ng]
    o_ref = refs[5 * ng]
    kf_ref, vf_ref, os_ref, ms_ref, ls_ref = refs[5 * ng + 1:]
    b = pl.program_id(1)
    scale = HEAD_DIM ** -0.5

    qi = lax.broadcasted_iota(I32, (BLOCK, 2 * BLOCK), 0)
    kj = lax.broadcasted_iota(I32, (BLOCK, 2 * BLOCK), 1)
    rel = BLOCK + qi - kj
    blocks = []
    n_blocks = DIL_SPAN // BLOCK
    for g, (window, d) in enumerate(DIL_PATTERNS):
        q_ref, kc_ref, kp_ref, vc_ref, vp_ref = in_refs[5 * g:5 * g + 5]
        count = window // d
        band = (rel >= 0) & (rel <= count)
        bias_std = jnp.where(band, 0.0, NEG)
        bias_first = jnp.where(band & (kj >= BLOCK), 0.0, NEG)
        hist = BLOCK * d
        nsb = DIL_SPAN // hist
        assert d * nsb == n_blocks
        kf_ref[g, 0:hist, :] = kp_ref[...]
        kf_ref[g, hist:hist + DIL_SPAN, :] = kc_ref[...]
        vf_ref[g, 0:hist, :] = vp_ref[...]
        vf_ref[g, hist:hist + DIL_SPAN, :] = vc_ref[...]

        def block(idx, g=g, d=d, hist=hist, nsb=nsb, q_ref=q_ref, bias_std=bias_std, bias_first=bias_first):
            r = idx // nsb
            sb = idx % nsb
            start = r + hist * sb
            q = q_ref[pl.ds(start, BLOCK, stride=d), :].astype(BF16)
            k = kf_ref[g, pl.ds(start, 2 * BLOCK, stride=d), :].astype(BF16)
            v = vf_ref[g, pl.ds(start, 2 * BLOCK, stride=d), :].astype(BF16)
            s = lax.dot_general(q, k, (((1,), (1,)), ((), ())), preferred_element_type=F32) * scale
            first = jnp.logical_and(b == 0, sb == 0)
            s = s + jnp.where(first, bias_first, bias_std)
            m = jnp.max(s, axis=-1, keepdims=True)
            p = jnp.exp(s - m)
            l = jnp.sum(p, axis=-1, keepdims=True)
            o = jnp.dot(p.astype(BF16), v, preferred_element_type=F32)
            os_ref[g, pl.ds(start, BLOCK, stride=d), :] = o
            ms_ref[g, pl.ds(start, BLOCK, stride=d), :] = jnp.broadcast_to(m, (BLOCK, LANES))
            ls_ref[g, pl.ds(start, BLOCK, stride=d), :] = jnp.broadcast_to(l, (BLOCK, LANES))

        blocks.append(block)

    def body(i, _):
        for u in range(DIL_UNROLL):
            for block in blocks:
                block(i * DIL_UNROLL + u)
        return 0

    assert n_blocks % DIL_UNROLL == 0
    lax.fori_loop(0, n_blocks // DIL_UNROLL, body, 0)

    mx = ms_ref[0]
    for g in range(1, ng):
        mx = jnp.maximum(mx, ms_ref[g])
    num = jnp.zeros((DIL_SPAN, LANES), F32)
    den = jnp.zeros((DIL_SPAN, LANES), F32)
    for g in range(ng):
        e = jnp.exp(ms_ref[g] - mx)
        num = num + e * os_ref[g]
        den = den + e * ls_ref[g]
    o_ref[...] = (num / den).astype(o_ref.dtype)


def _dilated(qkv):
    T = qkv.shape[1]
    assert T % DIL_SPAN == 0 and HEAD_DIM == LANES
    H = DIL_HEADS_PER_GROUP
    in_specs, args = [], []
    max_hist = 0
    for g, (window, d) in enumerate(DIL_PATTERNS):
        assert window // d <= BLOCK and DIL_SPAN % (BLOCK * d) == 0
        hist = BLOCK * d
        max_hist = max(max_hist, hist)
        ratio = DIL_SPAN // hist

        def cur(j, b, base):
            return (base + j, b, 0)

        def prev(j, b, base, ratio=ratio):
            return (base + j, jnp.maximum(b * ratio - 1, 0), 0)

        qh, kh, vh = g * H, DIL_HEADS + g * H, 2 * DIL_HEADS + g * H
        in_specs += [
            pl.BlockSpec((None, DIL_SPAN, LANES), functools.partial(cur, base=qh)),
            pl.BlockSpec((None, DIL_SPAN, LANES), functools.partial(cur, base=kh)),
            pl.BlockSpec((None, hist, LANES), functools.partial(prev, base=kh)),
            pl.BlockSpec((None, DIL_SPAN, LANES), functools.partial(cur, base=vh)),
            pl.BlockSpec((None, hist, LANES), functools.partial(prev, base=vh)),
        ]
        args += [qkv] * 5
    ng = len(DIL_PATTERNS)
    return pl.pallas_call(
        _dil_kernel,
        out_shape=jax.ShapeDtypeStruct((T, DIL_OUT), BF16),
        grid=(H, T // DIL_SPAN),
        in_specs=in_specs,
        out_specs=pl.BlockSpec((DIL_SPAN, LANES), lambda j, b: (b, j)),
        scratch_shapes=[
            pltpu.VMEM((ng, max_hist + DIL_SPAN, LANES), F32),
            pltpu.VMEM((ng, max_hist + DIL_SPAN, LANES), F32),
            pltpu.VMEM((ng, DIL_SPAN, LANES), F32),
            pltpu.VMEM((ng, DIL_SPAN, LANES), F32),
            pltpu.VMEM((ng, DIL_SPAN, LANES), F32),
        ],
        compiler_params=_cparams(("arbitrary", "arbitrary")),
        name="dilated",
    )(*args)


OUT_LN_PARTS = 2


def _out_ln_kernel(*refs, n_in, slabs):
    a_refs = refs[:n_in]
    w_refs = refs[n_in:2 * n_in]
    x_ref, g_ref, b_ref, o_ref, o2_ref = refs[2 * n_in:]
    tm, D = x_ref.shape
    S = D // LANES
    th = tm // OUT_LN_PARTS
    for h in range(OUT_LN_PARTS):
        r = slice(h * th, (h + 1) * th)
        acc = ALPHA * x_ref[r, :]
        for a_ref, w_ref in zip(a_refs, w_refs):
            acc = acc + jnp.dot(a_ref[r, :], w_ref[...], preferred_element_type=F32)
        y = _layer_norm(acc, g_ref[...], b_ref[...])
        o_ref[r, :] = y
        if slabs:
            for s in range(S):
                o2_ref[pl.ds(h * th * S + s, th, stride=S), :] = y[:, s * LANES:(s + 1) * LANES]
        else:
            o2_ref[r, :] = y.astype(BF16)


def _out_ln(a_list, wb, x, g, b, *, slabs=False, tm=512):
    T, D = x.shape
    tm = min(tm, T)
    n_in = len(a_list)
    kk = a_list[0].shape[1]
    assert all(a.shape[1] == kk for a in a_list) and wb.shape[0] == n_in * kk and T % tm == 0
    w3 = wb.reshape(n_in, kk, D)
    in_specs = [pl.BlockSpec((tm, kk), lambda i: (i, 0)) for _ in range(n_in)]
    in_specs += [pl.BlockSpec((None, kk, D), functools.partial(lambda i, c: (c, 0, 0), c=c)) for c in range(n_in)]
    in_specs += [pl.BlockSpec((tm, D), lambda i: (i, 0)),
                 pl.BlockSpec((1, D), lambda i: (0, 0)),
                 pl.BlockSpec((1, D), lambda i: (0, 0))]
    S = D // LANES
    if slabs:
        out2 = jax.ShapeDtypeStruct((T * S, LANES), F32)
        spec2 = pl.BlockSpec((tm * S, LANES), lambda i: (i, 0))
    else:
        out2 = jax.ShapeDtypeStruct((T, D), BF16)
        spec2 = pl.BlockSpec((tm, D), lambda i: (i, 0))
    return pl.pallas_call(
        functools.partial(_out_ln_kernel, n_in=n_in, slabs=slabs),
        out_shape=(jax.ShapeDtypeStruct((T, D), F32), out2),
        grid=(T // tm,),
        in_specs=in_specs,
        out_specs=(pl.BlockSpec((tm, D), lambda i: (i, 0)), spec2),
        compiler_params=_cparams(("arbitrary",)),
        name="out_ln",
    )(*a_list, *([w3] * n_in), x, g.reshape(1, D), b.reshape(1, D))


def _res_ln_kernel(x_ref, y_ref, g_ref, b_ref, o_ref, ob_ref):
    y = _layer_norm(ALPHA * x_ref[...] + y_ref[...], g_ref[...], b_ref[...])
    o_ref[...] = y
    ob_ref[...] = y.astype(BF16)


def _res_ln(x, y, g, b, tm=512):
    T, D = x.shape
    tm = min(tm, T)
    row = pl.BlockSpec((tm, D), lambda i: (i, 0))
    vec = pl.BlockSpec((1, D), lambda i: (0, 0))
    return pl.pallas_call(
        _res_ln_kernel,
        out_shape=(jax.ShapeDtypeStruct((T, D), F32), jax.ShapeDtypeStruct((T, D), BF16)),
        grid=(T // tm,),
        in_specs=[row, row, vec, vec],
        out_specs=(row, row),
        compiler_params=_cparams(("arbitrary",)),
        name="res_ln",
    )(x, y, g.reshape(1, D), b.reshape(1, D))


FFN_TM = 1024
FFN_TM_MOE = 2048
FFN_TF = 512
FFN_CHUNK = 256


def _ffn_kernel(vt_ref, ve_ref, vlo_ref, vhi_ref, vok_ref, x_ref, wg_ref, wu_ref, wd_ref, o_ref):
    v = pl.program_id(0)
    f = pl.program_id(1)
    lo = vlo_ref[v]
    hi = vhi_ref[v]

    @pl.when(jnp.logical_and(vok_ref[v] == 2, f == 0))
    def _():
        o_ref[...] = jnp.zeros(o_ref.shape, o_ref.dtype)

    def rows(c):
        return pl.ds(pl.multiple_of(c * FFN_CHUNK, FFN_CHUNK), FFN_CHUNK)

    @pl.when(f == 0)
    def _():
        def zero(c, _):
            o_ref[rows(c), :] = jnp.zeros((FFN_CHUNK, o_ref.shape[1]), o_ref.dtype)
            return 0
        lax.fori_loop(lo, hi, zero, 0)

    def swiglu(c, nchunks):
        r = pl.ds(pl.multiple_of(c * FFN_CHUNK, FFN_CHUNK), nchunks * FFN_CHUNK)
        x = x_ref[r, :]
        g = jnp.dot(x, wg_ref[...].astype(BF16), preferred_element_type=F32)
        u = jnp.dot(x, wu_ref[...].astype(BF16), preferred_element_type=F32)
        h = (g * jax.nn.sigmoid(g) * u).astype(BF16)
        o_ref[r, :] += jnp.dot(h, wd_ref[...].astype(BF16), preferred_element_type=F32)

    def pair(i, _):
        swiglu(lo + 2 * i, 2)
        return 0

    n = hi - lo
    lax.fori_loop(0, n // 2, pair, 0)

    @pl.when(n % 2 == 1)
    def _():
        swiglu(hi - 1, 1)


def _ffn(visits, xs, wg, wu, wd, *, tm):
    vt, ve, vlo, vhi, vok = visits
    R, D = xs.shape
    F = wg.shape[2]
    assert R % tm == 0 and F % FFN_TF == 0 and tm % FFN_CHUNK == 0
    nf = F // FFN_TF
    V = vt.shape[0]
    row_mode = {} if tm <= FFN_TM else dict(pipeline_mode=pl.Buffered(1))

    def fidx(v, f, vok):
        return jnp.where(vok[v] == 1, f, nf - 1)

    return pl.pallas_call(
        _ffn_kernel,
        out_shape=jax.ShapeDtypeStruct((R, D), F32),
        grid_spec=pltpu.PrefetchScalarGridSpec(
            num_scalar_prefetch=5,
            grid=(V, nf),
            in_specs=[
                pl.BlockSpec((tm, D), lambda v, f, vt, ve, vlo, vhi, vok: (vt[v], 0), **row_mode),
                pl.BlockSpec((None, D, FFN_TF), lambda v, f, vt, ve, vlo, vhi, vok: (ve[v], 0, fidx(v, f, vok))),
                pl.BlockSpec((None, D, FFN_TF), lambda v, f, vt, ve, vlo, vhi, vok: (ve[v], 0, fidx(v, f, vok))),
                pl.BlockSpec((None, FFN_TF, D), lambda v, f, vt, ve, vlo, vhi, vok: (ve[v], fidx(v, f, vok), 0)),
            ],
            out_specs=pl.BlockSpec((tm, D), lambda v, f, vt, ve, vlo, vhi, vok: (vt[v], 0), **row_mode),
        ),
        compiler_params=_cparams(("arbitrary", "arbitrary")),
        name="ffn",
    )(vt, ve, vlo, vhi, vok, xs, wg, wu, wd)


def _dense_visits(T, layer, tm):
    nt = T // tm
    cpt = tm // FFN_CHUNK
    z = jnp.zeros((nt,), I32)
    return (jnp.arange(nt, dtype=I32), z + layer, z, z + cpt, z + 1)


def _router_kernel(x_ref, r_ref, o_ref):
    logits = jnp.dot(x_ref[...], r_ref[...], preferred_element_type=F32, precision=lax.Precision.HIGHEST)
    lane = lax.broadcasted_iota(I32, logits.shape, 1)
    logits = jnp.where(lane < N_EXPERTS, logits, -jnp.inf)
    v1 = jnp.max(logits, axis=-1, keepdims=True)
    i1 = jnp.min(jnp.where(logits == v1, lane, LANES), axis=-1, keepdims=True)
    rest = jnp.where(lane == i1, -jnp.inf, logits)
    v2 = jnp.max(rest, axis=-1, keepdims=True)
    i2 = jnp.min(jnp.where(rest == v2, lane, LANES), axis=-1, keepdims=True)
    e2 = jnp.exp(v2 - v1)
    g1 = 1.0 / (1.0 + e2)
    g2 = e2 / (1.0 + e2)
    out = jnp.where(lane == 0, i1.astype(F32), 0.0)
    out = jnp.where(lane == 1, i2.astype(F32), out)
    out = jnp.where(lane == 2, g1, out)
    out = jnp.where(lane == 3, g2, out)
    o_ref[...] = out


def _router(x, router, tm=512):
    T, D = x.shape
    tm = min(tm, T)
    rp = jnp.pad(router, ((0, 0), (0, LANES - N_EXPERTS)))
    return pl.pallas_call(
        _router_kernel,
        out_shape=jax.ShapeDtypeStruct((T, LANES), F32),
        grid=(T // tm,),
        in_specs=[pl.BlockSpec((tm, D), lambda i: (i, 0)), pl.BlockSpec((D, LANES), lambda i: (0, 0))],
        out_specs=pl.BlockSpec((tm, LANES), lambda i: (i, 0)),
        compiler_params=_cparams(("arbitrary",)),
        name="router",
    )(x, rp)


GATHER_ROWS = 256


def _row_copy(src_hbm, row, dst, r, sem):
    return pltpu.make_async_copy(src_hbm.at[pl.ds(row, 1), :], dst.at[pl.ds(r, 1), :], sem)


def _prefetch_rows(issue, slot):
    i = pl.program_id(0)

    @pl.when(i == 0)
    def _():
        issue(0, 0)

    @pl.when(i + 1 < pl.num_programs(0))
    def _():
        issue(i + 1, 1 - slot)


def _gather_kernel(tok_ref, x_hbm, o_ref, buf, sem, *, S):
    slot = pl.program_id(0) % 2

    def issue(step, s):
        def body(r2, _):
            for k in range(2):
                r = 2 * r2 + k
                src = pl.ds(pl.multiple_of(tok_ref[step * GATHER_ROWS + r] * S, S), S)
                dst = pl.ds(pl.multiple_of(r * S, S), S)
                pltpu.make_async_copy(x_hbm.at[src, :], buf.at[s, dst, :], sem.at[s]).start(priority=k)
            return 0
        lax.fori_loop(0, GATHER_ROWS // 2, body, 0)

    _prefetch_rows(issue, slot)
    pltpu.make_async_copy(x_hbm.at[pl.ds(0, GATHER_ROWS * S), :], buf.at[slot], sem.at[slot]).wait()
    for s in range(S):
        o_ref[:, s * LANES:(s + 1) * LANES] = buf[slot, pl.ds(s, GATHER_ROWS, stride=S), :].astype(o_ref.dtype)


def _gather_rows(xslab, tok, D):
    R = tok.shape[0]
    S = D // LANES
    assert R % GATHER_ROWS == 0 and xslab.shape[0] >= GATHER_ROWS * S and S % 8 == 0
    return pl.pallas_call(
        functools.partial(_gather_kernel, S=S),
        out_shape=jax.ShapeDtypeStruct((R, D), BF16),
        grid_spec=pltpu.PrefetchScalarGridSpec(
            num_scalar_prefetch=1,
            grid=(R // GATHER_ROWS,),
            in_specs=[pl.BlockSpec(memory_space=pl.ANY)],
            out_specs=pl.BlockSpec((GATHER_ROWS, D), lambda i, tok: (i, 0)),
            scratch_shapes=[pltpu.VMEM((2, GATHER_ROWS * S, LANES), F32), pltpu.SemaphoreType.DMA((2,))],
        ),
        compiler_params=_cparams(("arbitrary",)),
        name="gather_rows",
    )(tok, xslab)


def _combine_kernel(p0_ref, p1_ref, y_hbm, r_ref, x_ref, g_ref, b_ref, o_ref, ob_ref, buf, sem):
    slot = pl.program_id(0) % 2

    def issue(step, s):
        def body(r, _):
            t = step * GATHER_ROWS + r
            _row_copy(y_hbm, p0_ref[t], buf.at[s, 0], r, sem.at[s]).start(priority=0)
            _row_copy(y_hbm, p1_ref[t], buf.at[s, 1], r, sem.at[s]).start(priority=1)
            return 0
        lax.fori_loop(0, GATHER_ROWS, body, 0)

    _prefetch_rows(issue, slot)
    for k in range(TOP_K):
        pltpu.make_async_copy(y_hbm.at[pl.ds(0, GATHER_ROWS), :], buf.at[slot, k], sem.at[slot]).wait()
    route = r_ref[...]
    g0 = route[:, TOP_K:TOP_K + 1]
    g1 = route[:, TOP_K + 1:TOP_K + 2]
    z = ALPHA * x_ref[...] + (g0 * buf[slot, 0] + g1 * buf[slot, 1])
    y = _layer_norm(z, g_ref[...], b_ref[...])
    o_ref[...] = y
    ob_ref[...] = y.astype(BF16)


def _combine_ln(y, p0, p1, route, x, g, b):
    T, D = x.shape
    assert T % GATHER_ROWS == 0 and y.shape[0] >= GATHER_ROWS
    row = pl.BlockSpec((GATHER_ROWS, D), lambda i, p0, p1: (i, 0))
    vec = pl.BlockSpec((1, D), lambda i, p0, p1: (0, 0))
    rspec = pl.BlockSpec((GATHER_ROWS, LANES), lambda i, p0, p1: (i, 0))
    return pl.pallas_call(
        _combine_kernel,
        out_shape=(jax.ShapeDtypeStruct((T, D), F32), jax.ShapeDtypeStruct((T, D), BF16)),
        grid_spec=pltpu.PrefetchScalarGridSpec(
            num_scalar_prefetch=2,
            grid=(T // GATHER_ROWS,),
            in_specs=[pl.BlockSpec(memory_space=pl.ANY), rspec, row, vec, vec],
            out_specs=(row, row),
            scratch_shapes=[pltpu.VMEM((2, TOP_K, GATHER_ROWS, D), F32), pltpu.SemaphoreType.DMA((2,))],
        ),
        compiler_params=_cparams(("arbitrary",)),
        name="combine_ln",
    )(p0, p1, y, route, x, g.reshape(1, D), b.reshape(1, D))


def _moe_plan(route, T, layer, tm):
    E = N_EXPERTS
    e = route[:, :TOP_K].astype(I32)
    ef = e.reshape(-1)
    onehot = (ef[:, None] == jnp.arange(E, dtype=I32)[None, :]).astype(I32)
    rank = jnp.cumsum(onehot, axis=0) - onehot
    counts = jnp.sum(onehot, axis=0)
    chunks = (counts + FFN_CHUNK - 1) // FFN_CHUNK
    seg_end = jnp.cumsum(chunks)
    seg_start = seg_end - chunks
    pos = seg_start[ef] * FFN_CHUNK + jnp.sum(rank * onehot, axis=1)
    R = -(-(TOP_K * T + E * FFN_CHUNK) // tm) * tm
    cpt = tm // FFN_CHUNK
    nt = R // tm
    n_slots = nt + E
    slot = jnp.arange(n_slots, dtype=I32)

    def place(vals, dest):
        return jnp.sum(jnp.where(dest[None, :] == slot[:, None], vals[None, :], 0), axis=1).astype(I32)

    tile_pts = jnp.arange(nt, dtype=I32) * cpt
    seg_pts = seg_start.astype(I32)
    tile_rank = jnp.arange(nt, dtype=I32) + jnp.sum(seg_pts[None, :] < tile_pts[:, None], axis=1)
    seg_rank = jnp.arange(E, dtype=I32) + jnp.sum(tile_pts[None, :] <= seg_pts[:, None], axis=1)
    pts = place(jnp.concatenate([tile_pts, seg_pts]), jnp.concatenate([tile_rank, seg_rank]))
    used = seg_end[-1]
    used_tiles_end = (used + cpt - 1) // cpt * cpt
    nxt = jnp.concatenate([pts[1:], jnp.array([nt * cpt], I32)])
    ok = (pts < used) & (nxt > pts)
    nxt = jnp.where(nxt >= used, jnp.maximum(used_tiles_end, pts), nxt)
    tile = pts // cpt
    nxt = jnp.minimum(nxt, (tile + 1) * cpt)
    exp = jnp.minimum(jnp.sum(seg_end[None, :] <= pts[:, None], axis=1), E - 1).astype(I32)
    last = jnp.max(jnp.where(ok, slot, 0))
    vt = jnp.where(ok, tile, jnp.sum(jnp.where(slot == last, tile, 0)))
    ve = jnp.where(ok, exp, jnp.sum(jnp.where(slot == last, exp, 0)))
    vlo = jnp.where(ok, pts - tile * cpt, 0)
    vhi = jnp.where(ok, nxt - tile * cpt, 0)
    oki = ok.astype(I32)
    dest = jnp.where(ok, jnp.cumsum(oki) - oki, jnp.sum(oki) + jnp.cumsum(1 - oki) - (1 - oki))
    vt, ve, vlo, vhi, vok = (place(a, dest) for a in (vt, ve, vlo, vhi, oki))
    used_tiles = used_tiles_end // cpt
    fill = slot - (n_slots - (nt - used_tiles))
    vt = jnp.where(fill >= 0, used_tiles + fill, vt).astype(I32)
    vok = jnp.where(fill >= 0, 2, vok).astype(I32)
    visits = (vt, (ve + layer * E).astype(I32), vlo, vhi, vok)
    p = pos.reshape(T, TOP_K).astype(I32)
    tok = jnp.zeros((R,), I32).at[pos].set(jnp.arange(TOP_K * T, dtype=I32) // TOP_K)
    return tok, visits, p[:, 0], p[:, 1]


DSA_TK = 512
NT_DIMS = (((1,), (1,)), ((), ()))
DSA_Q_SCALE = (HEAD_DIM ** -0.5) * LOG2_E


DSA_DIGIT = 15


def _srl(x, k):
    return lax.shift_right_logical(x, jnp.full(x.shape, k, x.dtype))


def _dsa_kernel(q_ref, iq_ref, iw_ref, ike_ref, iko_ref, kt_ref, v_ref, o_ref,
                key_s, pk_s, bias_s, s_s, p_s, m_s, l_s, al_s, acc_s, *, ksel, idx_bits):
    n = pl.program_id(0)
    TK = DSA_TK
    nkt = (n * BLOCK + BLOCK + TK - 1) // TK
    qpos = n * BLOCK + lax.broadcasted_iota(I32, (BLOCK, 1), 0)
    lane_k = lax.broadcasted_iota(I32, (1, TK), 1)

    def keys(j):
        return pl.ds(pl.multiple_of(j * TK, TK), TK)

    iqb = iq_ref[...]
    npair = IDX_HEADS // 2
    lhs = jnp.concatenate([iqb[:, p * LANES:(p + 1) * LANES] for p in range(npair)], axis=0)
    wts = iw_ref[...] * ((IDX_DIM ** -0.5) * (IDX_HEADS ** -0.5))

    def score_tile(j, _):
        digits = []
        for part in range(2):
            kr = pl.ds(pl.multiple_of(j * TK + part * half, half), half)
            le = lax.dot_general(lhs, ike_ref[kr, :], NT_DIMS, preferred_element_type=F32)
            lo = lax.dot_general(lhs, iko_ref[kr, :], NT_DIMS, preferred_element_type=F32)
            sc = jnp.zeros((BLOCK, half), F32)
            for p in range(npair):
                sc = sc + jnp.maximum(le[p * BLOCK:(p + 1) * BLOCK], 0.0) * wts[:, 2 * p:2 * p + 1]
                sc = sc + jnp.maximum(lo[p * BLOCK:(p + 1) * BLOCK], 0.0) * wts[:, 2 * p + 1:2 * p + 2]
            bits = pltpu.bitcast(sc, I32)
            key = bits ^ ((bits >> 31) & 0x7FFFFFFF)
            key = jnp.where(j * TK + part * half + lane_k[:, :half] <= qpos, key, INT_MIN)
            key_s[j, :, part * half:(part + 1) * half] = key
            digits.append(_srl(key ^ INT_MIN, 32 - DSA_DIGIT))
        pk_s[j] = digits[0] | jnp.left_shift(digits[1], 16) | guard
        return 0

    half = TK // 2
    guard = jnp.int32(-2147450880)
    ones2 = jnp.int32(0x00010001)

    def pack(d):
        return d[:, :half] | jnp.left_shift(d[:, half:], 16) | guard

    def packed_count(c):
        cc = c | jnp.left_shift(c, 16)

        def body(j, acc):
            hit = _srl(pk_s[j] - cc, DSA_DIGIT) & ones2
            for k in range(half // LANES):
                acc = acc + hit[:, k * LANES:(k + 1) * LANES]
            return acc
        acc = lax.fori_loop(0, nkt, body, jnp.zeros((BLOCK, LANES), I32))
        tot = (acc & 0xFFFF) + _srl(acc, 16)
        return jnp.sum(tot.astype(F32), axis=1, keepdims=True)

    lax.fori_loop(0, nkt, score_tile, 0)

    def count(pred):
        def body(j, acc):
            hit = jnp.where(pred(key_s[j], j * TK + lane_k), 1.0, 0.0)
            for c in range(TK // LANES):
                acc = acc + hit[:, c * LANES:(c + 1) * LANES]
            return acc
        acc = lax.fori_loop(0, nkt, body, jnp.zeros((BLOCK, LANES), F32))
        return jnp.sum(acc, axis=1, keepdims=True)

    searching = (n + 1) * BLOCK > ksel

    def write_bias(thr, pidx):
        def body(j, _):
            key = key_s[j]
            sel = (key > thr) | ((key == thr) & (j * TK + lane_k <= pidx))
            bias_s[j] = jnp.where(sel, 0.0, NEG)
            return 0
        lax.fori_loop(0, nkt, body, 0)

    @pl.when(jnp.logical_not(searching))
    def _():
        write_bias(jnp.full((BLOCK, 1), INT_MIN, I32), jnp.full((BLOCK, 1), -1, I32))

    @pl.when(searching)
    def _():
        def digit_search(base, cnt_thr, miss):
            def step(carry):
                i, d, cnt_thr, _ = carry
                c = d | jnp.left_shift(jnp.int32(1), DSA_DIGIT - 1 - i)
                cnt = base + packed_count(c)
                ok = cnt >= ksel
                cnt_thr = jnp.where(ok, cnt, cnt_thr)
                return i + 1, jnp.where(ok, c, d), cnt_thr, jnp.max(jnp.abs(cnt_thr - ksel))

            _, d, cnt_thr, miss = lax.while_loop(
                lambda c: jnp.logical_and(c[0] < DSA_DIGIT, c[3] > 0.0), step,
                (jnp.int32(0), jnp.zeros((BLOCK, 1), I32), cnt_thr, miss))
            return d, cnt_thr, miss

        zero = jnp.zeros((BLOCK, 1), F32)
        d_hi, cnt_thr, miss = digit_search(zero, zero - 1.0, jnp.float32(1.0))

        top = (1 << DSA_DIGIT) - 1
        above = jnp.where(d_hi == top, 0.0, packed_count(jnp.minimum(d_hi + 1, top)))

        @pl.when(miss > 0.0)
        def _():
            def body(j, _):
                u = key_s[j] ^ INT_MIN
                mid = _srl(u, 32 - 2 * DSA_DIGIT) & top
                pk_s[j] = pack(jnp.where(_srl(u, 32 - DSA_DIGIT) == d_hi, mid, 0))
                return 0
            lax.fori_loop(0, nkt, body, 0)

        d_mid, cnt_thr, miss = digit_search(above, cnt_thr, miss)
        thr = (jnp.left_shift(d_hi, 32 - DSA_DIGIT) | jnp.left_shift(d_mid, 32 - 2 * DSA_DIGIT)) ^ INT_MIN

        def bit_step(carry):
            i, thr, cnt_thr, _ = carry
            cand = thr + jnp.left_shift(jnp.int32(1), 31 - 2 * DSA_DIGIT - i)
            cnt = count(lambda key, kidx: key >= cand)
            ok = cnt >= ksel
            cnt_thr = jnp.where(ok, cnt, cnt_thr)
            return i + 1, jnp.where(ok, cand, thr), cnt_thr, jnp.max(jnp.abs(cnt_thr - ksel))

        _, thr, _, miss = lax.while_loop(
            lambda c: jnp.logical_and(c[0] < 32 - 2 * DSA_DIGIT, c[3] > 0.0), bit_step,
            (jnp.int32(0), thr, cnt_thr, miss))
        exact = miss == 0.0

        @pl.when(exact)
        def _():
            write_bias(thr, jnp.full((BLOCK, 1), 2 ** 30, I32))

        @pl.when(jnp.logical_not(exact))
        def _():
            need = ksel - count(lambda key, kidx: key > thr)

            def idx_step(i, pidx):
                cand = pidx + jnp.left_shift(jnp.int32(1), idx_bits - 1 - i)
                c = count(lambda key, kidx: (key == thr) & (kidx < cand))
                return jnp.where(c < need, cand, pidx)

            pidx = lax.fori_loop(0, idx_bits, idx_step, jnp.zeros((BLOCK, 1), I32))
            write_bias(thr, pidx)

    qb = q_ref[...]
    qs = jnp.concatenate([qb[:, h * LANES:(h + 1) * LANES] for h in range(DSA_HEADS)], axis=0)
    ncol = TK // LANES
    m_s[...] = jnp.full(m_s.shape, NEG, F32)
    l_s[...] = jnp.zeros(l_s.shape, F32)
    acc_s[...] = jnp.zeros(acc_s.shape, F32)

    def attend(j, _):
        s_s[...] = lax.dot_general(qs, kt_ref[keys(j), :], NT_DIMS, preferred_element_type=F32)
        for h in range(DSA_HEADS):
            r = slice(h * BLOCK, (h + 1) * BLOCK)
            t = [s_s[r, c * LANES:(c + 1) * LANES] + bias_s[j, :, c * LANES:(c + 1) * LANES] for c in range(ncol)]
            mx = t[0]
            for c in range(1, ncol):
                mx = jnp.maximum(mx, t[c])
            m_old = m_s[r, :]
            m_new = jnp.maximum(m_old, jnp.max(mx, axis=-1, keepdims=True))
            alpha = jnp.exp2(m_old - m_new)
            psum = jnp.zeros((BLOCK, LANES), F32)
            for c in range(ncol):
                p = jnp.exp2(t[c] - m_new)
                psum = psum + p
                p_s[r, c * LANES:(c + 1) * LANES] = p.astype(BF16)
            m_s[r, :] = m_new
            l_s[r, :] = alpha * l_s[r, :] + jnp.sum(psum, axis=-1, keepdims=True)
            al_s[r, :] = alpha
        acc_s[...] = al_s[...] * acc_s[...] + jnp.dot(p_s[...], v_ref[keys(j), :], preferred_element_type=F32)
        return 0

    lax.fori_loop(0, nkt, attend, 0)
    for h in range(DSA_HEADS):
        r = slice(h * BLOCK, (h + 1) * BLOCK)
        o_ref[:, h * LANES:(h + 1) * LANES] = (acc_s[r, :] / l_s[r, :]).astype(o_ref.dtype)


DSA_SIDE_IQ = IDX_HEADS * IDX_DIM
DSA_SIDE_COLS = DSA_SIDE_IQ + 4 * LANES


def _dsa_side_weights(w_in_l):
    o = DSA_Q
    wk = w_in_l[:, o:o + HEAD_DIM]; o += HEAD_DIM
    wv = w_in_l[:, o:o + HEAD_DIM]; o += HEAD_DIM
    wiq = w_in_l[:, o:o + DSA_SIDE_IQ]; o += DSA_SIDE_IQ
    wik = w_in_l[:, o:o + IDX_DIM]; o += IDX_DIM
    wiw = w_in_l[:, o:o + IDX_HEADS]
    z = jnp.zeros_like(wik)
    side = jnp.concatenate([wiq, wk, wv, wik, z, z, wik], axis=1)
    return side, jnp.pad(wiw, ((0, 0), (0, LANES - IDX_HEADS)))


def _dsa(q, side, iw):
    T = q.shape[0]
    TK = DSA_TK
    assert T % TK == 0 and TK % BLOCK == 0 and HEAD_DIM == LANES and 2 * IDX_DIM == LANES
    assert DSA_SIDE_IQ % LANES == 0
    ksel = min(DSA_TOPK, T // 4)
    assert ksel % BLOCK == 0
    nt = T // TK
    rows = DSA_HEADS * BLOCK
    g0 = DSA_SIDE_IQ // LANES

    def key_side(group):
        return pl.BlockSpec((T, LANES), lambda n: (0, group))

    return pl.pallas_call(
        functools.partial(_dsa_kernel, ksel=ksel, idx_bits=max(1, (T - 1).bit_length())),
        out_shape=jax.ShapeDtypeStruct((T, DSA_Q), BF16),
        grid=(T // BLOCK,),
        in_specs=[
            pl.BlockSpec((BLOCK, DSA_Q), lambda n: (n, 0)),
            pl.BlockSpec((BLOCK, DSA_SIDE_IQ), lambda n: (n, 0)),
            pl.BlockSpec((BLOCK, LANES), lambda n: (n, 0)),
            key_side(g0 + 2), key_side(g0 + 3), key_side(g0), key_side(g0 + 1),
        ],
        out_specs=pl.BlockSpec((BLOCK, DSA_Q), lambda n: (n, 0)),
        scratch_shapes=[
            pltpu.VMEM((nt, BLOCK, TK), I32),
            pltpu.VMEM((nt, BLOCK, TK // 2), I32),
            pltpu.VMEM((nt, BLOCK, TK), F32),
            pltpu.VMEM((rows, TK), F32),
            pltpu.VMEM((rows, TK), BF16),
            pltpu.VMEM((rows, LANES), F32),
            pltpu.VMEM((rows, LANES), F32),
            pltpu.VMEM((rows, LANES), F32),
            pltpu.VMEM((rows, HEAD_DIM), F32),
        ],
        compiler_params=_cparams(("arbitrary",)),
        name="dsa",
    )(q, side, iw, side, side, side, side)


def _even_layer(x, xb, rope, i, w_in, pool_w, pool_scale, w_out, ln1_g, ln1_b, wg, wu, wd, ln2_g, ln2_b):
    T = x.shape[0]
    cos2, sin2 = rope
    u = _proj(xb, w_in, i, cos2, sin2, col0=0, ncols=POOL_DIM, tn=1024, rope=(0, 0), out_dtype=F32)
    qkv = _proj(xb, w_in, i, cos2, sin2, col0=POOL_DIM, ncols=3 * DIL_QKV, tn=1024, rope=(0, 2 * DIL_QKV),
                out_dtype=F32, head_major=True)
    pooled = _pool(u, pool_w, i, pool_scale[i])
    attn = _dilated(qkv)
    x1, x1b = _out_ln([pooled, attn], w_out[i].astype(BF16), x, ln1_g[i], ln1_b[i])
    tm = min(FFN_TM, T)
    y = _ffn(_dense_visits(T, i, tm), x1b, wg, wu, wd, tm=tm)
    return _res_ln(x1, y, ln2_g[i], ln2_b[i])


def _odd_layer(x, xb, rope, i, w_in, w_out, ln1_g, ln1_b, router, wg, wu, wd, ln2_g, ln2_b):
    T, D = x.shape
    cos2, sin2 = rope
    q = _proj(xb, w_in, i, cos2, sin2, col0=0, ncols=DSA_Q, tn=1024, rope=(0, DSA_Q), out_dtype=BF16,
              out_scale=DSA_Q_SCALE)
    w_side, w_iw = _dsa_side_weights(w_in[i])
    side = _proj(xb, w_side[None], 0, cos2, sin2, col0=0, ncols=DSA_SIDE_COLS, tn=DSA_SIDE_COLS,
                 rope=(DSA_SIDE_IQ, DSA_SIDE_IQ + HEAD_DIM), out_dtype=BF16)
    iw = _proj(xb, w_iw[None], 0, cos2, sin2, col0=0, ncols=LANES, tn=LANES, rope=(0, 0), out_dtype=F32)
    attn = _dsa(q, side, iw)
    x1, x1_slabs = _out_ln([attn], w_out[i].astype(BF16), x, ln1_g[i], ln1_b[i], slabs=True)
    route = _router(x1, router[i])
    tok, visits, p0, p1 = _moe_plan(route, T, i, FFN_TM_MOE)
    xs = _gather_rows(x1_slabs, tok, D)
    F = wg.shape[3]
    y = _ffn(visits, xs, wg.reshape(-1, D, F), wu.reshape(-1, D, F), wd.reshape(-1, F, D), tm=FFN_TM_MOE)
    return _combine_ln(y, p0, p1, route, x1, ln2_g[i], ln2_b[i])


def kernel(x, ev_w_in, ev_pool_w, ev_pool_scale, ev_w_out, ev_ln1_g, ev_ln1_b, ev_ffn_w_gate, ev_ffn_w_up,
           ev_ffn_w_down, ev_ln2_g, ev_ln2_b, od_w_in, od_w_out, od_ln1_g, od_ln1_b, od_router,
           od_exp_w_gate, od_exp_w_up, od_exp_w_down, od_ln2_g, od_ln2_b):
    B, T, D = x.shape
    rope = _rope_tables(T)
    outs = []
    for bi in range(B):
        h = x[bi]
        hb = h.astype(BF16)
        for layer in range(DEPTH):
            i = layer // 2
            if layer % 2 == 0:
                h, hb = _even_layer(h, hb, rope, i, ev_w_in, ev_pool_w, ev_pool_scale, ev_w_out,
                                    ev_ln1_g, ev_ln1_b, ev_ffn_w_gate, ev_ffn_w_up,
                                    ev_ffn_w_down, ev_ln2_g, ev_ln2_b)
            else:
                h, hb = _odd_layer(h, hb, rope, i, od_w_in, od_w_out, od_ln1_g, od_ln1_b,
                                   od_router, od_exp_w_gate, od_exp_w_up, od_exp_w_down,
                                   od_ln2_g, od_ln2_b)
        outs.append(h)
    return jnp.stack(outs, axis=0)
```

```python
import functools

import jax
import jax.numpy as jnp
from jax import lax
from jax.experimental import pallas as pl
from jax.experimental.pallas import tpu as pltpu

F32 = jnp.float32
BF16 = jnp.bfloat16
I32 = jnp.int32

D_MODEL = 2048
DEPTH = 4
HEAD_DIM = 128
ROPE_THETA = 10000.0
LN_EPS = 1e-5
BLOCK = 128
ALPHA = (2 * DEPTH) ** 0.25
POOL_WINDOWS = (2, 4, 8, 16)
POOL_GROUP_DIM = D_MODEL // 8
POOL_DIM = len(POOL_WINDOWS) * POOL_GROUP_DIM
DIL_PATTERNS = ((128, 1), (512, 4), (2048, 16))
DIL_HEADS_PER_GROUP = D_MODEL // 256
DIL_HEADS = len(DIL_PATTERNS) * DIL_HEADS_PER_GROUP
DIL_QKV = DIL_HEADS * HEAD_DIM
DIL_OUT = DIL_HEADS_PER_GROUP * HEAD_DIM
DSA_HEADS = D_MODEL // HEAD_DIM
IDX_HEADS = 16
IDX_DIM = 64
DSA_TOPK = 256
DSA_Q = DSA_HEADS * HEAD_DIM
D_FF = 5632
N_EXPERTS = 8
TOP_K = 2

LANES = 128
VMEM_LIMIT = 56 * 1024 * 1024

NEG = -1e30
LOG2_E = 1.4426950408889634
INT_MIN = -(2 ** 31)


def _cparams(sem):
    return pltpu.CompilerParams(dimension_semantics=sem, vmem_limit_bytes=VMEM_LIMIT)


def _layer_norm(z, g, b):
    mu = jnp.mean(z, axis=-1, keepdims=True)
    zc = z - mu
    var = jnp.mean(zc * zc, axis=-1, keepdims=True)
    return zc * lax.rsqrt(var + LN_EPS) * g + b


PROJ_HEADS_PER_DOT = 2


def _proj_kernel(x_ref, w_ref, cos_ref, sin_ref, o_ref, wb_ref, *, rope, n_tiles, head_major, out_scale):
    @pl.when(pl.program_id(1) == 0)
    def _():
        wb_ref[...] = w_ref[...].astype(BF16)

    tn = wb_ref.shape[1]
    nh = tn // LANES
    if n_tiles == 1:
        lo, hi = rope[0] // LANES, rope[1] // LANES
        cos, sin = cos_ref[...], sin_ref[...]
    else:
        j = pl.program_id(0)
        rot = jnp.logical_and(j >= rope[0] // tn, j < rope[1] // tn)
        lo, hi = (0, nh) if rope[1] > rope[0] else (0, 0)
        cos = jnp.where(rot, cos_ref[...], 1.0)
        sin = jnp.where(rot, sin_ref[...], 0.0)

    x = x_ref[...]
    step = min(nh, PROJ_HEADS_PER_DOT)
    for c0 in range(0, nh, step):
        n = min(step, nh - c0)
        acc = jnp.dot(x, wb_ref[:, c0 * LANES:(c0 + n) * LANES], preferred_element_type=F32)
        for c in range(c0, c0 + n):
            a = acc[:, (c - c0) * LANES:(c - c0 + 1) * LANES]
            if lo <= c < hi:
                a = a * cos + pltpu.roll(a, HEAD_DIM // 2, 1) * sin
            if out_scale != 1.0:
                a = a * out_scale
            if head_major:
                o_ref[c] = a.astype(o_ref.dtype)
            else:
                o_ref[:, c * LANES:(c + 1) * LANES] = a.astype(o_ref.dtype)


def _proj(xb, w, layer, cos2, sin2, *, col0, ncols, tn, rope, out_dtype, head_major=False, out_scale=1.0,
          tm=1024):
    T, K = xb.shape
    tm = min(tm, T)
    assert T % tm == 0 and ncols % tn == 0 and col0 % tn == 0 and tn % LANES == 0
    assert all(r % (LANES if ncols == tn else tn) == 0 for r in rope)
    n_tiles = ncols // tn
    c0 = col0 // tn
    if head_major:
        out_shape = jax.ShapeDtypeStruct((ncols // LANES, T, LANES), out_dtype)
        out_spec = pl.BlockSpec((tn // LANES, tm, LANES), lambda j, i: (j, i, 0))
    else:
        out_shape = jax.ShapeDtypeStruct((T, ncols), out_dtype)
        out_spec = pl.BlockSpec((tm, tn), lambda j, i: (i, j))
    return pl.pallas_call(
        functools.partial(_proj_kernel, rope=rope, n_tiles=n_tiles, head_major=head_major,
                          out_scale=float(out_scale)),
        out_shape=out_shape,
        grid=(n_tiles, T // tm),
        in_specs=[
            pl.BlockSpec((tm, K), lambda j, i: (i, 0)),
            pl.BlockSpec((None, K, tn), lambda j, i: (layer, 0, j + c0)),
            pl.BlockSpec((tm, LANES), lambda j, i: (i, 0)),
            pl.BlockSpec((tm, LANES), lambda j, i: (i, 0)),
        ],
        out_specs=out_spec,
        scratch_shapes=[pltpu.VMEM((K, tn), BF16)],
        compiler_params=_cparams(("arbitrary", "arbitrary")),
        name="proj",
    )(xb, w, cos2, sin2)


def _rope_tables(T):
    inv_freq = ROPE_THETA ** (-jnp.arange(0, HEAD_DIM, 2, dtype=F32) / HEAD_DIM)
    ang = jnp.arange(T, dtype=F32)[:, None] * inv_freq[None, :]
    cos, sin = jnp.cos(ang), jnp.sin(ang)
    return jnp.concatenate([cos, cos], axis=-1), jnp.concatenate([-sin, sin], axis=-1)


POOL_HALO = 16


def _pool_kernel(u_ref, h_ref, w_ref, sc_ref, o_ref):
    i = pl.program_id(0)
    tm = u_ref.shape[0]
    cur = u_ref[...]
    halo = jnp.where(i > 0, h_ref[...], 0.0)
    cat = jnp.concatenate([halo, cur], axis=0)
    t = i * tm + lax.broadcasted_iota(I32, (tm, 1), 0)
    G = POOL_GROUP_DIM
    for g, w in enumerate(POOL_WINDOWS):
        s = cat[:, g * G:(g + 1) * G]
        k = 1
        while k < w:
            s = s + pltpu.roll(s, k, 0)
            k *= 2
        win = s[POOL_HALO:]
        cnt = jnp.minimum(t + 1, w).astype(F32)
        p = win / cnt - cur[:, g * G:(g + 1) * G]
        y = jnp.dot(p.astype(BF16), w_ref[g].astype(BF16), preferred_element_type=F32)
        o_ref[:, g * G:(g + 1) * G] = (y * sc_ref[:, g * G:(g + 1) * G]).astype(o_ref.dtype)


def _pool(u, pool_w, layer, pool_scale, tm=512):
    T = u.shape[0]
    tm = min(tm, T)
    assert T % tm == 0 and tm % POOL_HALO == 0 and max(POOL_WINDOWS) <= POOL_HALO
    hb = tm // POOL_HALO
    return pl.pallas_call(
        _pool_kernel,
        out_shape=jax.ShapeDtypeStruct((T, POOL_DIM), BF16),
        grid=(T // tm,),
        in_specs=[
            pl.BlockSpec((tm, POOL_DIM), lambda i: (i, 0)),
            pl.BlockSpec((POOL_HALO, POOL_DIM), lambda i: (jnp.maximum(i * hb - 1, 0), 0)),
            pl.BlockSpec((None,) + pool_w.shape[1:], lambda i: (layer, 0, 0, 0)),
            pl.BlockSpec((1, POOL_DIM), lambda i: (0, 0)),
        ],
        out_specs=pl.BlockSpec((tm, POOL_DIM), lambda i: (i, 0)),
        compiler_params=_cparams(("arbitrary",)),
        name="pool",
    )(u, u, pool_w, pool_scale.reshape(1, POOL_DIM))


DIL_SPAN = BLOCK * max(d for _, d in DIL_PATTERNS)
DIL_UNROLL = 4


def _dil_kernel(*refs):
    ng = len(DIL_PATTERNS)
    in_refs = refs[:5 * ng]
    o_ref = refs[5 * ng]
    kf_ref, vf_ref, os_ref, ms_ref, ls_ref = refs[5 * ng + 1:]
    b = pl.program_id(1)
    scale = HEAD_DIM ** -0.5

    qi = lax.broadcasted_iota(I32, (BLOCK, 2 * BLOCK), 0)
    kj = lax.broadcasted_iota(I32, (BLOCK, 2 * BLOCK), 1)
    rel = BLOCK + qi - kj
    blocks = []
    n_blocks = DIL_SPAN // BLOCK
    for g, (window, d) in enumerate(DIL_PATTERNS):
        q_ref, kc_ref, kp_ref, vc_ref, vp_ref = in_refs[5 * g:5 * g + 5]
        count = window // d
        band = (rel >= 0) & (rel <= count)
        bias_std = jnp.where(band, 0.0, NEG)
        bias_first = jnp.where(band & (kj >= BLOCK), 0.0, NEG)
        hist = BLOCK * d
        nsb = DIL_SPAN // hist
        assert d * nsb == n_blocks
        kf_ref[g, 0:hist, :] = kp_ref[...]
        kf_ref[g, hist:hist + DIL_SPAN, :] = kc_ref[...]
        vf_ref[g, 0:hist, :] = vp_ref[...]
        vf_ref[g, hist:hist + DIL_SPAN, :] = vc_ref[...]

        def block(idx, g=g, d=d, hist=hist, nsb=nsb, q_ref=q_ref, bias_std=bias_std, bias_first=bias_first):
            r = idx // nsb
            sb = idx % nsb
            start = r + hist * sb
            q = q_ref[pl.ds(start, BLOCK, stride=d), :].astype(BF16)
            k = kf_ref[g, pl.ds(start, 2 * BLOCK, stride=d), :].astype(BF16)
            v = vf_ref[g, pl.ds(start, 2 * BLOCK, stride=d), :].astype(BF16)
            s = lax.dot_general(q, k, (((1,), (1,)), ((), ())), preferred_element_type=F32) * scale
            first = jnp.logical_and(b == 0, sb == 0)
            s = s + jnp.where(first, bias_first, bias_std)
            m = jnp.max(s, axis=-1, keepdims=True)
            p = jnp.exp(s - m)
            l = jnp.sum(p, axis=-1, keepdims=True)
            o = jnp.dot(p.astype(BF16), v, preferred_element_type=F32)
            os_ref[g, pl.ds(start, BLOCK, stride=d), :] = o
            ms_ref[g, pl.ds(start, BLOCK, stride=d), :] = jnp.broadcast_to(m, (BLOCK, LANES))
            ls_ref[g, pl.ds(start, BLOCK, stride=d), :] = jnp.broadcast_to(l, (BLOCK, LANES))

        blocks.append(block)

    def body(i, _):
        for u in range(DIL_UNROLL):
            for block in blocks:
                block(i * DIL_UNROLL + u)
        return 0

    assert n_blocks % DIL_UNROLL == 0
    lax.fori_loop(0, n_blocks // DIL_UNROLL, body, 0)

    mx = ms_ref[0]
    for g in range(1, ng):
        mx = jnp.maximum(mx, ms_ref[g])
    num = jnp.zeros((DIL_SPAN, LANES), F32)
    den = jnp.zeros((DIL_SPAN, LANES), F32)
    for g in range(ng):
        e = jnp.exp(ms_ref[g] - mx)
        num = num + e * os_ref[g]
        den = den + e * ls_ref[g]
    o_ref[...] = (num / den).astype(o_ref.dtype)


def _dilated(qkv):
    T = qkv.shape[1]
    assert T % DIL_SPAN == 0 and HEAD_DIM == LANES
    H = DIL_HEADS_PER_GROUP
    in_specs, args = [], []
    max_hist = 0
    for g, (window, d) in enumerate(DIL_PATTERNS):
        assert window // d <= BLOCK and DIL_SPAN % (BLOCK * d) == 0
        hist = BLOCK * d
        max_hist = max(max_hist, hist)
        ratio = DIL_SPAN // hist

        def cur(j, b, base):
            return (base + j, b, 0)

        def prev(j, b, base, ratio=ratio):
            return (base + j, jnp.maximum(b * ratio - 1, 0), 0)

        qh, kh, vh = g * H, DIL_HEADS + g * H, 2 * DIL_HEADS + g * H
        in_specs += [
            pl.BlockSpec((None, DIL_SPAN, LANES), functools.partial(cur, base=qh)),
            pl.BlockSpec((None, DIL_SPAN, LANES), functools.partial(cur, base=kh)),
            pl.BlockSpec((None, hist, LANES), functools.partial(prev, base=kh)),
            pl.BlockSpec((None, DIL_SPAN, LANES), functools.partial(cur, base=vh)),
            pl.BlockSpec((None, hist, LANES), functools.partial(prev, base=vh)),
        ]
        args += [qkv] * 5
    ng = len(DIL_PATTERNS)
    return pl.pallas_call(
        _dil_kernel,
        out_shape=jax.ShapeDtypeStruct((T, DIL_OUT), BF16),
        grid=(H, T // DIL_SPAN),
        in_specs=in_specs,
        out_specs=pl.BlockSpec((DIL_SPAN, LANES), lambda j, b: (b, j)),
        scratch_shapes=[
            pltpu.VMEM((ng, max_hist + DIL_SPAN, LANES), F32),
            pltpu.VMEM((ng, max_hist + DIL_SPAN, LANES), F32),
            pltpu.VMEM((ng, DIL_SPAN, LANES), F32),
            pltpu.VMEM((ng, DIL_SPAN, LANES), F32),
            pltpu.VMEM((ng, DIL_SPAN, LANES), F32),
        ],
        compiler_params=_cparams(("arbitrary", "arbitrary")),
        name="dilated",
    )(*args)


OUT_LN_PARTS = 2


def _out_ln_kernel(*refs, n_in, slabs):
    a_refs = refs[:n_in]
    w_refs = refs[n_in:2 * n_in]
    x_ref, g_ref, b_ref, o_ref, o2_ref = refs[2 * n_in:]
    tm, D = x_ref.shape
    S = D // LANES
    th = tm // OUT_LN_PARTS
    for h in range(OUT_LN_PARTS):
        r = slice(h * th, (h + 1) * th)
        acc = ALPHA * x_ref[r, :]
        for a_ref, w_ref in zip(a_refs, w_refs):
            acc = acc + jnp.dot(a_ref[r, :], w_ref[...], preferred_element_type=F32)
        y = _layer_norm(acc, g_ref[...], b_ref[...])
        o_ref[r, :] = y
        if slabs:
            for s in range(S):
                o2_ref[pl.ds(h * th * S + s, th, stride=S), :] = y[:, s * LANES:(s + 1) * LANES]
        else:
            o2_ref[r, :] = y.astype(BF16)


def _out_ln(a_list, wb, x, g, b, *, slabs=False, tm=512):
    T, D = x.shape
    tm = min(tm, T)
    n_in = len(a_list)
    kk = a_list[0].shape[1]
    assert all(a.shape[1] == kk for a in a_list) and wb.shape[0] == n_in * kk and T % tm == 0
    w3 = wb.reshape(n_in, kk, D)
    in_specs = [pl.BlockSpec((tm, kk), lambda i: (i, 0)) for _ in range(n_in)]
    in_specs += [pl.BlockSpec((None, kk, D), functools.partial(lambda i, c: (c, 0, 0), c=c)) for c in range(n_in)]
    in_specs += [pl.BlockSpec((tm, D), lambda i: (i, 0)),
                 pl.BlockSpec((1, D), lambda i: (0, 0)),
                 pl.BlockSpec((1, D), lambda i: (0, 0))]
    S = D // LANES
    if slabs:
        out2 = jax.ShapeDtypeStruct((T * S, LANES), F32)
        spec2 = pl.BlockSpec((tm * S, LANES), lambda i: (i, 0))
    else:
        out2 = jax.ShapeDtypeStruct((T, D), BF16)
        spec2 = pl.BlockSpec((tm, D), lambda i: (i, 0))
    return pl.pallas_call(
        functools.partial(_out_ln_kernel, n_in=n_in, slabs=slabs),
        out_shape=(jax.ShapeDtypeStruct((T, D), F32), out2),
        grid=(T // tm,),
        in_specs=in_specs,
        out_specs=(pl.BlockSpec((tm, D), lambda i: (i, 0)), spec2),
        compiler_params=_cparams(("arbitrary",)),
        name="out_ln",
    )(*a_list, *([w3] * n_in), x, g.reshape(1, D), b.reshape(1, D))


def _res_ln_kernel(x_ref, y_ref, g_ref, b_ref, o_ref, ob_ref):
    y = _layer_norm(ALPHA * x_ref[...] + y_ref[...], g_ref[...], b_ref[...])
    o_ref[...] = y
    ob_ref[...] = y.astype(BF16)


def _res_ln(x, y, g, b, tm=512):
    T, D = x.shape
    tm = min(tm, T)
    row = pl.BlockSpec((tm, D), lambda i: (i, 0))
    vec = pl.BlockSpec((1, D), lambda i: (0, 0))
    return pl.pallas_call(
        _res_ln_kernel,
        out_shape=(jax.ShapeDtypeStruct((T, D), F32), jax.ShapeDtypeStruct((T, D), BF16)),
        grid=(T // tm,),
        in_specs=[row, row, vec, vec],
        out_specs=(row, row),
        compiler_params=_cparams(("arbitrary",)),
        name="res_ln",
    )(x, y, g.reshape(1, D), b.reshape(1, D))


FFN_TM = 1024
FFN_TM_MOE = 1024
FFN_TF = 512
FFN_CHUNK = 256


def _ffn_kernel(vt_ref, ve_ref, vlo_ref, vhi_ref, vok_ref, x_ref, wg_ref, wu_ref, wd_ref, o_ref):
    v = pl.program_id(0)
    f = pl.program_id(1)
    lo = vlo_ref[v]
    hi = vhi_ref[v]

    @pl.when(jnp.logical_and(vok_ref[v] == 2, f == 0))
    def _():
        o_ref[...] = jnp.zeros(o_ref.shape, o_ref.dtype)

    def rows(c):
        return pl.ds(pl.multiple_of(c * FFN_CHUNK, FFN_CHUNK), FFN_CHUNK)

    @pl.when(f == 0)
    def _():
        def zero(c, _):
            o_ref[rows(c), :] = jnp.zeros((FFN_CHUNK, o_ref.shape[1]), o_ref.dtype)
            return 0
        lax.fori_loop(lo, hi, zero, 0)

    def swiglu(c, nchunks):
        r = pl.ds(pl.multiple_of(c * FFN_CHUNK, FFN_CHUNK), nchunks * FFN_CHUNK)
        x = x_ref[r, :]
        g = jnp.dot(x, wg_ref[...].astype(BF16), preferred_element_type=F32)
        u = jnp.dot(x, wu_ref[...].astype(BF16), preferred_element_type=F32)
        h = (g * jax.nn.sigmoid(g) * u).astype(BF16)
        o_ref[r, :] += jnp.dot(h, wd_ref[...].astype(BF16), preferred_element_type=F32)

    def pair(i, _):
        swiglu(lo + 2 * i, 2)
        return 0

    n = hi - lo
    lax.fori_loop(0, n // 2, pair, 0)

    @pl.when(n % 2 == 1)
    def _():
        swiglu(hi - 1, 1)


def _ffn(visits, xs, wg, wu, wd, *, tm):
    vt, ve, vlo, vhi, vok = visits
    R, D = xs.shape
    F = wg.shape[2]
    assert R % tm == 0 and F % FFN_TF == 0 and tm % FFN_CHUNK == 0
    nf = F // FFN_TF
    V = vt.shape[0]
    row_mode = {} if tm <= FFN_TM else dict(pipeline_mode=pl.Buffered(1))

    def fidx(v, f, vok):
        return jnp.where(vok[v] == 1, f, nf - 1)

    return pl.pallas_call(
        _ffn_kernel,
        out_shape=jax.ShapeDtypeStruct((R, D), F32),
        grid_spec=pltpu.PrefetchScalarGridSpec(
            num_scalar_prefetch=5,
            grid=(V, nf),
            in_specs=[
                pl.BlockSpec((tm, D), lambda v, f, vt, ve, vlo, vhi, vok: (vt[v], 0), **row_mode),
                pl.BlockSpec((None, D, FFN_TF), lambda v, f, vt, ve, vlo, vhi, vok: (ve[v], 0, fidx(v, f, vok))),
                pl.BlockSpec((None, D, FFN_TF), lambda v, f, vt, ve, vlo, vhi, vok: (ve[v], 0, fidx(v, f, vok))),
                pl.BlockSpec((None, FFN_TF, D), lambda v, f, vt, ve, vlo, vhi, vok: (ve[v], fidx(v, f, vok), 0)),
            ],
            out_specs=pl.BlockSpec((tm, D), lambda v, f, vt, ve, vlo, vhi, vok: (vt[v], 0), **row_mode),
        ),
        compiler_params=_cparams(("arbitrary", "arbitrary")),
        name="ffn",
    )(vt, ve, vlo, vhi, vok, xs, wg, wu, wd)


def _dense_visits(T, layer, tm):
    nt = T // tm
    cpt = tm // FFN_CHUNK
    z = jnp.zeros((nt,), I32)
    return (jnp.arange(nt, dtype=I32), z + layer, z, z + cpt, z + 1)


def _router_kernel(x_ref, r_ref, o_ref):
    logits = jnp.dot(x_ref[...], r_ref[...], preferred_element_type=F32, precision=lax.Precision.HIGHEST)
    lane = lax.broadcasted_iota(I32, logits.shape, 1)
    logits = jnp.where(lane < N_EXPERTS, logits, -jnp.inf)
    v1 = jnp.max(logits, axis=-1, keepdims=True)
    i1 = jnp.min(jnp.where(logits == v1, lane, LANES), axis=-1, keepdims=True)
    rest = jnp.where(lane == i1, -jnp.inf, logits)
    v2 = jnp.max(rest, axis=-1, keepdims=True)
    i2 = jnp.min(jnp.where(rest == v2, lane, LANES), axis=-1, keepdims=True)
    e2 = jnp.exp(v2 - v1)
    g1 = 1.0 / (1.0 + e2)
    g2 = e2 / (1.0 + e2)
    out = jnp.where(lane == 0, i1.astype(F32), 0.0)
    out = jnp.where(lane == 1, i2.astype(F32), out)
    out = jnp.where(lane == 2, g1, out)
    out = jnp.where(lane == 3, g2, out)
    o_ref[...] = out


def _router(x, router, tm=512):
    T, D = x.shape
    tm = min(tm, T)
    rp = jnp.pad(router, ((0, 0), (0, LANES - N_EXPERTS)))
    return pl.pallas_call(
        _router_kernel,
        out_shape=jax.ShapeDtypeStruct((T, LANES), F32),
        grid=(T // tm,),
        in_specs=[pl.BlockSpec((tm, D), lambda i: (i, 0)), pl.BlockSpec((D, LANES), lambda i: (0, 0))],
        out_specs=pl.BlockSpec((tm, LANES), lambda i: (i, 0)),
        compiler_params=_cparams(("arbitrary",)),
        name="router",
    )(x, rp)


GATHER_ROWS = 256


def _row_copy(src_hbm, row, dst, r, sem):
    return pltpu.make_async_copy(src_hbm.at[pl.ds(row, 1), :], dst.at[pl.ds(r, 1), :], sem)


def _prefetch_rows(issue, slot):
    i = pl.program_id(0)

    @pl.when(i == 0)
    def _():
        issue(0, 0)

    @pl.when(i + 1 < pl.num_programs(0))
    def _():
        issue(i + 1, 1 - slot)


def _gather_kernel(tok_ref, x_hbm, o_ref, buf, sem, *, S):
    slot = pl.program_id(0) % 2

    def issue(step, s):
        def body(r2, _):
            for k in range(2):
                r = 2 * r2 + k
                src = pl.ds(pl.multiple_of(tok_ref[step * GATHER_ROWS + r] * S, S), S)
                dst = pl.ds(pl.multiple_of(r * S, S), S)
                pltpu.make_async_copy(x_hbm.at[src, :], buf.at[s, dst, :], sem.at[s]).start(priority=k)
            return 0
        lax.fori_loop(0, GATHER_ROWS // 2, body, 0)

    _prefetch_rows(issue, slot)
    pltpu.make_async_copy(x_hbm.at[pl.ds(0, GATHER_ROWS * S), :], buf.at[slot], sem.at[slot]).wait()
    for s in range(S):
        o_ref[:, s * LANES:(s + 1) * LANES] = buf[slot, pl.ds(s, GATHER_ROWS, stride=S), :].astype(o_ref.dtype)


def _gather_rows(xslab, tok, D):
    R = tok.shape[0]
    S = D // LANES
    assert R % GATHER_ROWS == 0 and xslab.shape[0] >= GATHER_ROWS * S and S % 8 == 0
    return pl.pallas_call(
        functools.partial(_gather_kernel, S=S),
        out_shape=jax.ShapeDtypeStruct((R, D), BF16),
        grid_spec=pltpu.PrefetchScalarGridSpec(
            num_scalar_prefetch=1,
            grid=(R // GATHER_ROWS,),
            in_specs=[pl.BlockSpec(memory_space=pl.ANY)],
            out_specs=pl.BlockSpec((GATHER_ROWS, D), lambda i, tok: (i, 0)),
            scratch_shapes=[pltpu.VMEM((2, GATHER_ROWS * S, LANES), F32), pltpu.SemaphoreType.DMA((2,))],
        ),
        compiler_params=_cparams(("arbitrary",)),
        name="gather_rows",
    )(tok, xslab)


def _combine_kernel(p0_ref, p1_ref, y_hbm, r_ref, x_ref, g_ref, b_ref, o_ref, ob_ref, buf, sem):
    slot = pl.program_id(0) % 2

    def issue(step, s):
        def body(r, _):
            t = step * GATHER_ROWS + r
            _row_copy(y_hbm, p0_ref[t], buf.at[s, 0], r, sem.at[s]).start(priority=0)
            _row_copy(y_hbm, p1_ref[t], buf.at[s, 1], r, sem.at[s]).start(priority=1)
            return 0
        lax.fori_loop(0, GATHER_ROWS, body, 0)

    _prefetch_rows(issue, slot)
    for k in range(TOP_K):
        pltpu.make_async_copy(y_hbm.at[pl.ds(0, GATHER_ROWS), :], buf.at[slot, k], sem.at[slot]).wait()
    route = r_ref[...]
    g0 = route[:, TOP_K:TOP_K + 1]
    g1 = route[:, TOP_K + 1:TOP_K + 2]
    z = ALPHA * x_ref[...] + (g0 * buf[slot, 0] + g1 * buf[slot, 1])
    y = _layer_norm(z, g_ref[...], b_ref[...])
    o_ref[...] = y
    ob_ref[...] = y.astype(BF16)


def _combine_ln(y, p0, p1, route, x, g, b):
    T, D = x.shape
    assert T % GATHER_ROWS == 0 and y.shape[0] >= GATHER_ROWS
    row = pl.BlockSpec((GATHER_ROWS, D), lambda i, p0, p1: (i, 0))
    vec = pl.BlockSpec((1, D), lambda i, p0, p1: (0, 0))
    rspec = pl.BlockSpec((GATHER_ROWS, LANES), lambda i, p0, p1: (i, 0))
    return pl.pallas_call(
        _combine_kernel,
        out_shape=(jax.ShapeDtypeStruct((T, D), F32), jax.ShapeDtypeStruct((T, D), BF16)),
        grid_spec=pltpu.PrefetchScalarGridSpec(
            num_scalar_prefetch=2,
            grid=(T // GATHER_ROWS,),
            in_specs=[pl.BlockSpec(memory_space=pl.ANY), rspec, row, vec, vec],
            out_specs=(row, row),
            scratch_shapes=[pltpu.VMEM((2, TOP_K, GATHER_ROWS, D), F32), pltpu.SemaphoreType.DMA((2,))],
        ),
        compiler_params=_cparams(("arbitrary",)),
        name="combine_ln",
    )(p0, p1, y, route, x, g.reshape(1, D), b.reshape(1, D))


def _moe_plan(route, T, layer, tm):
    E = N_EXPERTS
    e = route[:, :TOP_K].astype(I32)
    ef = e.reshape(-1)
    onehot = (ef[:, None] == jnp.arange(E, dtype=I32)[None, :]).astype(I32)
    rank = jnp.cumsum(onehot, axis=0) - onehot
    counts = jnp.sum(onehot, axis=0)
    chunks = (counts + FFN_CHUNK - 1) // FFN_CHUNK
    seg_end = jnp.cumsum(chunks)
    seg_start = seg_end - chunks
    pos = seg_start[ef] * FFN_CHUNK + jnp.sum(rank * onehot, axis=1)
    R = -(-(TOP_K * T + E * FFN_CHUNK) // tm) * tm
    cpt = tm // FFN_CHUNK
    nt = R // tm
    n_slots = nt + E
    slot = jnp.arange(n_slots, dtype=I32)

    def place(vals, dest):
        return jnp.sum(jnp.where(dest[None, :] == slot[:, None], vals[None, :], 0), axis=1).astype(I32)

    tile_pts = jnp.arange(nt, dtype=I32) * cpt
    seg_pts = seg_start.astype(I32)
    tile_rank = jnp.arange(nt, dtype=I32) + jnp.sum(seg_pts[None, :] < tile_pts[:, None], axis=1)
    seg_rank = jnp.arange(E, dtype=I32) + jnp.sum(tile_pts[None, :] <= seg_pts[:, None], axis=1)
    pts = place(jnp.concatenate([tile_pts, seg_pts]), jnp.concatenate([tile_rank, seg_rank]))
    used = seg_end[-1]
    used_tiles_end = (used + cpt - 1) // cpt * cpt
    nxt = jnp.concatenate([pts[1:], jnp.array([nt * cpt], I32)])
    ok = (pts < used) & (nxt > pts)
    nxt = jnp.where(nxt >= used, jnp.maximum(used_tiles_end, pts), nxt)
    tile = pts // cpt
    nxt = jnp.minimum(nxt, (tile + 1) * cpt)
    exp = jnp.minimum(jnp.sum(seg_end[None, :] <= pts[:, None], axis=1), E - 1).astype(I32)
    last = jnp.max(jnp.where(ok, slot, 0))
    vt = jnp.where(ok, tile, jnp.sum(jnp.where(slot == last, tile, 0)))
    ve = jnp.where(ok, exp, jnp.sum(jnp.where(slot == last, exp, 0)))
    vlo = jnp.where(ok, pts - tile * cpt, 0)
    vhi = jnp.where(ok, nxt - tile * cpt, 0)
    oki = ok.astype(I32)
    dest = jnp.where(ok, jnp.cumsum(oki) - oki, jnp.sum(oki) + jnp.cumsum(1 - oki) - (1 - oki))
    vt, ve, vlo, vhi, vok = (place(a, dest) for a in (vt, ve, vlo, vhi, oki))
    used_tiles = used_tiles_end // cpt
    fill = slot - (n_slots - (nt - used_tiles))
    vt = jnp.where(fill >= 0, used_tiles + fill, vt).astype(I32)
    vok = jnp.where(fill >= 0, 2, vok).astype(I32)
    visits = (vt, (ve + layer * E).astype(I32), vlo, vhi, vok)
    p = pos.reshape(T, TOP_K).astype(I32)
    tok = jnp.zeros((R,), I32).at[pos].set(jnp.arange(TOP_K * T, dtype=I32) // TOP_K)
    return tok, visits, p[:, 0], p[:, 1]


DSA_TK = 512
NT_DIMS = (((1,), (1,)), ((), ()))
DSA_Q_SCALE = (HEAD_DIM ** -0.5) * LOG2_E


DSA_DIGIT = 15


def _srl(x, k):
    return lax.shift_right_logical(x, jnp.full(x.shape, k, x.dtype))


def _dsa_kernel(q_ref, iq_ref, iw_ref, ike_ref, iko_ref, kt_ref, v_ref, o_ref,
                key_s, pk_s, bias_s, s_s, p_s, m_s, l_s, al_s, acc_s, *, ksel, idx_bits):
    n = pl.program_id(0)
    TK = DSA_TK
    nkt = (n * BLOCK + BLOCK + TK - 1) // TK
    qpos = n * BLOCK + lax.broadcasted_iota(I32, (BLOCK, 1), 0)
    lane_k = lax.broadcasted_iota(I32, (1, TK), 1)

    def keys(j):
        return pl.ds(pl.multiple_of(j * TK, TK), TK)

    iqb = iq_ref[...]
    npair = IDX_HEADS // 2
    lhs = jnp.concatenate([iqb[:, p * LANES:(p + 1) * LANES] for p in range(npair)], axis=0)
    wts = iw_ref[...] * ((IDX_DIM ** -0.5) * (IDX_HEADS ** -0.5))

    def score_tile(j, _):
        digits = []
        for part in range(2):
            kr = pl.ds(pl.multiple_of(j * TK + part * half, half), half)
            le = lax.dot_general(lhs, ike_ref[kr, :], NT_DIMS, preferred_element_type=F32)
            lo = lax.dot_general(lhs, iko_ref[kr, :], NT_DIMS, preferred_element_type=F32)
            sc = jnp.zeros((BLOCK, half), F32)
            for p in range(npair):
                sc = sc + jnp.maximum(le[p * BLOCK:(p + 1) * BLOCK], 0.0) * wts[:, 2 * p:2 * p + 1]
                sc = sc + jnp.maximum(lo[p * BLOCK:(p + 1) * BLOCK], 0.0) * wts[:, 2 * p + 1:2 * p + 2]
            bits = pltpu.bitcast(sc, I32)
            key = bits ^ ((bits >> 31) & 0x7FFFFFFF)
            key = jnp.where(j * TK + part * half + lane_k[:, :half] <= qpos, key, INT_MIN)
            key_s[j, :, part * half:(part + 1) * half] = key
            digits.append(_srl(key ^ INT_MIN, 32 - DSA_DIGIT))
        pk_s[j] = digits[0] | jnp.left_shift(digits[1], 16) | guard
        return 0

    half = TK // 2
    guard = jnp.int32(-2147450880)
    ones2 = jnp.int32(0x00010001)

    def pack(d):
        return d[:, :half] | jnp.left_shift(d[:, half:], 16) | guard

    def packed_count(c):
        cc = c | jnp.left_shift(c, 16)

        def body(j, acc):
            hit = _srl(pk_s[j] - cc, DSA_DIGIT) & ones2
            for k in range(half // LANES):
                acc = acc + hit[:, k * LANES:(k + 1) * LANES]
            return acc
        acc = lax.fori_loop(0, nkt, body, jnp.zeros((BLOCK, LANES), I32))
        tot = (acc & 0xFFFF) + _srl(acc, 16)
        return jnp.sum(tot.astype(F32), axis=1, keepdims=True)

    lax.fori_loop(0, nkt, score_tile, 0)

    def count(pred):
        def body(j, acc):
            hit = jnp.where(pred(key_s[j], j * TK + lane_k), 1.0, 0.0)
            for c in range(TK // LANES):
                acc = acc + hit[:, c * LANES:(c + 1) * LANES]
            return acc
        acc = lax.fori_loop(0, nkt, body, jnp.zeros((BLOCK, LANES), F32))
        return jnp.sum(acc, axis=1, keepdims=True)

    searching = (n + 1) * BLOCK > ksel

    def write_bias(thr, pidx):
        def body(j, _):
            key = key_s[j]
            sel = (key > thr) | ((key == thr) & (j * TK + lane_k <= pidx))
            bias_s[j] = jnp.where(sel, 0.0, NEG)
            return 0
        lax.fori_loop(0, nkt, body, 0)

    @pl.when(jnp.logical_not(searching))
    def _():
        write_bias(jnp.full((BLOCK, 1), INT_MIN, I32), jnp.full((BLOCK, 1), -1, I32))

    @pl.when(searching)
    def _():
        def digit_search(base, cnt_thr, miss):
            def step(carry):
                i, d, cnt_thr, _ = carry
                c = d | jnp.left_shift(jnp.int32(1), DSA_DIGIT - 1 - i)
                cnt = base + packed_count(c)
                ok = cnt >= ksel
                cnt_thr = jnp.where(ok, cnt, cnt_thr)
                return i + 1, jnp.where(ok, c, d), cnt_thr, jnp.max(jnp.abs(cnt_thr - ksel))

            _, d, cnt_thr, miss = lax.while_loop(
                lambda c: jnp.logical_and(c[0] < DSA_DIGIT, c[3] > 0.0), step,
                (jnp.int32(0), jnp.zeros((BLOCK, 1), I32), cnt_thr, miss))
            return d, cnt_thr, miss

        zero = jnp.zeros((BLOCK, 1), F32)
        d_hi, cnt_thr, miss = digit_search(zero, zero - 1.0, jnp.float32(1.0))

        top = (1 << DSA_DIGIT) - 1
        above = jnp.where(d_hi == top, 0.0, packed_count(jnp.minimum(d_hi + 1, top)))

        @pl.when(miss > 0.0)
        def _():
            def body(j, _):
                u = key_s[j] ^ INT_MIN
                mid = _srl(u, 32 - 2 * DSA_DIGIT) & top
                pk_s[j] = pack(jnp.where(_srl(u, 32 - DSA_DIGIT) == d_hi, mid, 0))
                return 0
            lax.fori_loop(0, nkt, body, 0)

        d_mid, cnt_thr, miss = digit_search(above, cnt_thr, miss)
        thr = (jnp.left_shift(d_hi, 32 - DSA_DIGIT) | jnp.left_shift(d_mid, 32 - 2 * DSA_DIGIT)) ^ INT_MIN

        def bit_step(carry):
            i, thr, cnt_thr, _ = carry
            cand = thr + jnp.left_shift(jnp.int32(1), 31 - 2 * DSA_DIGIT - i)
            cnt = count(lambda key, kidx: key >= cand)
            ok = cnt >= ksel
            cnt_thr = jnp.where(ok, cnt, cnt_thr)
            return i + 1, jnp.where(ok, cand, thr), cnt_thr, jnp.max(jnp.abs(cnt_thr - ksel))

        _, thr, _, miss = lax.while_loop(
            lambda c: jnp.logical_and(c[0] < 32 - 2 * DSA_DIGIT, c[3] > 0.0), bit_step,
            (jnp.int32(0), thr, cnt_thr, miss))
        exact = miss == 0.0

        @pl.when(exact)
        def _():
            write_bias(thr, jnp.full((BLOCK, 1), 2 ** 30, I32))

        @pl.when(jnp.logical_not(exact))
        def _():
            need = ksel - count(lambda key, kidx: key > thr)

            def idx_step(i, pidx):
                cand = pidx + jnp.left_shift(jnp.int32(1), idx_bits - 1 - i)
                c = count(lambda key, kidx: (key == thr) & (kidx < cand))
                return jnp.where(c < need, cand, pidx)

            pidx = lax.fori_loop(0, idx_bits, idx_step, jnp.zeros((BLOCK, 1), I32))
            write_bias(thr, pidx)

    qb = q_ref[...]
    qs = jnp.concatenate([qb[:, h * LANES:(h + 1) * LANES] for h in range(DSA_HEADS)], axis=0)
    ncol = TK // LANES
    m_s[...] = jnp.full(m_s.shape, NEG, F32)
    l_s[...] = jnp.zeros(l_s.shape, F32)
    acc_s[...] = jnp.zeros(acc_s.shape, F32)

    def attend(j, _):
        s_s[...] = lax.dot_general(qs, kt_ref[keys(j), :], NT_DIMS, preferred_element_type=F32)
        for h in range(DSA_HEADS):
            r = slice(h * BLOCK, (h + 1) * BLOCK)
            t = [s_s[r, c * LANES:(c + 1) * LANES] + bias_s[j, :, c * LANES:(c + 1) * LANES] for c in range(ncol)]
            mx = t[0]
            for c in range(1, ncol):
                mx = jnp.maximum(mx, t[c])
            m_old = m_s[r, :]
            m_new = jnp.maximum(m_old, jnp.max(mx, axis=-1, keepdims=True))
            alpha = jnp.exp2(m_old - m_new)
            psum = jnp.zeros((BLOCK, LANES), F32)
            for c in range(ncol):
                p = jnp.exp2(t[c] - m_new)
                psum = psum + p
                p_s[r, c * LANES:(c + 1) * LANES] = p.astype(BF16)
            m_s[r, :] = m_new
            l_s[r, :] = alpha * l_s[r, :] + jnp.sum(psum, axis=-1, keepdims=True)
            al_s[r, :] = alpha
        acc_s[...] = al_s[...] * acc_s[...] + jnp.dot(p_s[...], v_ref[keys(j), :], preferred_element_type=F32)
        return 0

    lax.fori_loop(0, nkt, attend, 0)
    for h in range(DSA_HEADS):
        r = slice(h * BLOCK, (h + 1) * BLOCK)
        o_ref[:, h * LANES:(h + 1) * LANES] = (acc_s[r, :] / l_s[r, :]).astype(o_ref.dtype)


DSA_SIDE_IQ = IDX_HEADS * IDX_DIM
DSA_SIDE_COLS = DSA_SIDE_IQ + 4 * LANES


def _dsa_side_weights(w_in_l):
    o = DSA_Q
    wk = w_in_l[:, o:o + HEAD_DIM]; o += HEAD_DIM
    wv = w_in_l[:, o:o + HEAD_DIM]; o += HEAD_DIM
    wiq = w_in_l[:, o:o + DSA_SIDE_IQ]; o += DSA_SIDE_IQ
    wik = w_in_l[:, o:o + IDX_DIM]; o += IDX_DIM
    wiw = w_in_l[:, o:o + IDX_HEADS]
    z = jnp.zeros_like(wik)
    side = jnp.concatenate([wiq, wk, wv, wik, z, z, wik], axis=1)
    return side, jnp.pad(wiw, ((0, 0), (0, LANES - IDX_HEADS)))


def _dsa(q, side, iw):
    T = q.shape[0]
    TK = DSA_TK
    assert T % TK == 0 and TK % BLOCK == 0 and HEAD_DIM == LANES and 2 * IDX_DIM == LANES
    assert DSA_SIDE_IQ % LANES == 0
    ksel = min(DSA_TOPK, T // 4)
    assert ksel % BLOCK == 0
    nt = T // TK
    rows = DSA_HEADS * BLOCK
    g0 = DSA_SIDE_IQ // LANES

    def key_side(group):
        return pl.BlockSpec((T, LANES), lambda n: (0, group))

    return pl.pallas_call(
        functools.partial(_dsa_kernel, ksel=ksel, idx_bits=max(1, (T - 1).bit_length())),
        out_shape=jax.ShapeDtypeStruct((T, DSA_Q), BF16),
        grid=(T // BLOCK,),
        in_specs=[
            pl.BlockSpec((BLOCK, DSA_Q), lambda n: (n, 0)),
            pl.BlockSpec((BLOCK, DSA_SIDE_IQ), lambda n: (n, 0)),
            pl.BlockSpec((BLOCK, LANES), lambda n: (n, 0)),
            key_side(g0 + 2), key_side(g0 + 3), key_side(g0), key_side(g0 + 1),
        ],
        out_specs=pl.BlockSpec((BLOCK, DSA_Q), lambda n: (n, 0)),
        scratch_shapes=[
            pltpu.VMEM((nt, BLOCK, TK), I32),
            pltpu.VMEM((nt, BLOCK, TK // 2), I32),
            pltpu.VMEM((nt, BLOCK, TK), F32),
            pltpu.VMEM((rows, TK), F32),
            pltpu.VMEM((rows, TK), BF16),
            pltpu.VMEM((rows, LANES), F32),
            pltpu.VMEM((rows, LANES), F32),
            pltpu.VMEM((rows, LANES), F32),
            pltpu.VMEM((rows, HEAD_DIM), F32),
        ],
        compiler_params=_cparams(("arbitrary",)),
        name="dsa",
    )(q, side, iw, side, side, side, side)


def _even_layer(x, xb, rope, i, w_in, pool_w, pool_scale, w_out, ln1_g, ln1_b, wg, wu, wd, ln2_g, ln2_b):
    T = x.shape[0]
    cos2, sin2 = rope
    u = _proj(xb, w_in, i, cos2, sin2, col0=0, ncols=POOL_DIM, tn=1024, rope=(0, 0), out_dtype=F32)
    qkv = _proj(xb, w_in, i, cos2, sin2, col0=POOL_DIM, ncols=3 * DIL_QKV, tn=1024, rope=(0, 2 * DIL_QKV),
                out_dtype=F32, head_major=True)
    pooled = _pool(u, pool_w, i, pool_scale[i])
    attn = _dilated(qkv)
    x1, x1b = _out_ln([pooled, attn], w_out[i].astype(BF16), x, ln1_g[i], ln1_b[i])
    tm = min(FFN_TM, T)
    y = _ffn(_dense_visits(T, i, tm), x1b, wg, wu, wd, tm=tm)
    return _res_ln(x1, y, ln2_g[i], ln2_b[i])


def _odd_layer(x, xb, rope, i, w_in, w_out, ln1_g, ln1_b, router, wg, wu, wd, ln2_g, ln2_b):
    T, D = x.shape
    cos2, sin2 = rope
    q = _proj(xb, w_in, i, cos2, sin2, col0=0, ncols=DSA_Q, tn=1024, rope=(0, DSA_Q), out_dtype=BF16,
              out_scale=DSA_Q_SCALE)
    w_side, w_iw = _dsa_side_weights(w_in[i])
    side = _proj(xb, w_side[None], 0, cos2, sin2, col0=0, ncols=DSA_SIDE_COLS, tn=DSA_SIDE_COLS,
                 rope=(DSA_SIDE_IQ, DSA_SIDE_IQ + HEAD_DIM), out_dtype=BF16)
    iw = _proj(xb, w_iw[None], 0, cos2, sin2, col0=0, ncols=LANES, tn=LANES, rope=(0, 0), out_dtype=F32)
    attn = _dsa(q, side, iw)
    x1, x1_slabs = _out_ln([attn], w_out[i].astype(BF16), x, ln1_g[i], ln1_b[i], slabs=True)
    route = _router(x1, router[i])
    tok, visits, p0, p1 = _moe_plan(route, T, i, FFN_TM_MOE)
    xs = _gather_rows(x1_slabs, tok, D)
    F = wg.shape[3]
    y = _ffn(visits, xs, wg.reshape(-1, D, F), wu.reshape(-1, D, F), wd.reshape(-1, F, D), tm=FFN_TM_MOE)
    return _combine_ln(y, p0, p1, route, x1, ln2_g[i], ln2_b[i])


def kernel(x, ev_w_in, ev_pool_w, ev_pool_scale, ev_w_out, ev_ln1_g, ev_ln1_b, ev_ffn_w_gate, ev_ffn_w_up,
           ev_ffn_w_down, ev_ln2_g, ev_ln2_b, od_w_in, od_w_out, od_ln1_g, od_ln1_b, od_router,
           od_exp_w_gate, od_exp_w_up, od_exp_w_down, od_ln2_g, od_ln2_b):
    B, T, D = x.shape
    rope = _rope_tables(T)
    outs = []
    for bi in range(B):
        h = x[bi]
        hb = h.astype(BF16)
        for layer in range(DEPTH):
            i = layer // 2
            if layer % 2 == 0:
                h, hb = _even_layer(h, hb, rope, i, ev_w_in, ev_pool_w, ev_pool_scale, ev_w_out,
                                    ev_ln1_g, ev_ln1_b, ev_ffn_w_gate, ev_ffn_w_up,
                                    ev_ffn_w_down, ev_ln2_g, ev_ln2_b)
            else:
                h, hb = _odd_layer(h, hb, rope, i, od_w_in, od_w_out, od_ln1_g, od_ln1_b,
                                   od_router, od_exp_w_gate, od_exp_w_up, od_exp_w_down,
                                   od_ln2_g, od_ln2_b)
        outs.append(h)
    return jnp.stack(outs, axis=0)
```

```python
import functools

import jax
import jax.numpy as jnp
from jax import lax
from jax.experimental import pallas as pl
from jax.experimental.pallas import tpu as pltpu

F32 = jnp.float32
BF16 = jnp.bfloat16
I32 = jnp.int32

D_MODEL = 2048
DEPTH = 4
HEAD_DIM = 128
ROPE_THETA = 10000.0
LN_EPS = 1e-5
BLOCK = 128
ALPHA = (2 * DEPTH) ** 0.25
POOL_WINDOWS = (2, 4, 8, 16)
POOL_GROUP_DIM = D_MODEL // 8
POOL_DIM = len(POOL_WINDOWS) * POOL_GROUP_DIM
DIL_PATTERNS = ((128, 1), (512, 4), (2048, 16))
DIL_HEADS_PER_GROUP = D_MODEL // 256
DIL_HEADS = len(DIL_PATTERNS) * DIL_HEADS_PER_GROUP
DIL_QKV = DIL_HEADS * HEAD_DIM
DIL_OUT = DIL_HEADS_PER_GROUP * HEAD_DIM
DSA_HEADS = D_MODEL // HEAD_DIM
IDX_HEADS = 16
IDX_DIM = 64
DSA_TOPK = 256
DSA_Q = DSA_HEADS * HEAD_DIM
D_FF = 5632
N_EXPERTS = 8
TOP_K = 2

LANES = 128
VMEM_LIMIT = 56 * 1024 * 1024

NEG = -1e30
LOG2_E = 1.4426950408889634
INT_MIN = -(2 ** 31)


def _cparams(sem):
    return pltpu.CompilerParams(dimension_semantics=sem, vmem_limit_bytes=VMEM_LIMIT)


def _layer_norm(z, g, b):
    mu = jnp.mean(z, axis=-1, keepdims=True)
    zc = z - mu
    var = jnp.mean(zc * zc, axis=-1, keepdims=True)
    return zc * lax.rsqrt(var + LN_EPS) * g + b


PROJ_HEADS_PER_DOT = 2


def _proj_kernel(x_ref, w_ref, cos_ref, sin_ref, o_ref, wb_ref, *, rope, n_tiles, head_major, out_scale):
    @pl.when(pl.program_id(1) == 0)
    def _():
        wb_ref[...] = w_ref[...].astype(BF16)

    tn = wb_ref.shape[1]
    nh = tn // LANES
    if n_tiles == 1:
        lo, hi = rope[0] // LANES, rope[1] // LANES
        cos, sin = cos_ref[...], sin_ref[...]
    else:
        j = pl.program_id(0)
        rot = jnp.logical_and(j >= rope[0] // tn, j < rope[1] // tn)
        lo, hi = (0, nh) if rope[1] > rope[0] else (0, 0)
        cos = jnp.where(rot, cos_ref[...], 1.0)
        sin = jnp.where(rot, sin_ref[...], 0.0)

    x = x_ref[...]
    step = min(nh, PROJ_HEADS_PER_DOT)
    for c0 in range(0, nh, step):
        n = min(step, nh - c0)
        acc = jnp.dot(x, wb_ref[:, c0 * LANES:(c0 + n) * LANES], preferred_element_type=F32)
        for c in range(c0, c0 + n):
            a = acc[:, (c - c0) * LANES:(c - c0 + 1) * LANES]
            if lo <= c < hi:
                a = a * cos + pltpu.roll(a, HEAD_DIM // 2, 1) * sin
            if out_scale != 1.0:
                a = a * out_scale
            if head_major:
                o_ref[c] = a.astype(o_ref.dtype)
            else:
                o_ref[:, c * LANES:(c + 1) * LANES] = a.astype(o_ref.dtype)


def _proj(xb, w, layer, cos2, sin2, *, col0, ncols, tn, rope, out_dtype, head_major=False, out_scale=1.0,
          tm=1024):
    T, K = xb.shape
    tm = min(tm, T)
    assert T % tm == 0 and ncols % tn == 0 and col0 % tn == 0 and tn % LANES == 0
    assert all(r % (LANES if ncols == tn else tn) == 0 for r in rope)
    n_tiles = ncols // tn
    c0 = col0 // tn
    if head_major:
        out_shape = jax.ShapeDtypeStruct((ncols // LANES, T, LANES), out_dtype)
        out_spec = pl.BlockSpec((tn // LANES, tm, LANES), lambda j, i: (j, i, 0))
    else:
        out_shape = jax.ShapeDtypeStruct((T, ncols), out_dtype)
        out_spec = pl.BlockSpec((tm, tn), lambda j, i: (i, j))
    return pl.pallas_call(
        functools.partial(_proj_kernel, rope=rope, n_tiles=n_tiles, head_major=head_major,
                          out_scale=float(out_scale)),
        out_shape=out_shape,
        grid=(n_tiles, T // tm),
        in_specs=[
            pl.BlockSpec((tm, K), lambda j, i: (i, 0)),
            pl.BlockSpec((None, K, tn), lambda j, i: (layer, 0, j + c0)),
            pl.BlockSpec((tm, LANES), lambda j, i: (i, 0)),
            pl.BlockSpec((tm, LANES), lambda j, i: (i, 0)),
        ],
        out_specs=out_spec,
        scratch_shapes=[pltpu.VMEM((K, tn), BF16)],
        compiler_params=_cparams(("arbitrary", "arbitrary")),
        name="proj",
    )(xb, w, cos2, sin2)


def _rope_tables(T):
    inv_freq = ROPE_THETA ** (-jnp.arange(0, HEAD_DIM, 2, dtype=F32) / HEAD_DIM)
    ang = jnp.arange(T, dtype=F32)[:, None] * inv_freq[None, :]
    cos, sin = jnp.cos(ang), jnp.sin(ang)
    return jnp.concatenate([cos, cos], axis=-1), jnp.concatenate([-sin, sin], axis=-1)


POOL_HALO = 16


def _pool_kernel(u_ref, h_ref, w_ref, sc_ref, o_ref):
    i = pl.program_id(0)
    tm = u_ref.shape[0]
    cur = u_ref[...]
    halo = jnp.where(i > 0, h_ref[...], 0.0)
    cat = jnp.concatenate([halo, cur], axis=0)
    t = i * tm + lax.broadcasted_iota(I32, (tm, 1), 0)
    G = POOL_GROUP_DIM
    for g, w in enumerate(POOL_WINDOWS):
        s = cat[:, g * G:(g + 1) * G]
        k = 1
        while k < w:
            s = s + pltpu.roll(s, k, 0)
            k *= 2
        win = s[POOL_HALO:]
        cnt = jnp.minimum(t + 1, w).astype(F32)
        p = win / cnt - cur[:, g * G:(g + 1) * G]
        y = jnp.dot(p.astype(BF16), w_ref[g].astype(BF16), preferred_element_type=F32)
        o_ref[:, g * G:(g + 1) * G] = (y * sc_ref[:, g * G:(g + 1) * G]).astype(o_ref.dtype)


def _pool(u, pool_w, layer, pool_scale, tm=512):
    T = u.shape[0]
    tm = min(tm, T)
    assert T % tm == 0 and tm % POOL_HALO == 0 and max(POOL_WINDOWS) <= POOL_HALO
    hb = tm // POOL_HALO
    return pl.pallas_call(
        _pool_kernel,
        out_shape=jax.ShapeDtypeStruct((T, POOL_DIM), BF16),
        grid=(T // tm,),
        in_specs=[
            pl.BlockSpec((tm, POOL_DIM), lambda i: (i, 0)),
            pl.BlockSpec((POOL_HALO, POOL_DIM), lambda i: (jnp.maximum(i * hb - 1, 0), 0)),
            pl.BlockSpec((None,) + pool_w.shape[1:], lambda i: (layer, 0, 0, 0)),
            pl.BlockSpec((1, POOL_DIM), lambda i: (0, 0)),
        ],
        out_specs=pl.BlockSpec((tm, POOL_DIM), lambda i: (i, 0)),
        compiler_params=_cparams(("arbitrary",)),
        name="pool",
    )(u, u, pool_w, pool_scale.reshape(1, POOL_DIM))


DIL_SPAN = BLOCK * max(d for _, d in DIL_PATTERNS)
DIL_UNROLL = 4


def _dil_kernel(*refs):
    ng = len(DIL_PATTERNS)
    in_refs = refs[:5 * ng]
    o_ref = refs[5 * ng]
    kf_ref, vf_ref, os_ref, ms_ref, ls_ref = refs[5 * ng + 1:]
    b = pl.program_id(1)
    scale = HEAD_DIM ** -0.5

    qi = lax.broadcasted_iota(I32, (BLOCK, 2 * BLOCK), 0)
    kj = lax.broadcasted_iota(I32, (BLOCK, 2 * BLOCK), 1)
    rel = BLOCK + qi - kj
    blocks = []
    n_blocks = DIL_SPAN // BLOCK
    for g, (window, d) in enumerate(DIL_PATTERNS):
        q_ref, kc_ref, kp_ref, vc_ref, vp_ref = in_refs[5 * g:5 * g + 5]
        count = window // d
        band = (rel >= 0) & (rel <= count)
        bias_std = jnp.where(band, 0.0, NEG)
        bias_first = jnp.where(band & (kj >= BLOCK), 0.0, NEG)
        hist = BLOCK * d
        nsb = DIL_SPAN // hist
        assert d * nsb == n_blocks
        kf_ref[g, 0:hist, :] = kp_ref[...]
        kf_ref[g, hist:hist + DIL_SPAN, :] = kc_ref[...]
        vf_ref[g, 0:hist, :] = vp_ref[...]
        vf_ref[g, hist:hist + DIL_SPAN, :] = vc_ref[...]

        def block(idx, g=g, d=d, hist=hist, nsb=nsb, q_ref=q_ref, bias_std=bias_std, bias_first=bias_first):
            r = idx // nsb
            sb = idx % nsb
            start = r + hist * sb
            q = q_ref[pl.ds(start, BLOCK, stride=d), :].astype(BF16)
            k = kf_ref[g, pl.ds(start, 2 * BLOCK, stride=d), :].astype(BF16)
            v = vf_ref[g, pl.ds(start, 2 * BLOCK, stride=d), :].astype(BF16)
            s = lax.dot_general(q, k, (((1,), (1,)), ((), ())), preferred_element_type=F32) * scale
            first = jnp.logical_and(b == 0, sb == 0)
            s = s + jnp.where(first, bias_first, bias_std)
            m = jnp.max(s, axis=-1, keepdims=True)
            p = jnp.exp(s - m)
            l = jnp.sum(p, axis=-1, keepdims=True)
            o = jnp.dot(p.astype(BF16), v, preferred_element_type=F32)
            os_ref[g, pl.ds(start, BLOCK, stride=d), :] = o
            ms_ref[g, pl.ds(start, BLOCK, stride=d), :] = jnp.broadcast_to(m, (BLOCK, LANES))
            ls_ref[g, pl.ds(start, BLOCK, stride=d), :] = jnp.broadcast_to(l, (BLOCK, LANES))

        blocks.append(block)

    def body(i, _):
        for u in range(DIL_UNROLL):
            for block in blocks:
                block(i * DIL_UNROLL + u)
        return 0

    assert n_blocks % DIL_UNROLL == 0
    lax.fori_loop(0, n_blocks // DIL_UNROLL, body, 0)

    mx = ms_ref[0]
    for g in range(1, ng):
        mx = jnp.maximum(mx, ms_ref[g])
    num = jnp.zeros((DIL_SPAN, LANES), F32)
    den = jnp.zeros((DIL_SPAN, LANES), F32)
    for g in range(ng):
        e = jnp.exp(ms_ref[g] - mx)
        num = num + e * os_ref[g]
        den = den + e * ls_ref[g]
    o_ref[...] = (num / den).astype(o_ref.dtype)


def _dilated(qkv):
    T = qkv.shape[1]
    assert T % DIL_SPAN == 0 and HEAD_DIM == LANES
    H = DIL_HEADS_PER_GROUP
    in_specs, args = [], []
    max_hist = 0
    for g, (window, d) in enumerate(DIL_PATTERNS):
        assert window // d <= BLOCK and DIL_SPAN % (BLOCK * d) == 0
        hist = BLOCK * d
        max_hist = max(max_hist, hist)
        ratio = DIL_SPAN // hist

        def cur(j, b, base):
            return (base + j, b, 0)

        def prev(j, b, base, ratio=ratio):
            return (base + j, jnp.maximum(b * ratio - 1, 0), 0)

        qh, kh, vh = g * H, DIL_HEADS + g * H, 2 * DIL_HEADS + g * H
        in_specs += [
            pl.BlockSpec((None, DIL_SPAN, LANES), functools.partial(cur, base=qh)),
            pl.BlockSpec((None, DIL_SPAN, LANES), functools.partial(cur, base=kh)),
            pl.BlockSpec((None, hist, LANES), functools.partial(prev, base=kh)),
            pl.BlockSpec((None, DIL_SPAN, LANES), functools.partial(cur, base=vh)),
            pl.BlockSpec((None, hist, LANES), functools.partial(prev, base=vh)),
        ]
        args += [qkv] * 5
    ng = len(DIL_PATTERNS)
    return pl.pallas_call(
        _dil_kernel,
        out_shape=jax.ShapeDtypeStruct((T, DIL_OUT), BF16),
        grid=(H, T // DIL_SPAN),
        in_specs=in_specs,
        out_specs=pl.BlockSpec((DIL_SPAN, LANES), lambda j, b: (b, j)),
        scratch_shapes=[
            pltpu.VMEM((ng, max_hist + DIL_SPAN, LANES), F32),
            pltpu.VMEM((ng, max_hist + DIL_SPAN, LANES), F32),
            pltpu.VMEM((ng, DIL_SPAN, LANES), F32),
            pltpu.VMEM((ng, DIL_SPAN, LANES), F32),
            pltpu.VMEM((ng, DIL_SPAN, LANES), F32),
        ],
        compiler_params=_cparams(("arbitrary", "arbitrary")),
        name="dilated",
    )(*args)


OUT_LN_PARTS = 2


def _out_ln_kernel(*refs, n_in, slabs):
    a_refs = refs[:n_in]
    w_refs = refs[n_in:2 * n_in]
    x_ref, g_ref, b_ref, o_ref, o2_ref = refs[2 * n_in:]
    tm, D = x_ref.shape
    S = D // LANES
    th = tm // OUT_LN_PARTS
    for h in range(OUT_LN_PARTS):
        r = slice(h * th, (h + 1) * th)
        acc = ALPHA * x_ref[r, :]
        for a_ref, w_ref in zip(a_refs, w_refs):
            acc = acc + jnp.dot(a_ref[r, :], w_ref[...], preferred_element_type=F32)
        y = _layer_norm(acc, g_ref[...], b_ref[...])
        o_ref[r, :] = y
        if slabs:
            for s in range(S):
                o2_ref[pl.ds(h * th * S + s, th, stride=S), :] = y[:, s * LANES:(s + 1) * LANES]
        else:
            o2_ref[r, :] = y.astype(BF16)


def _out_ln(a_list, wb, x, g, b, *, slabs=False, tm=512):
    T, D = x.shape
    tm = min(tm, T)
    n_in = len(a_list)
    kk = a_list[0].shape[1]
    assert all(a.shape[1] == kk for a in a_list) and wb.shape[0] == n_in * kk and T % tm == 0
    w3 = wb.reshape(n_in, kk, D)
    in_specs = [pl.BlockSpec((tm, kk), lambda i: (i, 0)) for _ in range(n_in)]
    in_specs += [pl.BlockSpec((None, kk, D), functools.partial(lambda i, c: (c, 0, 0), c=c)) for c in range(n_in)]
    in_specs += [pl.BlockSpec((tm, D), lambda i: (i, 0)),
                 pl.BlockSpec((1, D), lambda i: (0, 0)),
                 pl.BlockSpec((1, D), lambda i: (0, 0))]
    S = D // LANES
    if slabs:
        out2 = jax.ShapeDtypeStruct((T * S, LANES), F32)
        spec2 = pl.BlockSpec((tm * S, LANES), lambda i: (i, 0))
    else:
        out2 = jax.ShapeDtypeStruct((T, D), BF16)
        spec2 = pl.BlockSpec((tm, D), lambda i: (i, 0))
    return pl.pallas_call(
        functools.partial(_out_ln_kernel, n_in=n_in, slabs=slabs),
        out_shape=(jax.ShapeDtypeStruct((T, D), F32), out2),
        grid=(T // tm,),
        in_specs=in_specs,
        out_specs=(pl.BlockSpec((tm, D), lambda i: (i, 0)), spec2),
        compiler_params=_cparams(("arbitrary",)),
        name="out_ln",
    )(*a_list, *([w3] * n_in), x, g.reshape(1, D), b.reshape(1, D))


def _res_ln_kernel(x_ref, y_ref, g_ref, b_ref, o_ref, ob_ref):
    y = _layer_norm(ALPHA * x_ref[...] + y_ref[...], g_ref[...], b_ref[...])
    o_ref[...] = y
    ob_ref[...] = y.astype(BF16)


def _res_ln(x, y, g, b, tm=512):
    T, D = x.shape
    tm = min(tm, T)
    row = pl.BlockSpec((tm, D), lambda i: (i, 0))
    vec = pl.BlockSpec((1, D), lambda i: (0, 0))
    return pl.pallas_call(
        _res_ln_kernel,
        out_shape=(jax.ShapeDtypeStruct((T, D), F32), jax.ShapeDtypeStruct((T, D), BF16)),
        grid=(T // tm,),
        in_specs=[row, row, vec, vec],
        out_specs=(row, row),
        compiler_params=_cparams(("arbitrary",)),
        name="res_ln",
    )(x, y, g.reshape(1, D), b.reshape(1, D))


FFN_TM = 1024
FFN_TM_MOE = 1024
FFN_TF = 512
FFN_CHUNK = 256


def _ffn_kernel(vt_ref, ve_ref, vlo_ref, vhi_ref, vok_ref, x_ref, wg_ref, wu_ref, wd_ref, o_ref):
    v = pl.program_id(0)
    f = pl.program_id(1)
    lo = vlo_ref[v]
    hi = vhi_ref[v]

    @pl.when(jnp.logical_and(vok_ref[v] == 2, f == 0))
    def _():
        o_ref[...] = jnp.zeros(o_ref.shape, o_ref.dtype)

    def rows(c):
        return pl.ds(pl.multiple_of(c * FFN_CHUNK, FFN_CHUNK), FFN_CHUNK)

    @pl.when(f == 0)
    def _():
        def zero(c, _):
            o_ref[rows(c), :] = jnp.zeros((FFN_CHUNK, o_ref.shape[1]), o_ref.dtype)
            return 0
        lax.fori_loop(lo, hi, zero, 0)

    def swiglu(c, nchunks):
        r = pl.ds(pl.multiple_of(c * FFN_CHUNK, FFN_CHUNK), nchunks * FFN_CHUNK)
        x = x_ref[r, :]
        g = jnp.dot(x, wg_ref[...].astype(BF16), preferred_element_type=F32)
        u = jnp.dot(x, wu_ref[...].astype(BF16), preferred_element_type=F32)
        h = (g * jax.nn.sigmoid(g) * u).astype(BF16)
        o_ref[r, :] += jnp.dot(h, wd_ref[...].astype(BF16), preferred_element_type=F32)

    def pair(i, _):
        swiglu(lo + 2 * i, 2)
        return 0

    n = hi - lo
    lax.fori_loop(0, n // 2, pair, 0)

    @pl.when(n % 2 == 1)
    def _():
        swiglu(hi - 1, 1)


def _ffn(visits, xs, wg, wu, wd, *, tm):
    vt, ve, vlo, vhi, vok = visits
    R, D = xs.shape
    F = wg.shape[2]
    assert R % tm == 0 and F % FFN_TF == 0 and tm % FFN_CHUNK == 0
    nf = F // FFN_TF
    V = vt.shape[0]
    row_mode = {} if tm <= FFN_TM else dict(pipeline_mode=pl.Buffered(1))

    def fidx(v, f, vok):
        return jnp.where(vok[v] == 1, f, nf - 1)

    return pl.pallas_call(
        _ffn_kernel,
        out_shape=jax.ShapeDtypeStruct((R, D), F32),
        grid_spec=pltpu.PrefetchScalarGridSpec(
            num_scalar_prefetch=5,
            grid=(V, nf),
            in_specs=[
                pl.BlockSpec((tm, D), lambda v, f, vt, ve, vlo, vhi, vok: (vt[v], 0), **row_mode),
                pl.BlockSpec((None, D, FFN_TF), lambda v, f, vt, ve, vlo, vhi, vok: (ve[v], 0, fidx(v, f, vok))),
                pl.BlockSpec((None, D, FFN_TF), lambda v, f, vt, ve, vlo, vhi, vok: (ve[v], 0, fidx(v, f, vok))),
                pl.BlockSpec((None, FFN_TF, D), lambda v, f, vt, ve, vlo, vhi, vok: (ve[v], fidx(v, f, vok), 0)),
            ],
            out_specs=pl.BlockSpec((tm, D), lambda v, f, vt, ve, vlo, vhi, vok: (vt[v], 0), **row_mode),
        ),
        compiler_params=_cparams(("arbitrary", "arbitrary")),
        name="ffn",
    )(vt, ve, vlo, vhi, vok, xs, wg, wu, wd)


def _dense_visits(T, layer, tm):
    nt = T // tm
    cpt = tm // FFN_CHUNK
    z = jnp.zeros((nt,), I32)
    return (jnp.arange(nt, dtype=I32), z + layer, z, z + cpt, z + 1)


def _router_kernel(x_ref, r_ref, o_ref):
    logits = jnp.dot(x_ref[...], r_ref[...], preferred_element_type=F32, precision=lax.Precision.HIGHEST)
    lane = lax.broadcasted_iota(I32, logits.shape, 1)
    logits = jnp.where(lane < N_EXPERTS, logits, -jnp.inf)
    v1 = jnp.max(logits, axis=-1, keepdims=True)
    i1 = jnp.min(jnp.where(logits == v1, lane, LANES), axis=-1, keepdims=True)
    rest = jnp.where(lane == i1, -jnp.inf, logits)
    v2 = jnp.max(rest, axis=-1, keepdims=True)
    i2 = jnp.min(jnp.where(rest == v2, lane, LANES), axis=-1, keepdims=True)
    e2 = jnp.exp(v2 - v1)
    g1 = 1.0 / (1.0 + e2)
    g2 = e2 / (1.0 + e2)
    out = jnp.where(lane == 0, i1.astype(F32), 0.0)
    out = jnp.where(lane == 1, i2.astype(F32), out)
    out = jnp.where(lane == 2, g1, out)
    out = jnp.where(lane == 3, g2, out)
    o_ref[...] = out


def _router(x, router, tm=512):
    T, D = x.shape
    tm = min(tm, T)
    rp = jnp.pad(router, ((0, 0), (0, LANES - N_EXPERTS)))
    return pl.pallas_call(
        _router_kernel,
        out_shape=jax.ShapeDtypeStruct((T, LANES), F32),
        grid=(T // tm,),
        in_specs=[pl.BlockSpec((tm, D), lambda i: (i, 0)), pl.BlockSpec((D, LANES), lambda i: (0, 0))],
        out_specs=pl.BlockSpec((tm, LANES), lambda i: (i, 0)),
        compiler_params=_cparams(("arbitrary",)),
        name="router",
    )(x, rp)


GATHER_ROWS = 512


def _row_copy(src_hbm, row, dst, r, sem):
    return pltpu.make_async_copy(src_hbm.at[pl.ds(row, 1), :], dst.at[pl.ds(r, 1), :], sem)


def _prefetch_rows(issue, slot):
    i = pl.program_id(0)

    @pl.when(i == 0)
    def _():
        issue(0, 0)

    @pl.when(i + 1 < pl.num_programs(0))
    def _():
        issue(i + 1, 1 - slot)


def _gather_kernel(tok_ref, x_hbm, o_ref, buf, sem, *, S):
    slot = pl.program_id(0) % 2

    def issue(step, s):
        def body(r2, _):
            for k in range(2):
                r = 2 * r2 + k
                src = pl.ds(pl.multiple_of(tok_ref[step * GATHER_ROWS + r] * S, S), S)
                dst = pl.ds(pl.multiple_of(r * S, S), S)
                pltpu.make_async_copy(x_hbm.at[src, :], buf.at[s, dst, :], sem.at[s]).start(priority=k)
            return 0
        lax.fori_loop(0, GATHER_ROWS // 2, body, 0)

    _prefetch_rows(issue, slot)
    pltpu.make_async_copy(x_hbm.at[pl.ds(0, GATHER_ROWS * S), :], buf.at[slot], sem.at[slot]).wait()
    for s in range(S):
        o_ref[:, s * LANES:(s + 1) * LANES] = buf[slot, pl.ds(s, GATHER_ROWS, stride=S), :].astype(o_ref.dtype)


def _gather_rows(xslab, tok, D):
    R = tok.shape[0]
    S = D // LANES
    assert R % GATHER_ROWS == 0 and xslab.shape[0] >= GATHER_ROWS * S and S % 8 == 0
    return pl.pallas_call(
        functools.partial(_gather_kernel, S=S),
        out_shape=jax.ShapeDtypeStruct((R, D), BF16),
        grid_spec=pltpu.PrefetchScalarGridSpec(
            num_scalar_prefetch=1,
            grid=(R // GATHER_ROWS,),
            in_specs=[pl.BlockSpec(memory_space=pl.ANY)],
            out_specs=pl.BlockSpec((GATHER_ROWS, D), lambda i, tok: (i, 0)),
            scratch_shapes=[pltpu.VMEM((2, GATHER_ROWS * S, LANES), F32), pltpu.SemaphoreType.DMA((2,))],
        ),
        compiler_params=_cparams(("arbitrary",)),
        name="gather_rows",
    )(tok, xslab)


def _combine_kernel(p0_ref, p1_ref, y_hbm, r_ref, x_ref, g_ref, b_ref, o_ref, ob_ref, buf, sem):
    slot = pl.program_id(0) % 2

    def issue(step, s):
        def body(r, _):
            t = step * GATHER_ROWS + r
            _row_copy(y_hbm, p0_ref[t], buf.at[s, 0], r, sem.at[s]).start(priority=0)
            _row_copy(y_hbm, p1_ref[t], buf.at[s, 1], r, sem.at[s]).start(priority=1)
            return 0
        lax.fori_loop(0, GATHER_ROWS, body, 0)

    _prefetch_rows(issue, slot)
    for k in range(TOP_K):
        pltpu.make_async_copy(y_hbm.at[pl.ds(0, GATHER_ROWS), :], buf.at[slot, k], sem.at[slot]).wait()
    route = r_ref[...]
    g0 = route[:, TOP_K:TOP_K + 1]
    g1 = route[:, TOP_K + 1:TOP_K + 2]
    z = ALPHA * x_ref[...] + (g0 * buf[slot, 0] + g1 * buf[slot, 1])
    y = _layer_norm(z, g_ref[...], b_ref[...])
    o_ref[...] = y
    ob_ref[...] = y.astype(BF16)


def _combine_ln(y, p0, p1, route, x, g, b):
    T, D = x.shape
    assert T % GATHER_ROWS == 0 and y.shape[0] >= GATHER_ROWS
    row = pl.BlockSpec((GATHER_ROWS, D), lambda i, p0, p1: (i, 0))
    vec = pl.BlockSpec((1, D), lambda i, p0, p1: (0, 0))
    rspec = pl.BlockSpec((GATHER_ROWS, LANES), lambda i, p0, p1: (i, 0))
    return pl.pallas_call(
        _combine_kernel,
        out_shape=(jax.ShapeDtypeStruct((T, D), F32), jax.ShapeDtypeStruct((T, D), BF16)),
        grid_spec=pltpu.PrefetchScalarGridSpec(
            num_scalar_prefetch=2,
            grid=(T // GATHER_ROWS,),
            in_specs=[pl.BlockSpec(memory_space=pl.ANY), rspec, row, vec, vec],
            out_specs=(row, row),
            scratch_shapes=[pltpu.VMEM((2, TOP_K, GATHER_ROWS, D), F32), pltpu.SemaphoreType.DMA((2,))],
        ),
        compiler_params=_cparams(("arbitrary",)),
        name="combine_ln",
    )(p0, p1, y, route, x, g.reshape(1, D), b.reshape(1, D))


def _moe_plan(route, T, layer, tm):
    E = N_EXPERTS
    e = route[:, :TOP_K].astype(I32)
    ef = e.reshape(-1)
    onehot = (ef[:, None] == jnp.arange(E, dtype=I32)[None, :]).astype(I32)
    rank = jnp.cumsum(onehot, axis=0) - onehot
    counts = jnp.sum(onehot, axis=0)
    chunks = (counts + FFN_CHUNK - 1) // FFN_CHUNK
    seg_end = jnp.cumsum(chunks)
    seg_start = seg_end - chunks
    pos = seg_start[ef] * FFN_CHUNK + jnp.sum(rank * onehot, axis=1)
    R = -(-(TOP_K * T + E * FFN_CHUNK) // tm) * tm
    cpt = tm // FFN_CHUNK
    nt = R // tm
    n_slots = nt + E
    slot = jnp.arange(n_slots, dtype=I32)

    def place(vals, dest):
        return jnp.sum(jnp.where(dest[None, :] == slot[:, None], vals[None, :], 0), axis=1).astype(I32)

    tile_pts = jnp.arange(nt, dtype=I32) * cpt
    seg_pts = seg_start.astype(I32)
    tile_rank = jnp.arange(nt, dtype=I32) + jnp.sum(seg_pts[None, :] < tile_pts[:, None], axis=1)
    seg_rank = jnp.arange(E, dtype=I32) + jnp.sum(tile_pts[None, :] <= seg_pts[:, None], axis=1)
    pts = place(jnp.concatenate([tile_pts, seg_pts]), jnp.concatenate([tile_rank, seg_rank]))
    used = seg_end[-1]
    used_tiles_end = (used + cpt - 1) // cpt * cpt
    nxt = jnp.concatenate([pts[1:], jnp.array([nt * cpt], I32)])
    ok = (pts < used) & (nxt > pts)
    nxt = jnp.where(nxt >= used, jnp.maximum(used_tiles_end, pts), nxt)
    tile = pts // cpt
    nxt = jnp.minimum(nxt, (tile + 1) * cpt)
    exp = jnp.minimum(jnp.sum(seg_end[None, :] <= pts[:, None], axis=1), E - 1).astype(I32)
    last = jnp.max(jnp.where(ok, slot, 0))
    vt = jnp.where(ok, tile, jnp.sum(jnp.where(slot == last, tile, 0)))
    ve = jnp.where(ok, exp, jnp.sum(jnp.where(slot == last, exp, 0)))
    vlo = jnp.where(ok, pts - tile * cpt, 0)
    vhi = jnp.where(ok, nxt - tile * cpt, 0)
    oki = ok.astype(I32)
    dest = jnp.where(ok, jnp.cumsum(oki) - oki, jnp.sum(oki) + jnp.cumsum(1 - oki) - (1 - oki))
    vt, ve, vlo, vhi, vok = (place(a, dest) for a in (vt, ve, vlo, vhi, oki))
    used_tiles = used_tiles_end // cpt
    fill = slot - (n_slots - (nt - used_tiles))
    vt = jnp.where(fill >= 0, used_tiles + fill, vt).astype(I32)
    vok = jnp.where(fill >= 0, 2, vok).astype(I32)
    visits = (vt, (ve + layer * E).astype(I32), vlo, vhi, vok)
    p = pos.reshape(T, TOP_K).astype(I32)
    tok = jnp.zeros((R,), I32).at[pos].set(jnp.arange(TOP_K * T, dtype=I32) // TOP_K)
    return tok, visits, p[:, 0], p[:, 1]


DSA_TK = 512
NT_DIMS = (((1,), (1,)), ((), ()))
DSA_Q_SCALE = (HEAD_DIM ** -0.5) * LOG2_E


DSA_DIGIT = 15


def _srl(x, k):
    return lax.shift_right_logical(x, jnp.full(x.shape, k, x.dtype))


def _dsa_kernel(q_ref, iq_ref, iw_ref, ike_ref, iko_ref, kt_ref, v_ref, o_ref,
                key_s, pk_s, bias_s, s_s, p_s, m_s, l_s, al_s, acc_s, *, ksel, idx_bits):
    n = pl.program_id(0)
    TK = DSA_TK
    nkt = (n * BLOCK + BLOCK + TK - 1) // TK
    qpos = n * BLOCK + lax.broadcasted_iota(I32, (BLOCK, 1), 0)
    lane_k = lax.broadcasted_iota(I32, (1, TK), 1)

    def keys(j):
        return pl.ds(pl.multiple_of(j * TK, TK), TK)

    iqb = iq_ref[...]
    npair = IDX_HEADS // 2
    lhs = jnp.concatenate([iqb[:, p * LANES:(p + 1) * LANES] for p in range(npair)], axis=0)
    wts = iw_ref[...] * ((IDX_DIM ** -0.5) * (IDX_HEADS ** -0.5))

    def score_tile(j, _):
        digits = []
        for part in range(2):
            kr = pl.ds(pl.multiple_of(j * TK + part * half, half), half)
            le = lax.dot_general(lhs, ike_ref[kr, :], NT_DIMS, preferred_element_type=F32)
            lo = lax.dot_general(lhs, iko_ref[kr, :], NT_DIMS, preferred_element_type=F32)
            sc = jnp.zeros((BLOCK, half), F32)
            for p in range(npair):
                sc = sc + jnp.maximum(le[p * BLOCK:(p + 1) * BLOCK], 0.0) * wts[:, 2 * p:2 * p + 1]
                sc = sc + jnp.maximum(lo[p * BLOCK:(p + 1) * BLOCK], 0.0) * wts[:, 2 * p + 1:2 * p + 2]
            bits = pltpu.bitcast(sc, I32)
            key = bits ^ ((bits >> 31) & 0x7FFFFFFF)
            key = jnp.where(j * TK + part * half + lane_k[:, :half] <= qpos, key, INT_MIN)
            key_s[j, :, part * half:(part + 1) * half] = key
            digits.append(_srl(key ^ INT_MIN, 32 - DSA_DIGIT))
        pk_s[j] = digits[0] | jnp.left_shift(digits[1], 16) | guard
        return 0

    half = TK // 2
    guard = jnp.int32(-2147450880)
    ones2 = jnp.int32(0x00010001)

    def pack(d):
        return d[:, :half] | jnp.left_shift(d[:, half:], 16) | guard

    def packed_count(c):
        cc = c | jnp.left_shift(c, 16)

        def body(j, acc):
            hit = _srl(pk_s[j] - cc, DSA_DIGIT) & ones2
            for k in range(half // LANES):
                acc = acc + hit[:, k * LANES:(k + 1) * LANES]
            return acc
        acc = lax.fori_loop(0, nkt, body, jnp.zeros((BLOCK, LANES), I32))
        tot = (acc & 0xFFFF) + _srl(acc, 16)
        return jnp.sum(tot.astype(F32), axis=1, keepdims=True)

    lax.fori_loop(0, nkt, score_tile, 0)

    def count(pred):
        def body(j, acc):
            hit = jnp.where(pred(key_s[j], j * TK + lane_k), 1.0, 0.0)
            for c in range(TK // LANES):
                acc = acc + hit[:, c * LANES:(c + 1) * LANES]
            return acc
        acc = lax.fori_loop(0, nkt, body, jnp.zeros((BLOCK, LANES), F32))
        return jnp.sum(acc, axis=1, keepdims=True)

    searching = (n + 1) * BLOCK > ksel

    def write_bias(thr, pidx):
        def body(j, _):
            key = key_s[j]
            sel = (key > thr) | ((key == thr) & (j * TK + lane_k <= pidx))
            bias_s[j] = jnp.where(sel, 0.0, NEG)
            return 0
        lax.fori_loop(0, nkt, body, 0)

    @pl.when(jnp.logical_not(searching))
    def _():
        write_bias(jnp.full((BLOCK, 1), INT_MIN, I32), jnp.full((BLOCK, 1), -1, I32))

    @pl.when(searching)
    def _():
        def digit_step(i, d, cnt_thr, base):
            c = d | jnp.left_shift(jnp.int32(1), DSA_DIGIT - 1 - i)
            cnt = base + packed_count(c)
            ok = cnt >= ksel
            return jnp.where(ok, c, d), jnp.where(ok, cnt, cnt_thr)

        def digit_search(base, cnt_thr, miss):
            def step(carry):
                i, d, cnt_thr, _ = carry
                d, cnt_thr = digit_step(i, d, cnt_thr, base)
                return i + 1, d, cnt_thr, jnp.max(jnp.abs(cnt_thr - ksel))

            _, d, cnt_thr, miss = lax.while_loop(
                lambda c: jnp.logical_and(c[0] < DSA_DIGIT, c[3] > 0.0), step,
                (jnp.int32(0), jnp.zeros((BLOCK, 1), I32), cnt_thr, miss))
            return d, cnt_thr, miss

        zero = jnp.zeros((BLOCK, 1), F32)
        d_hi, cnt_thr = lax.fori_loop(
            0, DSA_DIGIT, lambda i, c: digit_step(i, c[0], c[1], zero), (jnp.zeros((BLOCK, 1), I32), zero - 1.0))
        miss = jnp.max(jnp.abs(cnt_thr - ksel))

        top = (1 << DSA_DIGIT) - 1
        above = jnp.where(d_hi == top, 0.0, packed_count(jnp.minimum(d_hi + 1, top)))

        @pl.when(miss > 0.0)
        def _():
            def body(j, _):
                u = key_s[j] ^ INT_MIN
                mid = _srl(u, 32 - 2 * DSA_DIGIT) & top
                pk_s[j] = pack(jnp.where(_srl(u, 32 - DSA_DIGIT) == d_hi, mid, 0))
                return 0
            lax.fori_loop(0, nkt, body, 0)

        d_mid, cnt_thr, miss = digit_search(above, cnt_thr, miss)
        thr = (jnp.left_shift(d_hi, 32 - DSA_DIGIT) | jnp.left_shift(d_mid, 32 - 2 * DSA_DIGIT)) ^ INT_MIN

        def bit_step(carry):
            i, thr, cnt_thr, _ = carry
            cand = thr + jnp.left_shift(jnp.int32(1), 31 - 2 * DSA_DIGIT - i)
            cnt = count(lambda key, kidx: key >= cand)
            ok = cnt >= ksel
            cnt_thr = jnp.where(ok, cnt, cnt_thr)
            return i + 1, jnp.where(ok, cand, thr), cnt_thr, jnp.max(jnp.abs(cnt_thr - ksel))

        _, thr, _, miss = lax.while_loop(
            lambda c: jnp.logical_and(c[0] < 32 - 2 * DSA_DIGIT, c[3] > 0.0), bit_step,
            (jnp.int32(0), thr, cnt_thr, miss))
        exact = miss == 0.0

        @pl.when(exact)
        def _():
            write_bias(thr, jnp.full((BLOCK, 1), 2 ** 30, I32))

        @pl.when(jnp.logical_not(exact))
        def _():
            need = ksel - count(lambda key, kidx: key > thr)

            def idx_step(i, pidx):
                cand = pidx + jnp.left_shift(jnp.int32(1), idx_bits - 1 - i)
                c = count(lambda key, kidx: (key == thr) & (kidx < cand))
                return jnp.where(c < need, cand, pidx)

            pidx = lax.fori_loop(0, idx_bits, idx_step, jnp.zeros((BLOCK, 1), I32))
            write_bias(thr, pidx)

    qb = q_ref[...]
    qs = jnp.concatenate([qb[:, h * LANES:(h + 1) * LANES] for h in range(DSA_HEADS)], axis=0)
    ncol = TK // LANES
    m_s[...] = jnp.full(m_s.shape, NEG, F32)
    l_s[...] = jnp.zeros(l_s.shape, F32)
    acc_s[...] = jnp.zeros(acc_s.shape, F32)

    def attend(j, _):
        s_s[...] = lax.dot_general(qs, kt_ref[keys(j), :], NT_DIMS, preferred_element_type=F32)
        for h in range(DSA_HEADS):
            r = slice(h * BLOCK, (h + 1) * BLOCK)
            t = [s_s[r, c * LANES:(c + 1) * LANES] + bias_s[j, :, c * LANES:(c + 1) * LANES] for c in range(ncol)]
            mx = t[0]
            for c in range(1, ncol):
                mx = jnp.maximum(mx, t[c])
            m_old = m_s[r, :]
            m_new = jnp.maximum(m_old, jnp.max(mx, axis=-1, keepdims=True))
            alpha = jnp.exp2(m_old - m_new)
            psum = jnp.zeros((BLOCK, LANES), F32)
            for c in range(ncol):
                p = jnp.exp2(t[c] - m_new)
                psum = psum + p
                p_s[r, c * LANES:(c + 1) * LANES] = p.astype(BF16)
            m_s[r, :] = m_new
            l_s[r, :] = alpha * l_s[r, :] + jnp.sum(psum, axis=-1, keepdims=True)
            al_s[r, :] = alpha
        acc_s[...] = al_s[...] * acc_s[...] + jnp.dot(p_s[...], v_ref[keys(j), :], preferred_element_type=F32)
        return 0

    lax.fori_loop(0, nkt, attend, 0)
    for h in range(DSA_HEADS):
        r = slice(h * BLOCK, (h + 1) * BLOCK)
        o_ref[:, h * LANES:(h + 1) * LANES] = (acc_s[r, :] / l_s[r, :]).astype(o_ref.dtype)


DSA_SIDE_IQ = IDX_HEADS * IDX_DIM
DSA_SIDE_COLS = DSA_SIDE_IQ + 4 * LANES


def _dsa_side_weights(w_in_l):
    o = DSA_Q
    wk = w_in_l[:, o:o + HEAD_DIM]; o += HEAD_DIM
    wv = w_in_l[:, o:o + HEAD_DIM]; o += HEAD_DIM
    wiq = w_in_l[:, o:o + DSA_SIDE_IQ]; o += DSA_SIDE_IQ
    wik = w_in_l[:, o:o + IDX_DIM]; o += IDX_DIM
    wiw = w_in_l[:, o:o + IDX_HEADS]
    z = jnp.zeros_like(wik)
    side = jnp.concatenate([wiq, wk, wv, wik, z, z, wik], axis=1)
    return side, jnp.pad(wiw, ((0, 0), (0, LANES - IDX_HEADS)))


def _dsa(q, side, iw):
    T = q.shape[0]
    TK = DSA_TK
    assert T % TK == 0 and TK % BLOCK == 0 and HEAD_DIM == LANES and 2 * IDX_DIM == LANES
    assert DSA_SIDE_IQ % LANES == 0
    ksel = min(DSA_TOPK, T // 4)
    assert ksel % BLOCK == 0
    nt = T // TK
    rows = DSA_HEADS * BLOCK
    g0 = DSA_SIDE_IQ // LANES

    def key_side(group):
        return pl.BlockSpec((T, LANES), lambda n: (0, group))

    return pl.pallas_call(
        functools.partial(_dsa_kernel, ksel=ksel, idx_bits=max(1, (T - 1).bit_length())),
        out_shape=jax.ShapeDtypeStruct((T, DSA_Q), BF16),
        grid=(T // BLOCK,),
        in_specs=[
            pl.BlockSpec((BLOCK, DSA_Q), lambda n: (n, 0)),
            pl.BlockSpec((BLOCK, DSA_SIDE_IQ), lambda n: (n, 0)),
            pl.BlockSpec((BLOCK, LANES), lambda n: (n, 0)),
            key_side(g0 + 2), key_side(g0 + 3), key_side(g0), key_side(g0 + 1),
        ],
        out_specs=pl.BlockSpec((BLOCK, DSA_Q), lambda n: (n, 0)),
        scratch_shapes=[
            pltpu.VMEM((nt, BLOCK, TK), I32),
            pltpu.VMEM((nt, BLOCK, TK // 2), I32),
            pltpu.VMEM((nt, BLOCK, TK), F32),
            pltpu.VMEM((rows, TK), F32),
            pltpu.VMEM((rows, TK), BF16),
            pltpu.VMEM((rows, LANES), F32),
            pltpu.VMEM((rows, LANES), F32),
            pltpu.VMEM((rows, LANES), F32),
            pltpu.VMEM((rows, HEAD_DIM), F32),
        ],
        compiler_params=_cparams(("arbitrary",)),
        name="dsa",
    )(q, side, iw, side, side, side, side)


def _even_layer(x, xb, rope, i, w_in, pool_w, pool_scale, w_out, ln1_g, ln1_b, wg, wu, wd, ln2_g, ln2_b):
    T = x.shape[0]
    cos2, sin2 = rope
    u = _proj(xb, w_in, i, cos2, sin2, col0=0, ncols=POOL_DIM, tn=1024, rope=(0, 0), out_dtype=F32)
    qkv = _proj(xb, w_in, i, cos2, sin2, col0=POOL_DIM, ncols=3 * DIL_QKV, tn=1024, rope=(0, 2 * DIL_QKV),
                out_dtype=F32, head_major=True)
    pooled = _pool(u, pool_w, i, pool_scale[i])
    attn = _dilated(qkv)
    x1, x1b = _out_ln([pooled, attn], w_out[i].astype(BF16), x, ln1_g[i], ln1_b[i])
    tm = min(FFN_TM, T)
    y = _ffn(_dense_visits(T, i, tm), x1b, wg, wu, wd, tm=tm)
    return _res_ln(x1, y, ln2_g[i], ln2_b[i])


def _odd_layer(x, xb, rope, i, w_in, w_out, ln1_g, ln1_b, router, wg, wu, wd, ln2_g, ln2_b):
    T, D = x.shape
    cos2, sin2 = rope
    q = _proj(xb, w_in, i, cos2, sin2, col0=0, ncols=DSA_Q, tn=1024, rope=(0, DSA_Q), out_dtype=BF16,
              out_scale=DSA_Q_SCALE)
    w_side, w_iw = _dsa_side_weights(w_in[i])
    side = _proj(xb, w_side[None], 0, cos2, sin2, col0=0, ncols=DSA_SIDE_COLS, tn=DSA_SIDE_COLS,
                 rope=(DSA_SIDE_IQ, DSA_SIDE_IQ + HEAD_DIM), out_dtype=BF16)
    iw = _proj(xb, w_iw[None], 0, cos2, sin2, col0=0, ncols=LANES, tn=LANES, rope=(0, 0), out_dtype=F32)
    attn = _dsa(q, side, iw)
    x1, x1_slabs = _out_ln([attn], w_out[i].astype(BF16), x, ln1_g[i], ln1_b[i], slabs=True)
    route = _router(x1, router[i])
    tok, visits, p0, p1 = _moe_plan(route, T, i, FFN_TM_MOE)
    xs = _gather_rows(x1_slabs, tok, D)
    F = wg.shape[3]
    y = _ffn(visits, xs, wg.reshape(-1, D, F), wu.reshape(-1, D, F), wd.reshape(-1, F, D), tm=FFN_TM_MOE)
    return _combine_ln(y, p0, p1, route, x1, ln2_g[i], ln2_b[i])


def kernel(x, ev_w_in, ev_pool_w, ev_pool_scale, ev_w_out, ev_ln1_g, ev_ln1_b, ev_ffn_w_gate, ev_ffn_w_up,
           ev_ffn_w_down, ev_ln2_g, ev_ln2_b, od_w_in, od_w_out, od_ln1_g, od_ln1_b, od_router,
           od_exp_w_gate, od_exp_w_up, od_exp_w_down, od_ln2_g, od_ln2_b):
    B, T, D = x.shape
    rope = _rope_tables(T)
    outs = []
    for bi in range(B):
        h = x[bi]
        hb = h.astype(BF16)
        for layer in range(DEPTH):
            i = layer // 2
            if layer % 2 == 0:
                h, hb = _even_layer(h, hb, rope, i, ev_w_in, ev_pool_w, ev_pool_scale, ev_w_out,
                                    ev_ln1_g, ev_ln1_b, ev_ffn_w_gate, ev_ffn_w_up,
                                    ev_ffn_w_down, ev_ln2_g, ev_ln2_b)
            else:
                h, hb = _odd_layer(h, hb, rope, i, od_w_in, od_w_out, od_ln1_g, od_ln1_b,
                                   od_router, od_exp_w_gate, od_exp_w_up, od_exp_w_down,
                                   od_ln2_g, od_ln2_b)
        outs.append(h)
    return jnp.stack(outs, axis=0)
```
